```python
import math
import jax, jax.numpy as jnp
from jax import lax
import numpy as np

D_MODEL = 2048
BATCH = 2
SEQ = 4096
DEPTH = 2
DEC_BATCH = 32
DEC_SEQ = 1
PAST_LEN = 16384
PAGE_SIZE = 128

MIX_DIM = D_MODEL
CONV_DIM = MIX_DIM // 4
CONV_W = 3
HG_DIM = MIX_DIM // 4
HG_HEADS = 4
HG_DK = HG_DIM // HG_HEADS
HG_CHUNK = 16
HEAD_DIM = 64
ATTN_DIM = MIX_DIM - CONV_DIM - HG_DIM
N_HEADS = ATTN_DIM // HEAD_DIM
N_KV_HEADS = N_HEADS // 4
Q_PER_KV = N_HEADS // N_KV_HEADS
KV_DIM = N_KV_HEADS * HEAD_DIM
WINDOW = 128
N_GROUPS = 4
EXPERTS_PER_GROUP = 4
N_EXPERTS = N_GROUPS * EXPERTS_PER_GROUP
TOP_K = 2
D_EXPERT = D_MODEL // 4
ALPHA = (2 * DEPTH) ** 0.25
BETA = (8 * DEPTH) ** -0.25
LN_EPS = 1e-5
RMS_EPS = 1e-6
SPLIT_SIZES = (CONV_DIM, CONV_DIM, CONV_DIM, HG_DIM, HG_DIM, HG_DIM, HG_DIM, ATTN_DIM, KV_DIM, KV_DIM)
IN_DIM = sum(SPLIT_SIZES)

kernel_name = "hymba_hybrid_conv_hgrn2_swa_hmoe_step"

F32 = jnp.float32


def layer_norm(x, g, b):
    xf = x.astype(F32)
    mu = jnp.mean(xf, -1, keepdims=True)
    var = jnp.mean(jnp.square(xf - mu), -1, keepdims=True)
    return ((xf - mu) * lax.rsqrt(var + LN_EPS) * g + b).astype(x.dtype)


def alibi_slopes():
    h = jnp.arange(1, N_HEADS + 1, dtype=F32)
    return jnp.exp2(-8.0 * h / N_HEADS).reshape(N_KV_HEADS, Q_PER_KV)


def causal_short_conv(u, hist, w):
    ext = jnp.concatenate([hist.astype(u.dtype), u], axis=1)
    L = u.shape[1]
    y = w[0] * ext[:, 0:L]
    for j in range(1, CONV_W):
        y = y + w[j] * ext[:, j:j + L]
    return y, ext[:, ext.shape[1] - (CONV_W - 1):]


def hgrn2_recurrence(q, k, v, logf, s0, chunk):
    b_, l_, h_, _ = q.shape
    dv = v.shape[-1]
    n = l_ // chunk
    blk = lambda a: a.astype(F32).reshape(b_, n, chunk, h_, a.shape[-1])
    q, k, v, g = blk(q), blk(k), blk(v), blk(logf)
    cum = jnp.cumsum(g, axis=2)
    causal = jnp.tril(jnp.ones((chunk, chunk), dtype=bool))
    diff = cum[:, :, :, None] - cum[:, :, None, :]
    decay = jnp.exp(jnp.where(causal[:, :, None, None], diff, -jnp.inf))
    scores = jnp.einsum('bnthd,bnshd,bntshd->bnhts', q, k, decay)
    o_intra = jnp.einsum('bnhts,bnshv->bnthv', scores, v)
    last = cum[:, :, -1]
    k_to_end = k * jnp.exp(last[:, :, None] - cum)
    u = jnp.einsum('bnshd,bnshv->bnhdv', k_to_end, v)

    def step(state, inp):
        a, du = inp
        return jnp.exp(a)[..., None] * state + du, state

    s_final, s_before = lax.scan(step, s0.astype(F32),
                                 (jnp.moveaxis(last, 1, 0), jnp.moveaxis(u, 1, 0)))
    o_inter = jnp.einsum('bnthd,nbhdv->bnthv', q * jnp.exp(cum), s_before)
    return (o_intra + o_inter).reshape(b_, l_, h_, dv), s_final


def sink_attention(q, k, v, dist, valid, sinks):
    slopes = alibi_slopes()
    s = jnp.einsum('bnqkgd,bnskd->bnkgqs', q.astype(F32), k.astype(F32)) * (HEAD_DIM ** -0.5)
    s = s - slopes[:, :, None, None] * dist.astype(F32)
    s = jnp.where(valid[None, :, None, None], s, -jnp.inf)
    sink = sinks.astype(F32)[:, :, None, None]
    m = jnp.maximum(s.max(-1, keepdims=True), sink)
    p = jnp.exp(s - m)
    denom = p.sum(-1, keepdims=True) + jnp.exp(sink - m)
    return jnp.einsum('bnkgqs,bnskd->bnqkgd', p / denom, v.astype(F32))


def token_mixers(x, conv_hist, s0, kv_hist, lb, w_in, w_out, conv_w, hg_gain, sinks, hg_chunk):
    bsz, L, _ = x.shape
    proj = x @ w_in
    cb, cc, ch, hq, hf, hi, hgt, aq, ak, av = jnp.split(
        proj, np.cumsum(SPLIT_SIZES)[:-1].tolist(), axis=-1)
    conv_y, conv_state = causal_short_conv(cc * ch, conv_hist, conv_w)
    y_a = cb * conv_y
    f = lb + (1.0 - lb) * jax.nn.sigmoid(hf.astype(F32))
    heads = lambda a: a.reshape(bsz, L, HG_HEADS, HG_DK)
    o, s_new = hgrn2_recurrence(heads(hq.astype(F32)) * (HG_DK ** -0.5), heads(1.0 - f),
                                heads(hi), heads(jnp.log(f)), s0, hg_chunk)
    o = o * lax.rsqrt(jnp.mean(o * o, -1, keepdims=True) + RMS_EPS) * hg_gain
    y_b = (o * jax.nn.silu(heads(hgt).astype(F32))).reshape(bsz, L, HG_DIM).astype(x.dtype)
    q = aq.reshape(bsz, L, N_KV_HEADS, Q_PER_KV, HEAD_DIM)
    k = ak.reshape(bsz, L, N_KV_HEADS, HEAD_DIM)
    v = av.reshape(bsz, L, N_KV_HEADS, HEAD_DIM)
    sinks = sinks.reshape(N_KV_HEADS, Q_PER_KV)
    if kv_hist is None:
        nb = L // WINDOW
        qb = q.reshape(bsz, nb, WINDOW, N_KV_HEADS, Q_PER_KV, HEAD_DIM)
        kb = k.reshape(bsz, nb, WINDOW, N_KV_HEADS, HEAD_DIM)
        vb = v.reshape(bsz, nb, WINDOW, N_KV_HEADS, HEAD_DIM)
        pad = ((0, 0), (1, 0), (0, 0), (0, 0), (0, 0))
        kk = jnp.concatenate([jnp.pad(kb[:, :-1], pad), kb], axis=2)
        vv = jnp.concatenate([jnp.pad(vb[:, :-1], pad), vb], axis=2)
        i = jnp.arange(WINDOW)[:, None]
        j = jnp.arange(2 * WINDOW)[None, :]
        dist = i + WINDOW - j
        key_pos = jnp.arange(nb)[:, None, None] * WINDOW - WINDOW + j[None]
        valid = (dist >= 0) & (dist < WINDOW) & (key_pos >= 0)
        o_c = sink_attention(qb, kk, vv, dist, valid, sinks)
        w_buf = min(WINDOW, PAST_LEN)
        k_state, v_state = k[:, L - w_buf:], v[:, L - w_buf:]
    else:
        k_cache, v_cache = kv_hist
        w_buf = k_cache.shape[1]
        kk = jnp.concatenate([k_cache.astype(k.dtype), k], axis=1)
        vv = jnp.concatenate([v_cache.astype(v.dtype), v], axis=1)
        i = jnp.arange(L)[:, None]
        j = jnp.arange(w_buf + L)[None, :]
        dist = i + w_buf - j
        valid = ((dist >= 0) & (dist < WINDOW))[None]
        o_c = sink_attention(q[:, None], kk[:, None], vv[:, None], dist, valid, sinks)[:, 0]
        k_state, v_state = kk[:, L:], vv[:, L:]
    y_c = o_c.reshape(bsz, L, ATTN_DIM).astype(x.dtype)
    y = jnp.concatenate([y_a, y_b, y_c], axis=-1) @ w_out
    return y, conv_state, s_new.astype(x.dtype), k_state, v_state


def hier_moe(x, w_group, b_group, w_router, b_router, w_gate, w_up, w_down):
    shp = x.shape
    xf = x.reshape(-1, D_MODEL)
    t = xf.shape[0]
    g_logits = (xf @ w_group).astype(F32) + b_group
    g_prob = jax.nn.softmax(g_logits, axis=-1)
    _, g_idx = lax.top_k(g_logits, 1)
    e_logits = ((xf @ w_router).astype(F32) + b_router).reshape(t, N_GROUPS, EXPERTS_PER_GROUP)
    e_in_group = jnp.take_along_axis(e_logits, g_idx[:, :, None], axis=1)[:, 0]
    e_val, e_idx = lax.top_k(e_in_group, TOP_K)
    weights = jax.nn.softmax(e_val, axis=-1) * jnp.take_along_axis(g_prob, g_idx, axis=1)
    expert = g_idx * EXPERTS_PER_GROUP + e_idx
    gate = jnp.sum(jax.nn.one_hot(expert, N_EXPERTS, dtype=F32) * weights[..., None], axis=1)
    h = jax.nn.silu(jnp.einsum('td,edf->tef', xf, w_gate)) * jnp.einsum('td,edf->tef', xf, w_up)
    y = jnp.einsum('tef,te,efd->td', h, gate.astype(h.dtype), w_down)
    return y.reshape(shp)


def setup_inputs(seed: int = 0) -> dict:
    key = jax.random.key(seed)
    ks = jax.random.split(key, 24)
    nrm = lambda k, shape, scale: jax.random.normal(k, shape, F32) * scale
    w_buf = min(WINDOW, PAST_LEN)
    return {
        "x_prompt": nrm(ks[0], (BATCH, SEQ, D_MODEL), 1.0),
        "x_sample": nrm(ks[1], (DEC_BATCH, DEC_SEQ, D_MODEL), 1.0),
        "state_conv": nrm(ks[2], (DEPTH, DEC_BATCH, CONV_W - 1, CONV_DIM), 1.0),
        "state_hgrn": nrm(ks[3], (DEPTH, DEC_BATCH, HG_HEADS, HG_DK, HG_DK), 0.5),
        "cache_k_win": nrm(ks[4], (DEPTH, DEC_BATCH, w_buf, N_KV_HEADS, HEAD_DIM), 1.0),
        "cache_v_win": nrm(ks[5], (DEPTH, DEC_BATCH, w_buf, N_KV_HEADS, HEAD_DIM), 1.0),
        "w_in": nrm(ks[6], (DEPTH, D_MODEL, IN_DIM), D_MODEL ** -0.5),
        "w_out": nrm(ks[7], (DEPTH, MIX_DIM, D_MODEL), BETA * MIX_DIM ** -0.5),
        "conv_w": nrm(ks[8], (DEPTH, CONV_W, CONV_DIM), CONV_W ** -0.5),
        "hg_lb_param": nrm(ks[9], (DEPTH, HG_DIM), 1.0),
        "hg_gain": 1.0 + nrm(ks[10], (DEPTH, HG_DK), 0.02),
        "attn_sinks": nrm(ks[11], (DEPTH, N_HEADS), 1.0),
        "ln1_g": 1.0 + nrm(ks[12], (DEPTH, D_MODEL), 0.02),
        "ln1_b": nrm(ks[13], (DEPTH, D_MODEL), 0.02),
        "w_group": nrm(ks[14], (DEPTH, D_MODEL, N_GROUPS), D_MODEL ** -0.5),
        "b_group": nrm(ks[15], (DEPTH, N_GROUPS), 0.01),
        "w_router": nrm(ks[16], (DEPTH, D_MODEL, N_EXPERTS), D_MODEL ** -0.5),
        "b_router": nrm(ks[17], (DEPTH, N_EXPERTS), 0.01),
        "w_gate": nrm(ks[18], (DEPTH, N_EXPERTS, D_MODEL, D_EXPERT), D_MODEL ** -0.5),
        "w_up": nrm(ks[19], (DEPTH, N_EXPERTS, D_MODEL, D_EXPERT), D_MODEL ** -0.5),
        "w_down": nrm(ks[20], (DEPTH, N_EXPERTS, D_EXPERT, D_MODEL), BETA * D_EXPERT ** -0.5),
        "ln2_g": 1.0 + nrm(ks[21], (DEPTH, D_MODEL), 0.02),
        "ln2_b": nrm(ks[22], (DEPTH, D_MODEL), 0.02),
    }


def reference(x_prompt, x_sample, state_conv, state_hgrn, cache_k_win, cache_v_win,
              w_in, w_out, conv_w, hg_lb_param, hg_gain, attn_sinks, ln1_g, ln1_b,
              w_group, b_group, w_router, b_router, w_gate, w_up, w_down, ln2_g, ln2_b):
    lb_soft = jax.nn.softmax(hg_lb_param.astype(F32), axis=0)
    lb_all = jnp.cumsum(lb_soft, axis=0) - lb_soft[0]

    def layer(x, conv_hist, s0, kv_hist, hg_chunk, l):
        mix, cs, ss, ks_, vs_ = token_mixers(x, conv_hist, s0, kv_hist, lb_all[l], w_in[l], w_out[l],
                                             conv_w[l], hg_gain[l], attn_sinks[l], hg_chunk)
        x = layer_norm(ALPHA * x + mix, ln1_g[l], ln1_b[l])
        ffn = hier_moe(x, w_group[l], b_group[l], w_router[l], b_router[l],
                       w_gate[l], w_up[l], w_down[l])
        x = layer_norm(ALPHA * x + ffn, ln2_g[l], ln2_b[l])
        return x, cs, ss, ks_, vs_

    yp, ys = x_prompt, x_sample
    bp = x_prompt.shape[0]
    conv_p, conv_s, hg_p, hg_s, k_p, k_s, v_p, v_s = [], [], [], [], [], [], [], []
    for l in range(DEPTH):
        zero_conv = jnp.zeros((bp, CONV_W - 1, CONV_DIM), x_prompt.dtype)
        zero_s = jnp.zeros((bp, HG_HEADS, HG_DK, HG_DK), F32)
        yp, cs, ss, kst, vst = layer(yp, zero_conv, zero_s, None, HG_CHUNK, l)
        conv_p.append(cs); hg_p.append(ss); k_p.append(kst); v_p.append(vst)
        ys, cs, ss, kst, vst = layer(ys, state_conv[l], state_hgrn[l],
                                     (cache_k_win[l], cache_v_win[l]), x_sample.shape[1], l)
        conv_s.append(cs); hg_s.append(ss); k_s.append(kst); v_s.append(vst)

    return (yp, ys, jnp.stack(conv_p), jnp.stack(conv_s), jnp.stack(hg_p), jnp.stack(hg_s),
            jnp.stack(k_p), jnp.stack(k_s), jnp.stack(v_p), jnp.stack(v_s))
```

```python
import functools

import jax
import jax.numpy as jnp
import numpy as np
from jax import lax
from jax.experimental import pallas as pl
from jax.experimental.pallas import tpu as pltpu

F32 = jnp.float32
BF16 = jnp.bfloat16
I32 = jnp.int32

D_MODEL = 2048
DEPTH = 2
CONV_DIM = 512
CONV_W = 3
HG_DIM = 512
HG_HEADS = 4
HG_DK = 128
HG_CHUNK = 16
HEAD_DIM = 64
ATTN_DIM = 1024
N_HEADS = 16
N_KV_HEADS = 4
Q_PER_KV = 4
KV_DIM = 256
WINDOW = 128
N_GROUPS = 4
EXPERTS_PER_GROUP = 4
N_EXPERTS = 16
TOP_K = 2
D_EXPERT = 512
ALPHA = (2 * DEPTH) ** 0.25
LN_EPS = 1e-5
RMS_EPS = 1e-6
IN_DIM = 5120
OFF_CB, OFF_CC, OFF_CH = 0, 512, 1024
OFF_HQ, OFF_HF, OFF_HI, OFF_HG = 1536, 2048, 2560, 3072
OFF_AQ, OFF_AK, OFF_AV = 3584, 4608, 4864

VMEM_LIMIT_BYTES = 56 * 1024 * 1024
LANES = 128
SUBLANES = 8

EXPERT_TILE = 256
ROUTE_ROWS = 32


def _cparams(*sem):
    return pltpu.CompilerParams(dimension_semantics=sem,
                                vmem_limit_bytes=VMEM_LIMIT_BYTES)


def _dot(a, b):
    return jnp.dot(a, b, preferred_element_type=F32)


def _dot_nt(a, b):
    return lax.dot_general(a, b, (((1,), (1,)), ((), ())),
                           preferred_element_type=F32)


def _split(x):
    hi = x.astype(BF16)
    lo = (x - hi.astype(F32)).astype(BF16)
    return hi, lo


def _dot3(a, b):
    ah, al = _split(a)
    bh, bl = _split(b)
    return _dot(ah, bh) + (_dot(ah, bl) + _dot(al, bh))


def _dot3_nt(a, b):
    ah, al = _split(a)
    bh, bl = _split(b)
    return _dot_nt(ah, bh) + (_dot_nt(ah, bl) + _dot_nt(al, bh))


def _sigmoid(x):
    return 1.0 / (1.0 + jnp.exp(-x))


def _silu(x):
    return x * _sigmoid(x)


def _col_from_row(row):
    n = row.shape[1]
    eye = (lax.broadcasted_iota(I32, (n, n), 0)
           == lax.broadcasted_iota(I32, (n, n), 1))
    return jnp.sum(jnp.where(eye, row, 0.0), axis=1, keepdims=True)


def _hg_lower_bound(lbp, layer):
    m = jnp.max(lbp, axis=0, keepdims=True)
    e = jnp.exp(lbp - m)
    soft = e / jnp.sum(e, axis=0, keepdims=True)
    acc = soft[0:1]
    for i in range(1, layer + 1):
        acc = acc + soft[i:i + 1]
    return acc - soft[0:1]


def _alibi_slope(head):
    return float(2.0 ** (-8.0 * (head + 1) / N_HEADS))


def _inproj_kernel(x_ref, w_ref, o_ref, xb_ref):
    @pl.when(pl.program_id(1) == 0)
    def _():
        xb_ref[...] = x_ref[...].astype(BF16)

    o_ref[...] = _dot(xb_ref[...], w_ref[...].astype(BF16))


def _inproj(x, w, *, tm=1024, tn=1024):
    m, k = x.shape
    n = w.shape[1]
    tm = min(tm, m)
    return pl.pallas_call(
        _inproj_kernel,
        out_shape=jax.ShapeDtypeStruct((m, n), F32),
        grid=(m // tm, n // tn),
        in_specs=[pl.BlockSpec((tm, k), lambda i, j: (i, 0)),
                  pl.BlockSpec((k, tn), lambda i, j: (0, j))],
        out_specs=pl.BlockSpec((tm, tn), lambda i, j: (i, j)),
        scratch_shapes=[pltpu.VMEM((tm, k), BF16)],
        compiler_params=_cparams("arbitrary", "arbitrary"),
        name="inproj",
    )(x, w)


def _inproj3_kernel(x_ref, w_ref, o_ref):
    o_ref[...] = _dot3(x_ref[...], w_ref[...])


def _inproj3(x, w, *, tn=512):
    m, k = x.shape
    n = w.shape[1]
    return pl.pallas_call(
        _inproj3_kernel,
        out_shape=jax.ShapeDtypeStruct((m, n), F32),
        grid=(n // tn,),
        in_specs=[pl.BlockSpec((m, k), lambda j: (0, 0)),
                  pl.BlockSpec((k, tn), lambda j: (0, j))],
        out_specs=pl.BlockSpec((m, tn), lambda j: (0, j)),
        compiler_params=_cparams("arbitrary"),
        name="inproj_sample",
    )(x, w)


def _conv_kernel(cb_ref, cc_ref, ch_ref, w_ref, y_ref, tail_ref, carry_ref):
    i = pl.program_id(1)

    @pl.when(i == 0)
    def _():
        carry_ref[...] = jnp.zeros_like(carry_ref)

    u = cc_ref[...] * ch_ref[...]
    tl = u.shape[0]
    row = lax.broadcasted_iota(I32, u.shape, 0)
    prev1 = carry_ref[SUBLANES - 1:SUBLANES, :]
    prev2 = carry_ref[SUBLANES - 2:SUBLANES - 1, :]
    u1 = jnp.where(row == 0, prev1, pltpu.roll(u, 1, 0))
    u2 = jnp.where(row == 0, prev2, jnp.where(row == 1, prev1, pltpu.roll(u, 2, 0)))
    w = w_ref[...]
    y = w[0:1] * u2 + w[1:2] * u1 + w[2:3] * u
    y_ref[...] = (cb_ref[...] * y).astype(y_ref.dtype)
    tail = u[tl - SUBLANES:tl, :]
    carry_ref[...] = tail
    tail_ref[...] = tail


def _conv_prompt(proj, conv_w, bsz, seq, *, tl=512):
    tl = min(tl, seq)
    nt = seq // tl
    cblk = lambda c: pl.BlockSpec((tl, CONV_DIM), lambda b, i, c=c: (b * nt + i, c))
    return pl.pallas_call(
        _conv_kernel,
        out_shape=(jax.ShapeDtypeStruct((bsz * seq, CONV_DIM), BF16),
                   jax.ShapeDtypeStruct((bsz, SUBLANES, CONV_DIM), F32)),
        grid=(bsz, nt),
        in_specs=[cblk(OFF_CB // CONV_DIM), cblk(OFF_CC // CONV_DIM),
                  cblk(OFF_CH // CONV_DIM),
                  pl.BlockSpec((CONV_W, CONV_DIM), lambda b, i: (0, 0))],
        out_specs=(pl.BlockSpec((tl, CONV_DIM), lambda b, i: (b * nt + i, 0)),
                   pl.BlockSpec((None, SUBLANES, CONV_DIM), lambda b, i: (b, 0, 0))),
        scratch_shapes=[pltpu.VMEM((SUBLANES, CONV_DIM), F32)],
        compiler_params=_cparams("arbitrary", "arbitrary"),
        name="conv_prompt",
    )(proj, proj, proj, conv_w)


def _seg_cumsum(g, seg):
    row = lax.broadcasted_iota(I32, g.shape, 0) % seg
    s = 1
    while s < seg:
        g = g + jnp.where(row >= s, pltpu.roll(g, s, 0), 0.0)
        s *= 2
    return g


def _hgrn_kernel(hq_ref, hf_ref, hi_ref, hg_ref, lbp_ref, gain_ref,
                 y_ref, st_ref, cum_ref, k_ref, q_ref, o_ref, s_ref, *, layer):
    i = pl.program_id(1)
    nt = pl.num_programs(1)
    tb = hq_ref.shape[0]
    c = HG_CHUNK

    @pl.when(i == 0)
    def _():
        s_ref[...] = jnp.zeros_like(s_ref)

    lb = _hg_lower_bound(lbp_ref[...], layer)
    f = lb + (1.0 - lb) * _sigmoid(hf_ref[...])
    cum_ref[...] = _seg_cumsum(jnp.log(f), c)
    k_ref[...] = 1.0 - f
    q_ref[...] = hq_ref[...] * (HG_DK ** -0.5)

    ones = jnp.ones((HG_DK, HG_DK), BF16)
    trow = lax.broadcasted_iota(I32, (c, HG_DK), 0)

    def chunk(ci, carry):
        r0 = pl.multiple_of(ci * c, c)
        for h in range(HG_HEADS):
            cols = slice(h * HG_DK, (h + 1) * HG_DK)
            cum = cum_ref[pl.ds(r0, c), cols]
            kc = k_ref[pl.ds(r0, c), cols]
            qc = q_ref[pl.ds(r0, c), cols]
            vc = hi_ref[pl.ds(r0, c), cols]
            parts = []
            for s in range(c):
                dec = jnp.exp(jnp.minimum(cum - cum[s:s + 1], 0.0))
                parts.append((qc * kc[s:s + 1] * dec).astype(BF16))
            sc = _dot(jnp.concatenate(parts, axis=0), ones)
            o = jnp.zeros((c, HG_DK), F32)
            for s in range(c):
                o = o + jnp.where(trow >= s, sc[s * c:(s + 1) * c], 0.0) * vc[s:s + 1]
            st = s_ref[h]
            last = cum[c - 1:c]
            o = o + _dot_nt((qc * jnp.exp(cum)).astype(BF16), st.astype(BF16))
            kte = (kc * jnp.exp(last - cum)).astype(BF16)
            du = lax.dot_general(vc.astype(BF16), kte, (((0,), (0,)), ((), ())),
                                 preferred_element_type=F32)
            s_ref[h] = jnp.exp(last) * st + du
            o_ref[pl.ds(r0, c), cols] = o
        return carry

    lax.fori_loop(0, tb // c, chunk, 0)

    gain = gain_ref[...]
    outs = []
    for h in range(HG_HEADS):
        cols = slice(h * HG_DK, (h + 1) * HG_DK)
        o = o_ref[:, cols]
        o = o * lax.rsqrt(jnp.mean(o * o, axis=-1, keepdims=True) + RMS_EPS) * gain
        outs.append(o * _silu(hg_ref[:, cols]))
    y_ref[...] = jnp.concatenate(outs, axis=-1).astype(y_ref.dtype)

    @pl.when(i == nt - 1)
    def _():
        for h in range(HG_HEADS):
            st_ref[h] = s_ref[h].T


def _hgrn_prompt(proj, lb_param, gain, layer, bsz, seq, *, tb=256):
    tb = min(tb, seq)
    nt = seq // tb
    cblk = lambda c: pl.BlockSpec((tb, HG_DIM), lambda b, i, c=c: (b * nt + i, c))
    return pl.pallas_call(
        functools.partial(_hgrn_kernel, layer=layer),
        out_shape=(jax.ShapeDtypeStruct((bsz * seq, HG_DIM), BF16),
                   jax.ShapeDtypeStruct((bsz, HG_HEADS, HG_DK, HG_DK), F32)),
        grid=(bsz, nt),
        in_specs=[cblk(OFF_HQ // HG_DIM), cblk(OFF_HF // HG_DIM),
                  cblk(OFF_HI // HG_DIM), cblk(OFF_HG // HG_DIM),
                  pl.BlockSpec((DEPTH, HG_DIM), lambda b, i: (0, 0)),
                  pl.BlockSpec((1, HG_DK), lambda b, i: (0, 0))],
        out_specs=(pl.BlockSpec((tb, HG_DIM), lambda b, i: (b * nt + i, 0)),
                   pl.BlockSpec((None, HG_HEADS, HG_DK, HG_DK), lambda b, i: (b, 0, 0, 0))),
        scratch_shapes=[pltpu.VMEM((tb, HG_DIM), F32),
                        pltpu.VMEM((tb, HG_DIM), F32),
                        pltpu.VMEM((tb, HG_DIM), F32),
                        pltpu.VMEM((tb, HG_DIM), F32),
                        pltpu.VMEM((HG_HEADS, HG_DK, HG_DK), F32)],
        compiler_params=_cparams("arbitrary", "arbitrary"),
        name="hgrn_prompt",
    )(proj, proj, proj, proj, lb_param, gain)


def _attn_kernel(qa_ref, qb_ref, kc_ref, vc_ref, kp_ref, vp_ref, sink_ref, y_ref):
    n = pl.program_id(1)
    w = WINDOW
    qi = lax.broadcasted_iota(I32, (Q_PER_KV * w, 2 * w), 0) % w
    kj = lax.broadcasted_iota(I32, (Q_PER_KV * w, 2 * w), 1)
    dist = qi + w - kj
    valid = (dist >= 0) & (dist < w) & ((kj >= w) | (n > 0))
    distf = dist.astype(F32)
    grp = lax.broadcasted_iota(I32, (Q_PER_KV * w, 1), 0) // w
    sinks = sink_ref[...]
    outs = []
    for kv in range(N_KV_HEADS):
        q_ref = qa_ref if kv < 2 else qb_ref
        qoff = (kv % 2) * Q_PER_KV * HEAD_DIM
        q = jnp.concatenate(
            [q_ref[:, qoff + g * HEAD_DIM: qoff + (g + 1) * HEAD_DIM] for g in range(Q_PER_KV)],
            axis=0).astype(BF16)
        ks = slice(kv * HEAD_DIM, (kv + 1) * HEAD_DIM)
        k = jnp.concatenate([kp_ref[:, ks], kc_ref[:, ks]], axis=0).astype(BF16)
        v = jnp.concatenate([vp_ref[:, ks], vc_ref[:, ks]], axis=0).astype(BF16)
        slope = jnp.zeros((Q_PER_KV * w, 1), F32)
        sink = jnp.zeros((Q_PER_KV * w, 1), F32)
        for g in range(Q_PER_KV):
            hd = kv * Q_PER_KV + g
            slope = jnp.where(grp == g, _alibi_slope(hd), slope)
            sink = jnp.where(grp == g, sinks[:, hd:hd + 1], sink)
        s = _dot_nt(q, k) * (HEAD_DIM ** -0.5) - slope * distf
        s = jnp.where(valid, s, -jnp.inf)
        m = jnp.maximum(jnp.max(s, axis=-1, keepdims=True), sink)
        p = jnp.exp(s - m)
        denom = jnp.sum(p, axis=-1, keepdims=True) + jnp.exp(sink - m)
        o = _dot(p.astype(BF16), v) / denom
        for g in range(Q_PER_KV):
            outs.append(o[g * w:(g + 1) * w])
    y_ref[...] = jnp.concatenate(outs, axis=-1).astype(y_ref.dtype)


def _attn_prompt(proj, sinks, bsz, seq):
    w = WINDOW
    nb = seq // w
    half = ATTN_DIM // 2
    cur = lambda width, off: pl.BlockSpec(
        (w, width), lambda b, n: (b * nb + n, off // width))
    prev = lambda width, off: pl.BlockSpec(
        (w, width), lambda b, n: (b * nb + jnp.maximum(n - 1, 0), off // width))
    return pl.pallas_call(
        _attn_kernel,
        out_shape=jax.ShapeDtypeStruct((bsz * seq, ATTN_DIM), BF16),
        grid=(bsz, nb),
        in_specs=[cur(half, OFF_AQ), cur(half, OFF_AQ + half),
                  cur(KV_DIM, OFF_AK), cur(KV_DIM, OFF_AV),
                  prev(KV_DIM, OFF_AK), prev(KV_DIM, OFF_AV),
                  pl.BlockSpec((1, N_HEADS), lambda b, n: (0, 0))],
        out_specs=pl.BlockSpec((w, ATTN_DIM), lambda b, n: (b * nb + n, 0)),
        compiler_params=_cparams("arbitrary", "arbitrary"),
        name="attn_prompt",
    )(proj, proj, proj, proj, proj, proj, sinks)


def _sample_mix_kernel(p_ref, sc_ref, s0_ref, kc_ref, vc_ref, cw_ref, lbp_ref,
                       gain_ref, sink_ref,
                       y_ref, sco_ref, so_ref, ko_ref, vo_ref, *, layer):
    p = p_ref[...]
    seg = lambda off, n: p[:, off:off + n]
    u = seg(OFF_CC, CONV_DIM) * seg(OFF_CH, CONV_DIM)
    hist = sc_ref[...]
    cw = cw_ref[...]
    conv = cw[0:1] * hist[0:1] + cw[1:2] * hist[1:2] + cw[2:3] * u
    ya = seg(OFF_CB, CONV_DIM) * conv
    sco_ref[...] = jnp.concatenate([hist[1:2], u], axis=0)
    lb = _hg_lower_bound(lbp_ref[...], layer)
    f = lb + (1.0 - lb) * _sigmoid(seg(OFF_HF, HG_DIM))
    g = jnp.log(f)
    kk = 1.0 - f
    q = seg(OFF_HQ, HG_DIM) * (HG_DK ** -0.5)
    v = seg(OFF_HI, HG_DIM)
    gate = seg(OFF_HG, HG_DIM)
    gain = gain_ref[...]
    yb = []
    for h in range(HG_HEADS):
        cols = slice(h * HG_DK, (h + 1) * HG_DK)
        s0 = s0_ref[h]
        eg = jnp.exp(g[:, cols])
        qe_col = _col_from_row(q[:, cols] * eg)
        o = (jnp.sum(q[:, cols] * kk[:, cols], axis=-1, keepdims=True) * v[:, cols]
             + jnp.sum(qe_col * s0, axis=0, keepdims=True))
        so_ref[h] = _col_from_row(eg) * s0 + _col_from_row(kk[:, cols]) * v[:, cols]
        o = o * lax.rsqrt(jnp.mean(o * o, axis=-1, keepdims=True) + RMS_EPS) * gain
        yb.append(o * _silu(gate[:, cols]))
    w = kc_ref.shape[0]
    kcache = kc_ref[...]
    vcache = vc_ref[...]
    knew = seg(OFF_AK, KV_DIM)
    vnew = seg(OFF_AV, KV_DIM)
    aq = seg(OFF_AQ, ATTN_DIM)
    sinks = sink_ref[...]
    kj = lax.broadcasted_iota(I32, (Q_PER_KV, w), 1)
    dist = w - kj
    valid = dist < WINDOW
    gi = lax.broadcasted_iota(I32, (Q_PER_KV, 1), 0)
    yc = []
    for kv in range(N_KV_HEADS):
        ks = slice(kv * HEAD_DIM, (kv + 1) * HEAD_DIM)
        qh = jnp.concatenate(
            [aq[:, (kv * Q_PER_KV + gq) * HEAD_DIM:(kv * Q_PER_KV + gq + 1) * HEAD_DIM]
             for gq in range(Q_PER_KV)], axis=0)
        slope = jnp.zeros((Q_PER_KV, 1), F32)
        sink = jnp.zeros((Q_PER_KV, 1), F32)
        for gq in range(Q_PER_KV):
            hd = kv * Q_PER_KV + gq
            slope = jnp.where(gi == gq, _alibi_slope(hd), slope)
            sink = jnp.where(gi == gq, sinks[:, hd:hd + 1], sink)
        scale = HEAD_DIM ** -0.5
        sc = _dot3_nt(qh, kcache[:, ks]) * scale - slope * dist.astype(F32)
        sc = jnp.where(valid, sc, -jnp.inf)
        sn = jnp.sum(qh * knew[:, ks], axis=-1, keepdims=True) * scale
        m = jnp.maximum(jnp.maximum(jnp.max(sc, axis=-1, keepdims=True), sn), sink)
        pc = jnp.exp(sc - m)
        pn = jnp.exp(sn - m)
        denom = jnp.sum(pc, axis=-1, keepdims=True) + pn + jnp.exp(sink - m)
        o = (_dot3(pc, vcache[:, ks]) + pn * vnew[:, ks]) / denom
        for gq in range(Q_PER_KV):
            yc.append(o[gq:gq + 1])
    y_ref[...] = jnp.concatenate([ya] + yb + yc, axis=-1)
    row = lax.broadcasted_iota(I32, (w, KV_DIM), 0)
    ko_ref[...] = jnp.where(row == w - 1, knew, pltpu.roll(kcache, w - 1, 0))
    vo_ref[...] = jnp.where(row == w - 1, vnew, pltpu.roll(vcache, w - 1, 0))


def _sample_mixers(proj, state_conv, state_hgrn, cache_k, cache_v, conv_w,
                   lb_param, gain, sinks, layer):
    nb = proj.shape[0]
    w = cache_k.shape[1]
    per_b = lambda *shape: pl.BlockSpec((None,) + shape,
                                        lambda b: (b,) + (0,) * len(shape))
    whole = lambda *shape: pl.BlockSpec(shape, lambda b: (0,) * len(shape))
    return pl.pallas_call(
        functools.partial(_sample_mix_kernel, layer=layer),
        out_shape=(jax.ShapeDtypeStruct((nb, 1, D_MODEL), F32),
                   jax.ShapeDtypeStruct((nb, CONV_W - 1, CONV_DIM), F32),
                   jax.ShapeDtypeStruct((nb, HG_HEADS, HG_DK, HG_DK), F32),
                   jax.ShapeDtypeStruct((nb, w, KV_DIM), F32),
                   jax.ShapeDtypeStruct((nb, w, KV_DIM), F32)),
        grid=(nb,),
        in_specs=[per_b(1, IN_DIM), per_b(CONV_W - 1, CONV_DIM),
                  per_b(HG_HEADS, HG_DK, HG_DK), per_b(w, KV_DIM), per_b(w, KV_DIM),
                  whole(CONV_W, CONV_DIM), whole(DEPTH, HG_DIM), whole(1, HG_DK),
                  whole(1, N_HEADS)],
        out_specs=(per_b(1, D_MODEL), per_b(CONV_W - 1, CONV_DIM),
                   per_b(HG_HEADS, HG_DK, HG_DK), per_b(w, KV_DIM), per_b(w, KV_DIM)),
        compiler_params=_cparams("arbitrary"),
        name="sample_mixers",
    )(proj.reshape(nb, 1, IN_DIM), state_conv, state_hgrn,
      cache_k.reshape(nb, w, KV_DIM), cache_v.reshape(nb, w, KV_DIM),
      conv_w, lb_param, gain, sinks)


def _layer_norm(z, g, b):
    mu = jnp.mean(z, axis=-1, keepdims=True)
    zc = z - mu
    var = jnp.mean(zc * zc, axis=-1, keepdims=True)
    return zc * lax.rsqrt(var + LN_EPS) * g + b


def _route(logits, bias):
    lg = logits + bias
    rowv = lambda r: lg[r:r + 1]
    best, gidx = rowv(0), jnp.zeros_like(rowv(0), dtype=I32)
    for r in range(1, N_GROUPS):
        upd = rowv(r) > best
        best = jnp.where(upd, rowv(r), best)
        gidx = jnp.where(upd, r, gidx)
    gden = sum(jnp.exp(rowv(r) - best) for r in range(N_GROUPS))
    gprob = 1.0 / gden
    ev = []
    for j in range(EXPERTS_PER_GROUP):
        val = rowv(N_GROUPS + j)
        for grp in range(1, N_GROUPS):
            val = jnp.where(gidx == grp, rowv(N_GROUPS + grp * EXPERTS_PER_GROUP + j), val)
        ev.append(val)
    v1, j1 = ev[0], jnp.zeros_like(gidx)
    for j in range(1, EXPERTS_PER_GROUP):
        upd = ev[j] > v1
        v1 = jnp.where(upd, ev[j], v1)
        j1 = jnp.where(upd, j, j1)
    v2, j2 = jnp.full_like(v1, -jnp.inf), jnp.zeros_like(gidx)
    for j in range(EXPERTS_PER_GROUP):
        upd = (j1 != j) & (ev[j] > v2)
        v2 = jnp.where(upd, ev[j], v2)
        j2 = jnp.where(upd, j, j2)
    e2 = jnp.exp(v2 - v1)
    w1 = gprob / (1.0 + e2)
    w2 = gprob * e2 / (1.0 + e2)
    base = gidx * EXPERTS_PER_GROUP
    return (jnp.concatenate([base + j1, base + j2], axis=0),
            jnp.concatenate([w1, w2], axis=0))


def _outproj_kernel(*refs, n_y, precise):
    y_refs = refs[:n_y]
    x_ref, w_ref, g_ref, b_ref, rw_ref, rb_ref = refs[n_y:n_y + 6]
    x1_ref, x1b_ref, eid_ref, ewt_ref = refs[n_y + 6:n_y + 10]
    acc_ref = refs[n_y + 10]
    j = pl.program_id(1)
    tn = w_ref.shape[1]
    mm = _dot3 if precise else _dot
    off = 0
    part = None
    for y_ref in y_refs:
        kk = y_ref.shape[1]
        t = mm(y_ref[...], w_ref[off:off + kk, :])
        part = t if part is None else part + t
        off += kk
    acc_ref[:, pl.ds(pl.multiple_of(j * tn, tn), tn)] = part

    @pl.when(j == pl.num_programs(1) - 1)
    def _():
        z = ALPHA * x_ref[...] + acc_ref[...]
        x1 = _layer_norm(z, g_ref[...], b_ref[...])
        x1_ref[...] = x1
        hi = x1.astype(BF16)
        x1b_ref[...] = hi
        lo = (x1 - hi.astype(F32)).astype(BF16)
        rw = rw_ref[...]
        rwh, rwl = _split(rw)
        logits = _dot_nt(rwh, hi) + (_dot_nt(rwl, hi) + _dot_nt(rwh, lo))
        eid, ewt = _route(logits, rb_ref[...])
        t = eid.shape[1]
        eid_ref[...] = jnp.concatenate([eid, jnp.zeros((SUBLANES - TOP_K, t), I32)], axis=0)
        ewt_ref[...] = jnp.concatenate([ewt, jnp.zeros((SUBLANES - TOP_K, t), F32)], axis=0)


def _outproj_ln_route(ys, x, w_out, ln_g, ln_b, rw_t, rb, *, precise, tm=512, tn=1024):
    m, d = x.shape
    tm = min(tm, m)
    n_y = len(ys)
    y_specs = [pl.BlockSpec((tm, y.shape[1]), lambda i, j: (i, 0)) for y in ys]
    row_tile = lambda: pl.BlockSpec((tm, d), lambda i, j: (i, 0))
    vec = lambda: pl.BlockSpec((1, d), lambda i, j: (0, 0))
    return pl.pallas_call(
        functools.partial(_outproj_kernel, n_y=n_y, precise=precise),
        out_shape=(jax.ShapeDtypeStruct((m, d), F32),
                   jax.ShapeDtypeStruct((m, d), BF16),
                   jax.ShapeDtypeStruct((SUBLANES, m), I32),
                   jax.ShapeDtypeStruct((SUBLANES, m), F32)),
        grid=(m // tm, d // tn),
        in_specs=y_specs + [row_tile(),
                            pl.BlockSpec((d, tn), lambda i, j: (0, j)),
                            vec(), vec(),
                            pl.BlockSpec((ROUTE_ROWS, d), lambda i, j: (0, 0)),
                            pl.BlockSpec((ROUTE_ROWS, 1), lambda i, j: (0, 0))],
        out_specs=(row_tile(), row_tile(),
                   pl.BlockSpec((SUBLANES, tm), lambda i, j: (0, i)),
                   pl.BlockSpec((SUBLANES, tm), lambda i, j: (0, i))),
        scratch_shapes=[pltpu.VMEM((tm, d), F32)],
        compiler_params=_cparams("arbitrary", "arbitrary"),
        name="outproj_ln_route",
    )(*ys, x, w_out, ln_g, ln_b, rw_t, rb)


def _row_copy(src_ref, src_row, dst_ref, dst_row, sem):
    return pltpu.make_async_copy(src_ref.at[pl.ds(src_row, 1)],
                                 dst_ref.at[pl.ds(dst_row, 1)], sem)


def _scatter_rows(pos_ref, slot0, n_slots_per_k, src_ref, src0, count, out_ref, sem):
    def start(r, c):
        for k in range(TOP_K):
            p = pos_ref[k * n_slots_per_k + slot0 + r]
            _row_copy(src_ref, src0 + r, out_ref, p, sem).start()
        return c

    def wait(r, c):
        for k in range(TOP_K):
            _row_copy(src_ref, src0, out_ref, 0, sem).wait()
        return c

    lax.fori_loop(0, count, start, 0)
    lax.fori_loop(0, count, wait, 0)


def _dispatch_kernel(pos_ref, xp_ref, xs_ref, out_ref, sem, *, n_prompt, n_sample, tile):
    i = pl.program_id(0)
    n_ptiles = n_prompt // tile
    total = n_prompt + n_sample

    @pl.when(i < n_ptiles)
    def _():
        tok0 = i * tile
        _scatter_rows(pos_ref, tok0, total, xp_ref, tok0, tile, out_ref, sem)

    @pl.when(i == n_ptiles)
    def _():
        _scatter_rows(pos_ref, n_prompt, total, xs_ref, 0, n_sample, out_ref, sem)


def _dispatch(pos, x1p, x1s, *, tile=256):
    n_prompt, d = x1p.shape
    n_sample = x1s.shape[0]
    tile = min(tile, n_prompt)
    rows = TOP_K * (n_prompt + n_sample)
    any_spec = pl.BlockSpec(memory_space=pl.ANY)
    return pl.pallas_call(
        functools.partial(_dispatch_kernel, n_prompt=n_prompt, n_sample=n_sample, tile=tile),
        out_shape=jax.ShapeDtypeStruct((rows, d), x1p.dtype),
        grid_spec=pltpu.PrefetchScalarGridSpec(
            num_scalar_prefetch=1,
            grid=(n_prompt // tile + 1,),
            in_specs=[any_spec, any_spec],
            out_specs=any_spec,
            scratch_shapes=[pltpu.SemaphoreType.DMA(())]),
        compiler_params=_cparams("arbitrary"),
        name="moe_dispatch",
    )(pos, x1p, x1s)


def _ffn_kernel(vt_ref, ve_ref, lo_ref, hi_ref, first_ref, nact_ref,
                x_ref, wg_ref, wu_ref, wd_ref, y_ref):
    v = pl.program_id(0)

    @pl.when(v < nact_ref[0])
    def _():
        tile = x_ref.shape[0]
        rows = vt_ref[v] * tile + lax.broadcasted_iota(I32, (tile, 1), 0)
        mine = (rows >= lo_ref[v]) & (rows < hi_ref[v])
        xb = jnp.where(mine, x_ref[...], 0.0).astype(BF16)
        h = _silu(_dot(xb, wg_ref[...].astype(BF16))) * _dot(xb, wu_ref[...].astype(BF16))
        y = _dot(h.astype(BF16), wd_ref[...].astype(BF16))
        y = jnp.where(mine, y, 0.0)

        @pl.when(first_ref[v] == 1)
        def _():
            y_ref[...] = y

        @pl.when(first_ref[v] == 0)
        def _():
            y_ref[...] += y


def _expert_ffn(plan, xs, w_gate, w_up, w_down, *, tile=EXPERT_TILE):
    rows, d = xs.shape
    n_visits = plan[0].shape[0]
    de = w_gate.shape[2]
    row_blk = pl.BlockSpec((tile, d), lambda v, vt, ve, lo, hi, fi, na: (vt[v], 0))
    return pl.pallas_call(
        _ffn_kernel,
        out_shape=jax.ShapeDtypeStruct((rows, d), F32),
        grid_spec=pltpu.PrefetchScalarGridSpec(
            num_scalar_prefetch=6,
            grid=(n_visits,),
            in_specs=[row_blk,
                      pl.BlockSpec((None, d, de), lambda v, vt, ve, lo, hi, fi, na: (ve[v], 0, 0)),
                      pl.BlockSpec((None, d, de), lambda v, vt, ve, lo, hi, fi, na: (ve[v], 0, 0)),
                      pl.BlockSpec((None, de, d), lambda v, vt, ve, lo, hi, fi, na: (ve[v], 0, 0))],
            out_specs=row_blk),
        compiler_params=_cparams("arbitrary"),
        name="moe_ffn",
    )(*plan, xs, w_gate, w_up, w_down)


def _combine_kernel(pos_ref, ys_ref, x_ref, wt_ref, g_ref, b_ref,
                    o_ref, ob_ref, buf_ref, sem, *, slot0, n_slots_per_k):
    i = pl.program_id(0)
    tile = x_ref.shape[0]
    tok0 = slot0 + i * tile

    def start(r, c):
        for k in range(TOP_K):
            p = pos_ref[k * n_slots_per_k + tok0 + r]
            _row_copy(ys_ref, p, buf_ref.at[k], r, sem).start()
        return c

    def wait(r, c):
        for k in range(TOP_K):
            _row_copy(ys_ref, 0, buf_ref.at[k], 0, sem).wait()
        return c

    lax.fori_loop(0, tile, start, 0)
    lax.fori_loop(0, tile, wait, 0)
    wt = wt_ref[...]
    ffn = wt[:, 0:1] * buf_ref[0] + wt[:, 1:2] * buf_ref[1]
    out = _layer_norm(ALPHA * x_ref[...] + ffn, g_ref[...], b_ref[...])
    o_ref[...] = out
    ob_ref[...] = out.astype(BF16)


def _combine_ln(pos, ys, x1, wt, ln_g, ln_b, *, slot0, n_slots_per_k, tile=256):
    m, d = x1.shape
    tile = min(tile, m)
    row = lambda width: pl.BlockSpec((tile, width), lambda i, pos: (i, 0))
    vec = pl.BlockSpec((1, d), lambda i, pos: (0, 0))
    return pl.pallas_call(
        functools.partial(_combine_kernel, slot0=slot0, n_slots_per_k=n_slots_per_k),
        out_shape=(jax.ShapeDtypeStruct((m, d), F32),
                   jax.ShapeDtypeStruct((m, d), BF16)),
        grid_spec=pltpu.PrefetchScalarGridSpec(
            num_scalar_prefetch=1,
            grid=(m // tile,),
            in_specs=[pl.BlockSpec(memory_space=pl.ANY), row(d), row(TOP_K), vec, vec],
            out_specs=(row(d), row(d)),
            scratch_shapes=[pltpu.VMEM((TOP_K, tile, d), F32),
                            pltpu.SemaphoreType.DMA(())]),
        compiler_params=_cparams("arbitrary"),
        name="moe_combine_ln",
    )(pos, ys, x1, wt, ln_g, ln_b)


def _route_plan(eid, tile):
    n_tok = eid.shape[1]
    flat = eid.reshape(-1)
    onehot = (flat[:, None] == jnp.arange(N_EXPERTS, dtype=I32)[None, :]).astype(I32)
    csum = jnp.cumsum(onehot, axis=0)
    counts = csum[-1]
    rank = jnp.sum(csum * onehot, axis=1) - 1
    ends = jnp.cumsum(counts)
    offs = ends - counts
    pos = (offs[flat] + rank).astype(I32)
    first_tile = offs // tile
    last_tile = (ends - 1) // tile
    nvis = jnp.where(counts > 0, last_tile - first_tile + 1, 0)
    vend = jnp.cumsum(nvis)
    vbase = vend - nvis
    nact = vend[-1]
    n_visits = pl.cdiv(TOP_K * n_tok, tile) + N_EXPERTS - 1
    v = jnp.minimum(jnp.arange(n_visits, dtype=I32), nact - 1)
    e = jnp.sum((vend[None, :] <= v[:, None]).astype(I32), axis=1)
    t = first_tile[e] + (v - vbase[e])
    first = jnp.concatenate([jnp.ones((1,), I32), (t[1:] != t[:-1]).astype(I32)])
    plan = (t.astype(I32), e.astype(I32), offs[e].astype(I32), ends[e].astype(I32),
            first, nact.reshape(1).astype(I32))
    return pos, plan


def kernel(x_prompt, x_sample, state_conv, state_hgrn, cache_k_win, cache_v_win, w_in, w_out, conv_w, hg_lb_param, hg_gain, attn_sinks, ln1_g, ln1_b, w_group, b_group, w_router, b_router, w_gate, w_up, w_down, ln2_g, ln2_b):
    bsz, seq, d = x_prompt.shape
    nb = x_sample.shape[0]
    n_prompt = bsz * seq
    n_tok = n_prompt + nb
    w_buf = cache_k_win.shape[2]

    xp = x_prompt.reshape(n_prompt, d)
    xp_mm = xp
    xs = x_sample.reshape(nb, d)
    pad_rows = ROUTE_ROWS - N_GROUPS - N_EXPERTS
    outs = {k: [] for k in ("cp", "cs", "hp", "hs", "kp", "ks", "vp", "vs")}
    for l in range(DEPTH):
        gain = hg_gain[l].reshape(1, HG_DK)
        sinks = attn_sinks[l].reshape(1, N_HEADS)
        rw_t = jnp.concatenate([w_group[l].T, w_router[l].T, jnp.zeros((pad_rows, d), F32)], axis=0)
        rb = jnp.concatenate([b_group[l], b_router[l], jnp.zeros((pad_rows,), F32)]).reshape(ROUTE_ROWS, 1)
        g1, b1 = ln1_g[l].reshape(1, d), ln1_b[l].reshape(1, d)
        g2, b2 = ln2_g[l].reshape(1, d), ln2_b[l].reshape(1, d)

        proj = _inproj(xp_mm, w_in[l])
        ya, ctail = _conv_prompt(proj, conv_w[l], bsz, seq)
        yb, hstate = _hgrn_prompt(proj, hg_lb_param, gain, l, bsz, seq)
        yc = _attn_prompt(proj, sinks, bsz, seq)
        x1p, _, eid_p, ewt_p = _outproj_ln_route(
            [ya, yb, yc], xp, w_out[l].astype(BF16), g1, b1, rw_t, rb, precise=False)
        outs["cp"].append(ctail[:, SUBLANES - (CONV_W - 1):])
        outs["hp"].append(hstate)
        kv = proj.reshape(bsz, seq, IN_DIM)[:, seq - w_buf:]
        outs["kp"].append(kv[:, :, OFF_AK:OFF_AK + KV_DIM].reshape(bsz, w_buf, N_KV_HEADS, HEAD_DIM))
        outs["vp"].append(kv[:, :, OFF_AV:OFF_AV + KV_DIM].reshape(bsz, w_buf, N_KV_HEADS, HEAD_DIM))

        proj_s = _inproj3(xs, w_in[l])
        ysm, cst, hst, kst, vst = _sample_mixers(
            proj_s, state_conv[l], state_hgrn[l], cache_k_win[l], cache_v_win[l],
            conv_w[l], hg_lb_param, gain, sinks, l)
        x1s, _, eid_s, ewt_s = _outproj_ln_route(
            [ysm.reshape(nb, d)], xs, w_out[l], g1, b1, rw_t, rb, precise=True)
        outs["cs"].append(cst)
        outs["hs"].append(hst)
        outs["ks"].append(kst.reshape(nb, w_buf, N_KV_HEADS, HEAD_DIM))
        outs["vs"].append(vst.reshape(nb, w_buf, N_KV_HEADS, HEAD_DIM))

        eid = jnp.concatenate([eid_p[:TOP_K], eid_s[:TOP_K]], axis=1)
        pos, plan = _route_plan(eid, EXPERT_TILE)
        x_sorted = _dispatch(pos, x1p, x1s)
        y_sorted = _expert_ffn(plan, x_sorted, w_gate[l], w_up[l], w_down[l])
        xp, xp_mm = _combine_ln(pos, y_sorted, x1p, ewt_p[:TOP_K].T, g2, b2,
                                slot0=0, n_slots_per_k=n_tok)
        xs, _ = _combine_ln(pos, y_sorted, x1s, ewt_s[:TOP_K].T, g2, b2,
                            slot0=n_prompt, n_slots_per_k=n_tok)

    st = lambda k: jnp.stack(outs[k])
    return (xp.reshape(bsz, seq, d), xs.reshape(nb, 1, d), st("cp"), st("cs"),
            st("hp"), st("hs"), st("kp"), st("ks"), st("vp"), st("vs"))
```

```python
import functools

import jax
import jax.numpy as jnp
import numpy as np
from jax import lax
from jax.experimental import pallas as pl
from jax.experimental.pallas import tpu as pltpu

F32 = jnp.float32
BF16 = jnp.bfloat16
I32 = jnp.int32

D_MODEL = 2048
DEPTH = 2
CONV_DIM = 512
CONV_W = 3
HG_DIM = 512
HG_HEADS = 4
HG_DK = 128
HG_CHUNK = 16
HEAD_DIM = 64
ATTN_DIM = 1024
N_HEADS = 16
N_KV_HEADS = 4
Q_PER_KV = 4
KV_DIM = 256
WINDOW = 128
N_GROUPS = 4
EXPERTS_PER_GROUP = 4
N_EXPERTS = 16
TOP_K = 2
D_EXPERT = 512
ALPHA = (2 * DEPTH) ** 0.25
LN_EPS = 1e-5
RMS_EPS = 1e-6
IN_DIM = 5120
OFF_CB, OFF_CC, OFF_CH = 0, 512, 1024
OFF_HQ, OFF_HF, OFF_HI, OFF_HG = 1536, 2048, 2560, 3072
OFF_AQ, OFF_AK, OFF_AV = 3584, 4608, 4864

VMEM_LIMIT_BYTES = 56 * 1024 * 1024
LANES = 128
SUBLANES = 8

EXPERT_TILE = 256
COMBINE_TILE = 512
INVERT_UNROLL = 8
ROUTE_ROWS = 32


TOKEN_ROWS = D_MODEL // LANES
TOKEN_PITCH = 20


def _store_token_major(ref, x, pitch):
    n = x.shape[0]
    for c in range(TOKEN_ROWS):
        ref[pl.ds(c, n, stride=pitch), :] = x[:, c * LANES:(c + 1) * LANES]


def _load_token_major(ref, n, pitch):
    return jnp.concatenate(
        [ref[pl.ds(c, n, stride=pitch), :] for c in range(TOKEN_ROWS)], axis=1)


def _cparams(*sem):
    return pltpu.CompilerParams(dimension_semantics=sem,
                                vmem_limit_bytes=VMEM_LIMIT_BYTES)


def _dot(a, b):
    return jnp.dot(a, b, preferred_element_type=F32)


def _dot_nt(a, b):
    return lax.dot_general(a, b, (((1,), (1,)), ((), ())),
                           preferred_element_type=F32)


def _split(x):
    hi = x.astype(BF16)
    lo = (x - hi.astype(F32)).astype(BF16)
    return hi, lo


def _dot3(a, b):
    ah, al = _split(a)
    bh, bl = _split(b)
    return _dot(ah, bh) + (_dot(ah, bl) + _dot(al, bh))


def _dot3_nt(a, b):
    ah, al = _split(a)
    bh, bl = _split(b)
    return _dot_nt(ah, bh) + (_dot_nt(ah, bl) + _dot_nt(al, bh))


def _sigmoid(x):
    return 1.0 / (1.0 + jnp.exp(-x))


def _silu(x):
    return x * _sigmoid(x)


def _col_from_row(row):
    n = row.shape[1]
    eye = (lax.broadcasted_iota(I32, (n, n), 0)
           == lax.broadcasted_iota(I32, (n, n), 1))
    return jnp.sum(jnp.where(eye, row, 0.0), axis=1, keepdims=True)


def _hg_lower_bound(lbp, layer):
    m = jnp.max(lbp, axis=0, keepdims=True)
    e = jnp.exp(lbp - m)
    soft = e / jnp.sum(e, axis=0, keepdims=True)
    acc = soft[0:1]
    for i in range(1, layer + 1):
        acc = acc + soft[i:i + 1]
    return acc - soft[0:1]


def _alibi_slope(head):
    return float(2.0 ** (-8.0 * (head + 1) / N_HEADS))


def _inproj_kernel(x_ref, w_ref, o_ref, xb_ref):
    @pl.when(pl.program_id(1) == 0)
    def _():
        xb_ref[...] = x_ref[...].astype(BF16)

    o_ref[...] = _dot(xb_ref[...], w_ref[...].astype(BF16))


def _inproj(x, w, layer, *, tm=1024, tn=1024):
    m, k = x.shape
    n = w.shape[2]
    tm = min(tm, m)
    return pl.pallas_call(
        _inproj_kernel,
        out_shape=jax.ShapeDtypeStruct((m, n), F32),
        grid=(m // tm, n // tn),
        in_specs=[pl.BlockSpec((tm, k), lambda i, j: (i, 0)),
                  pl.BlockSpec((None, k, tn), lambda i, j: (layer, 0, j))],
        out_specs=pl.BlockSpec((tm, tn), lambda i, j: (i, j)),
        scratch_shapes=[pltpu.VMEM((tm, k), BF16)],
        compiler_params=_cparams("arbitrary", "arbitrary"),
        name="inproj",
    )(x, w)


def _inproj3_kernel(x_ref, w_ref, o_ref):
    o_ref[...] = _dot3(x_ref[...], w_ref[...])


def _inproj3(x, w, layer, *, tn=512):
    m, k = x.shape
    n = w.shape[2]
    return pl.pallas_call(
        _inproj3_kernel,
        out_shape=jax.ShapeDtypeStruct((m, n), F32),
        grid=(n // tn,),
        in_specs=[pl.BlockSpec((m, k), lambda j: (0, 0)),
                  pl.BlockSpec((None, k, tn), lambda j: (layer, 0, j))],
        out_specs=pl.BlockSpec((m, tn), lambda j: (0, j)),
        compiler_params=_cparams("arbitrary"),
        name="inproj_sample",
    )(x, w)


def _conv_kernel(cb_ref, cc_ref, ch_ref, w_ref, y_ref, tail_ref, carry_ref):
    i = pl.program_id(1)

    @pl.when(i == 0)
    def _():
        carry_ref[...] = jnp.zeros_like(carry_ref)

    u = cc_ref[...] * ch_ref[...]
    tl = u.shape[0]
    row = lax.broadcasted_iota(I32, u.shape, 0)
    prev1 = carry_ref[SUBLANES - 1:SUBLANES, :]
    prev2 = carry_ref[SUBLANES - 2:SUBLANES - 1, :]
    u1 = jnp.where(row == 0, prev1, pltpu.roll(u, 1, 0))
    u2 = jnp.where(row == 0, prev2, jnp.where(row == 1, prev1, pltpu.roll(u, 2, 0)))
    w = w_ref[...]
    y = w[0:1] * u2 + w[1:2] * u1 + w[2:3] * u
    y_ref[...] = (cb_ref[...] * y).astype(y_ref.dtype)
    tail = u[tl - SUBLANES:tl, :]
    carry_ref[...] = tail
    tail_ref[...] = tail


def _conv_prompt(proj, conv_w, bsz, seq, *, tl=512):
    tl = min(tl, seq)
    nt = seq // tl
    cblk = lambda c: pl.BlockSpec((tl, CONV_DIM), lambda b, i, c=c: (b * nt + i, c))
    return pl.pallas_call(
        _conv_kernel,
        out_shape=(jax.ShapeDtypeStruct((bsz * seq, CONV_DIM), BF16),
                   jax.ShapeDtypeStruct((bsz, SUBLANES, CONV_DIM), F32)),
        grid=(bsz, nt),
        in_specs=[cblk(OFF_CB // CONV_DIM), cblk(OFF_CC // CONV_DIM),
                  cblk(OFF_CH // CONV_DIM),
                  pl.BlockSpec((CONV_W, CONV_DIM), lambda b, i: (0, 0))],
        out_specs=(pl.BlockSpec((tl, CONV_DIM), lambda b, i: (b * nt + i, 0)),
                   pl.BlockSpec((None, SUBLANES, CONV_DIM), lambda b, i: (b, 0, 0))),
        scratch_shapes=[pltpu.VMEM((SUBLANES, CONV_DIM), F32)],
        compiler_params=_cparams("arbitrary", "arbitrary"),
        name="conv_prompt",
    )(proj, proj, proj, conv_w)


def _seg_cumsum(g, seg):
    row = lax.broadcasted_iota(I32, g.shape, 0) % seg
    s = 1
    while s < seg:
        g = g + jnp.where(row >= s, pltpu.roll(g, s, 0), 0.0)
        s *= 2
    return g


def _hgrn_kernel(hq_ref, hf_ref, hi_ref, hg_ref, lbp_ref, gain_ref,
                 y_ref, st_ref, cum_ref, k_ref, q_ref, o_ref, s_ref, *, layer):
    i = pl.program_id(1)
    nt = pl.num_programs(1)
    tb = hq_ref.shape[0]
    c = HG_CHUNK

    @pl.when(i == 0)
    def _():
        s_ref[...] = jnp.zeros_like(s_ref)

    lb = _hg_lower_bound(lbp_ref[...], layer)
    f = lb + (1.0 - lb) * _sigmoid(hf_ref[...])
    cum_ref[...] = _seg_cumsum(jnp.log(f), c)
    k_ref[...] = 1.0 - f
    q_ref[...] = hq_ref[...] * (HG_DK ** -0.5)

    ones = jnp.ones((HG_DK, HG_DK), BF16)
    trow = lax.broadcasted_iota(I32, (c, HG_DK), 0)

    def chunk(ci, carry):
        r0 = pl.multiple_of(ci * c, c)
        for h in range(HG_HEADS):
            cols = slice(h * HG_DK, (h + 1) * HG_DK)
            cum = cum_ref[pl.ds(r0, c), cols]
            kc = k_ref[pl.ds(r0, c), cols]
            qc = q_ref[pl.ds(r0, c), cols]
            vc = hi_ref[pl.ds(r0, c), cols]
            parts = []
            for s in range(c):
                dec = jnp.exp(jnp.minimum(cum - cum[s:s + 1], 0.0))
                parts.append((qc * kc[s:s + 1] * dec).astype(BF16))
            sc = _dot(jnp.concatenate(parts, axis=0), ones)
            o = jnp.zeros((c, HG_DK), F32)
            for s in range(c):
                o = o + jnp.where(trow >= s, sc[s * c:(s + 1) * c], 0.0) * vc[s:s + 1]
            st = s_ref[h]
            last = cum[c - 1:c]
            o = o + _dot_nt((qc * jnp.exp(cum)).astype(BF16), st.astype(BF16))
            kte = (kc * jnp.exp(last - cum)).astype(BF16)
            du = lax.dot_general(vc.astype(BF16), kte, (((0,), (0,)), ((), ())),
                                 preferred_element_type=F32)
            s_ref[h] = jnp.exp(last) * st + du
            o_ref[pl.ds(r0, c), cols] = o
        return carry

    lax.fori_loop(0, tb // c, chunk, 0)

    gain = gain_ref[...]
    outs = []
    for h in range(HG_HEADS):
        cols = slice(h * HG_DK, (h + 1) * HG_DK)
        o = o_ref[:, cols]
        o = o * lax.rsqrt(jnp.mean(o * o, axis=-1, keepdims=True) + RMS_EPS) * gain
        outs.append(o * _silu(hg_ref[:, cols]))
    y_ref[...] = jnp.concatenate(outs, axis=-1).astype(y_ref.dtype)

    @pl.when(i == nt - 1)
    def _():
        for h in range(HG_HEADS):
            st_ref[h] = s_ref[h].T


def _hgrn_prompt(proj, lb_param, gain, layer, bsz, seq, *, tb=256):
    tb = min(tb, seq)
    nt = seq // tb
    cblk = lambda c: pl.BlockSpec((tb, HG_DIM), lambda b, i, c=c: (b * nt + i, c))
    return pl.pallas_call(
        functools.partial(_hgrn_kernel, layer=layer),
        out_shape=(jax.ShapeDtypeStruct((bsz * seq, HG_DIM), BF16),
                   jax.ShapeDtypeStruct((bsz, HG_HEADS, HG_DK, HG_DK), F32)),
        grid=(bsz, nt),
        in_specs=[cblk(OFF_HQ // HG_DIM), cblk(OFF_HF // HG_DIM),
                  cblk(OFF_HI // HG_DIM), cblk(OFF_HG // HG_DIM),
                  pl.BlockSpec((DEPTH, HG_DIM), lambda b, i: (0, 0)),
                  pl.BlockSpec((1, HG_DK), lambda b, i: (0, 0))],
        out_specs=(pl.BlockSpec((tb, HG_DIM), lambda b, i: (b * nt + i, 0)),
                   pl.BlockSpec((None, HG_HEADS, HG_DK, HG_DK), lambda b, i: (b, 0, 0, 0))),
        scratch_shapes=[pltpu.VMEM((tb, HG_DIM), F32),
                        pltpu.VMEM((tb, HG_DIM), F32),
                        pltpu.VMEM((tb, HG_DIM), F32),
                        pltpu.VMEM((tb, HG_DIM), F32),
                        pltpu.VMEM((HG_HEADS, HG_DK, HG_DK), F32)],
        compiler_params=_cparams("arbitrary", "arbitrary"),
        name="hgrn_prompt",
    )(proj, proj, proj, proj, lb_param, gain)


def _attn_kernel(qa_ref, qb_ref, kc_ref, vc_ref, kp_ref, vp_ref, sink_ref, y_ref):
    n = pl.program_id(1)
    w = WINDOW
    qi = lax.broadcasted_iota(I32, (Q_PER_KV * w, 2 * w), 0) % w
    kj = lax.broadcasted_iota(I32, (Q_PER_KV * w, 2 * w), 1)
    dist = qi + w - kj
    valid = (dist >= 0) & (dist < w) & ((kj >= w) | (n > 0))
    distf = dist.astype(F32)
    grp = lax.broadcasted_iota(I32, (Q_PER_KV * w, 1), 0) // w
    sinks = sink_ref[...]
    outs = []
    for kv in range(N_KV_HEADS):
        q_ref = qa_ref if kv < 2 else qb_ref
        qoff = (kv % 2) * Q_PER_KV * HEAD_DIM
        q = jnp.concatenate(
            [q_ref[:, qoff + g * HEAD_DIM: qoff + (g + 1) * HEAD_DIM] for g in range(Q_PER_KV)],
            axis=0).astype(BF16)
        ks = slice(kv * HEAD_DIM, (kv + 1) * HEAD_DIM)
        k = jnp.concatenate([kp_ref[:, ks], kc_ref[:, ks]], axis=0).astype(BF16)
        v = jnp.concatenate([vp_ref[:, ks], vc_ref[:, ks]], axis=0).astype(BF16)
        slope = jnp.zeros((Q_PER_KV * w, 1), F32)
        sink = jnp.zeros((Q_PER_KV * w, 1), F32)
        for g in range(Q_PER_KV):
            hd = kv * Q_PER_KV + g
            slope = jnp.where(grp == g, _alibi_slope(hd), slope)
            sink = jnp.where(grp == g, sinks[:, hd:hd + 1], sink)
        s = _dot_nt(q, k) * (HEAD_DIM ** -0.5) - slope * distf
        s = jnp.where(valid, s, -jnp.inf)
        m = jnp.maximum(jnp.max(s, axis=-1, keepdims=True), sink)
        p = jnp.exp(s - m)
        denom = jnp.sum(p, axis=-1, keepdims=True) + jnp.exp(sink - m)
        o = _dot(p.astype(BF16), v) / denom
        for g in range(Q_PER_KV):
            outs.append(o[g * w:(g + 1) * w])
    y_ref[...] = jnp.concatenate(outs, axis=-1).astype(y_ref.dtype)


def _attn_prompt(proj, sinks, bsz, seq):
    w = WINDOW
    nb = seq // w
    half = ATTN_DIM // 2
    cur = lambda width, off: pl.BlockSpec(
        (w, width), lambda b, n: (b * nb + n, off // width))
    prev = lambda width, off: pl.BlockSpec(
        (w, width), lambda b, n: (b * nb + jnp.maximum(n - 1, 0), off // width))
    return pl.pallas_call(
        _attn_kernel,
        out_shape=jax.ShapeDtypeStruct((bsz * seq, ATTN_DIM), BF16),
        grid=(bsz, nb),
        in_specs=[cur(half, OFF_AQ), cur(half, OFF_AQ + half),
                  cur(KV_DIM, OFF_AK), cur(KV_DIM, OFF_AV),
                  prev(KV_DIM, OFF_AK), prev(KV_DIM, OFF_AV),
                  pl.BlockSpec((1, N_HEADS), lambda b, n: (0, 0))],
        out_specs=pl.BlockSpec((w, ATTN_DIM), lambda b, n: (b * nb + n, 0)),
        compiler_params=_cparams("arbitrary", "arbitrary"),
        name="attn_prompt",
    )(proj, proj, proj, proj, proj, proj, sinks)


def _sample_mix_kernel(p_ref, sc_ref, s0_ref, kc_ref, vc_ref, cw_ref, lbp_ref,
                       gain_ref, sink_ref,
                       y_ref, sco_ref, so_ref, ko_ref, vo_ref, *, layer):
    p = p_ref[...]
    seg = lambda off, n: p[:, off:off + n]
    u = seg(OFF_CC, CONV_DIM) * seg(OFF_CH, CONV_DIM)
    hist = sc_ref[...]
    cw = cw_ref[...]
    conv = cw[0:1] * hist[0:1] + cw[1:2] * hist[1:2] + cw[2:3] * u
    ya = seg(OFF_CB, CONV_DIM) * conv
    sco_ref[...] = jnp.concatenate([hist[1:2], u], axis=0)
    lb = _hg_lower_bound(lbp_ref[...], layer)
    f = lb + (1.0 - lb) * _sigmoid(seg(OFF_HF, HG_DIM))
    g = jnp.log(f)
    kk = 1.0 - f
    q = seg(OFF_HQ, HG_DIM) * (HG_DK ** -0.5)
    v = seg(OFF_HI, HG_DIM)
    gate = seg(OFF_HG, HG_DIM)
    gain = gain_ref[...]
    yb = []
    for h in range(HG_HEADS):
        cols = slice(h * HG_DK, (h + 1) * HG_DK)
        s0 = s0_ref[h]
        eg = jnp.exp(g[:, cols])
        qe_col = _col_from_row(q[:, cols] * eg)
        o = (jnp.sum(q[:, cols] * kk[:, cols], axis=-1, keepdims=True) * v[:, cols]
             + jnp.sum(qe_col * s0, axis=0, keepdims=True))
        so_ref[h] = _col_from_row(eg) * s0 + _col_from_row(kk[:, cols]) * v[:, cols]
        o = o * lax.rsqrt(jnp.mean(o * o, axis=-1, keepdims=True) + RMS_EPS) * gain
        yb.append(o * _silu(gate[:, cols]))
    w = kc_ref.shape[0]
    kcache = kc_ref[...]
    vcache = vc_ref[...]
    knew = seg(OFF_AK, KV_DIM)
    vnew = seg(OFF_AV, KV_DIM)
    aq = seg(OFF_AQ, ATTN_DIM)
    sinks = sink_ref[...]
    kj = lax.broadcasted_iota(I32, (Q_PER_KV, w), 1)
    dist = w - kj
    valid = dist < WINDOW
    gi = lax.broadcasted_iota(I32, (Q_PER_KV, 1), 0)
    yc = []
    for kv in range(N_KV_HEADS):
        ks = slice(kv * HEAD_DIM, (kv + 1) * HEAD_DIM)
        qh = jnp.concatenate(
            [aq[:, (kv * Q_PER_KV + gq) * HEAD_DIM:(kv * Q_PER_KV + gq + 1) * HEAD_DIM]
             for gq in range(Q_PER_KV)], axis=0)
        slope = jnp.zeros((Q_PER_KV, 1), F32)
        sink = jnp.zeros((Q_PER_KV, 1), F32)
        for gq in range(Q_PER_KV):
            hd = kv * Q_PER_KV + gq
            slope = jnp.where(gi == gq, _alibi_slope(hd), slope)
            sink = jnp.where(gi == gq, sinks[:, hd:hd + 1], sink)
        scale = HEAD_DIM ** -0.5
        sc = _dot3_nt(qh, kcache[:, ks]) * scale - slope * dist.astype(F32)
        sc = jnp.where(valid, sc, -jnp.inf)
        sn = jnp.sum(qh * knew[:, ks], axis=-1, keepdims=True) * scale
        m = jnp.maximum(jnp.maximum(jnp.max(sc, axis=-1, keepdims=True), sn), sink)
        pc = jnp.exp(sc - m)
        pn = jnp.exp(sn - m)
        denom = jnp.sum(pc, axis=-1, keepdims=True) + pn + jnp.exp(sink - m)
        o = (_dot3(pc, vcache[:, ks]) + pn * vnew[:, ks]) / denom
        for gq in range(Q_PER_KV):
            yc.append(o[gq:gq + 1])
    y_ref[...] = jnp.concatenate([ya] + yb + yc, axis=-1)
    row = lax.broadcasted_iota(I32, (w, KV_DIM), 0)
    ko_ref[...] = jnp.where(row == w - 1, knew, pltpu.roll(kcache, w - 1, 0))
    vo_ref[...] = jnp.where(row == w - 1, vnew, pltpu.roll(vcache, w - 1, 0))


def _sample_mixers(proj, state_conv, state_hgrn, cache_k, cache_v, conv_w,
                   lb_param, gain, sinks, layer):
    nb = proj.shape[0]
    w = cache_k.shape[1]
    per_b = lambda *shape: pl.BlockSpec((None,) + shape,
                                        lambda b: (b,) + (0,) * len(shape))
    whole = lambda *shape: pl.BlockSpec(shape, lambda b: (0,) * len(shape))
    return pl.pallas_call(
        functools.partial(_sample_mix_kernel, layer=layer),
        out_shape=(jax.ShapeDtypeStruct((nb, 1, D_MODEL), F32),
                   jax.ShapeDtypeStruct((nb, CONV_W - 1, CONV_DIM), F32),
                   jax.ShapeDtypeStruct((nb, HG_HEADS, HG_DK, HG_DK), F32),
                   jax.ShapeDtypeStruct((nb, w, KV_DIM), F32),
                   jax.ShapeDtypeStruct((nb, w, KV_DIM), F32)),
        grid=(nb,),
        in_specs=[per_b(1, IN_DIM), per_b(CONV_W - 1, CONV_DIM),
                  per_b(HG_HEADS, HG_DK, HG_DK), per_b(w, KV_DIM), per_b(w, KV_DIM),
                  whole(CONV_W, CONV_DIM), whole(DEPTH, HG_DIM), whole(1, HG_DK),
                  whole(1, N_HEADS)],
        out_specs=(per_b(1, D_MODEL), per_b(CONV_W - 1, CONV_DIM),
                   per_b(HG_HEADS, HG_DK, HG_DK), per_b(w, KV_DIM), per_b(w, KV_DIM)),
        compiler_params=_cparams("arbitrary"),
        name="sample_mixers",
    )(proj.reshape(nb, 1, IN_DIM), state_conv, state_hgrn,
      cache_k.reshape(nb, w, KV_DIM), cache_v.reshape(nb, w, KV_DIM),
      conv_w, lb_param, gain, sinks)


def _layer_norm(z, g, b):
    mu = jnp.mean(z, axis=-1, keepdims=True)
    zc = z - mu
    var = jnp.mean(zc * zc, axis=-1, keepdims=True)
    return zc * lax.rsqrt(var + LN_EPS) * g + b


def _route(logits, bias):
    lg = logits + bias
    rowv = lambda r: lg[r:r + 1]
    best, gidx = rowv(0), jnp.zeros_like(rowv(0), dtype=I32)
    for r in range(1, N_GROUPS):
        upd = rowv(r) > best
        best = jnp.where(upd, rowv(r), best)
        gidx = jnp.where(upd, r, gidx)
    gden = sum(jnp.exp(rowv(r) - best) for r in range(N_GROUPS))
    gprob = 1.0 / gden
    ev = []
    for j in range(EXPERTS_PER_GROUP):
        val = rowv(N_GROUPS + j)
        for grp in range(1, N_GROUPS):
            val = jnp.where(gidx == grp, rowv(N_GROUPS + grp * EXPERTS_PER_GROUP + j), val)
        ev.append(val)
    v1, j1 = ev[0], jnp.zeros_like(gidx)
    for j in range(1, EXPERTS_PER_GROUP):
        upd = ev[j] > v1
        v1 = jnp.where(upd, ev[j], v1)
        j1 = jnp.where(upd, j, j1)
    v2, j2 = jnp.full_like(v1, -jnp.inf), jnp.zeros_like(gidx)
    for j in range(EXPERTS_PER_GROUP):
        upd = (j1 != j) & (ev[j] > v2)
        v2 = jnp.where(upd, ev[j], v2)
        j2 = jnp.where(upd, j, j2)
    e2 = jnp.exp(v2 - v1)
    w1 = gprob / (1.0 + e2)
    w2 = gprob * e2 / (1.0 + e2)
    base = gidx * EXPERTS_PER_GROUP
    return (jnp.concatenate([base + j1, base + j2], axis=0),
            jnp.concatenate([w1, w2], axis=0))


def _outproj_kernel(*refs, n_y, precise, n_real):
    y_refs = refs[:n_y]
    x_ref, w_ref, g_ref, b_ref, rw_ref, rb_ref = refs[n_y:n_y + 6]
    x1_ref, x1t_ref, eid_ref, ewt_ref, acc_ref = refs[-5:]
    i = pl.program_id(0)
    j = pl.program_id(1)
    last_j = pl.num_programs(1) - 1
    tn = w_ref.shape[1]
    mm = _dot3 if precise else _dot

    @pl.when(i < n_real)
    def _():
        off = 0
        part = None
        for y_ref in y_refs:
            kk = y_ref.shape[1]
            t = mm(y_ref[...], w_ref[off:off + kk, :])
            part = t if part is None else part + t
            off += kk
        acc_ref[:, pl.ds(pl.multiple_of(j * tn, tn), tn)] = part

    @pl.when((i >= n_real) & (j == last_j))
    def _():
        x1_ref[...] = jnp.zeros_like(x1_ref)
        x1t_ref[...] = jnp.zeros_like(x1t_ref)

    @pl.when((i < n_real) & (j == last_j))
    def _():
        z = ALPHA * x_ref[...] + acc_ref[...]
        x1 = _layer_norm(z, g_ref[...], b_ref[...])
        x1_ref[...] = x1
        _store_token_major(x1t_ref, x1, TOKEN_ROWS)
        hi, lo = _split(x1)
        rw = rw_ref[...]
        rwh, rwl = _split(rw)
        logits = _dot_nt(rwh, hi) + (_dot_nt(rwl, hi) + _dot_nt(rwh, lo))
        eid, ewt = _route(logits, rb_ref[...])
        t = eid.shape[1]
        eid_ref[...] = jnp.concatenate([eid, jnp.zeros((SUBLANES - TOP_K, t), I32)], axis=0)
        ewt_ref[...] = jnp.concatenate([ewt, jnp.zeros((SUBLANES - TOP_K, t), F32)], axis=0)


def _outproj_ln_route(ys, x, w_out, layer, ln_g, ln_b, rw_t, rb, *, precise,
                      x1_rows, x1_row0=0, x1_buf=None, tm=512, tn=1024):
    m, d = x.shape
    tm = min(tm, m)
    assert x1_row0 % tm == 0
    n_y = len(ys)
    n_real = m // tm
    n_j = d // tn
    n_i = n_real if x1_buf is not None else pl.cdiv(x1_rows, tm)
    real = lambda i: jnp.minimum(i, n_real - 1)
    y_specs = [pl.BlockSpec((tm, y.shape[1]), lambda i, j: (real(i), 0)) for y in ys]
    vec = lambda: pl.BlockSpec((1, d), lambda i, j: (0, 0))
    operands = list(ys) + [x, w_out, ln_g, ln_b, rw_t, rb]
    in_specs = y_specs + [pl.BlockSpec((tm, d), lambda i, j: (real(i), 0)),
                          pl.BlockSpec((None, d, tn),
                                       lambda i, j: (layer, 0, jnp.where(i < n_real, j, n_j - 1))),
                          vec(), vec(),
                          pl.BlockSpec((ROUTE_ROWS, d), lambda i, j: (0, 0)),
                          pl.BlockSpec((ROUTE_ROWS, 1), lambda i, j: (0, 0))]
    aliases = {}
    if x1_buf is not None:
        aliases = {len(operands): 0, len(operands) + 1: 1}
        operands.extend(x1_buf)
        in_specs.extend([pl.BlockSpec(memory_space=pl.ANY)] * 2)
    return pl.pallas_call(
        functools.partial(_outproj_kernel, n_y=n_y, precise=precise, n_real=n_real),
        out_shape=(jax.ShapeDtypeStruct((x1_rows, d), F32),
                   jax.ShapeDtypeStruct((x1_rows * TOKEN_ROWS, LANES), F32),
                   jax.ShapeDtypeStruct((SUBLANES, m), I32),
                   jax.ShapeDtypeStruct((SUBLANES, m), F32)),
        grid=(n_i, n_j),
        in_specs=in_specs,
        out_specs=(pl.BlockSpec((tm, d), lambda i, j: (x1_row0 // tm + i, 0)),
                   pl.BlockSpec((tm * TOKEN_ROWS, LANES), lambda i, j: (x1_row0 // tm + i, 0)),
                   pl.BlockSpec((SUBLANES, tm), lambda i, j: (0, real(i))),
                   pl.BlockSpec((SUBLANES, tm), lambda i, j: (0, real(i)))),
        scratch_shapes=[pltpu.VMEM((tm, d), F32)],
        input_output_aliases=aliases,
        compiler_params=_cparams("arbitrary", "arbitrary"),
        name="outproj_ln_route",
    )(*operands)


def _slot_base(k, is_sample, n_prompt, n_sample):
    return k * n_prompt + is_sample * (TOP_K * n_prompt + k * (n_sample - n_prompt))


def _token_copy(src_ref, src_row, dst_ref, dst_row, sem):
    return pltpu.make_async_copy(src_ref.at[pl.ds(src_row, TOKEN_ROWS)],
                                 dst_ref.at[pl.ds(dst_row, TOKEN_ROWS)], sem)


def _ffn_kernel(pos_ref, vt_ref, ve_ref, lo_ref, hi_ref, nact_ref,
                x_hbm, wg_ref, wu_ref, wd_ref, y_hbm,
                tok_ref, slot_ref, xbuf, ybuf, semx, semy, *, n_prompt, n_sample, tile):
    v = pl.program_id(0)
    nact = nact_ref[0]
    n_tok = n_prompt + n_sample
    rows = TOP_K * n_tok
    dump0 = rows

    def gather(u, b):
        t0 = vt_ref[u] * tile

        def body(i, c):
            tok = tok_ref[jnp.minimum(t0 + i, rows - 1)]
            _token_copy(x_hbm, tok * TOKEN_ROWS, xbuf.at[b], i * TOKEN_PITCH, semx.at[b]).start()
            return c

        lax.fori_loop(0, tile, body, 0, unroll=8)

    def scatter(u, b):
        t0 = vt_ref[u] * tile
        lo, hi = lo_ref[u], hi_ref[u]

        def body(i, c):
            p = t0 + i
            mine = (p >= lo) & (p < hi)
            dst = jnp.where(mine, slot_ref[jnp.minimum(p, rows - 1)], dump0 + b * tile + i)
            _token_copy(ybuf.at[b], i * TOKEN_PITCH, y_hbm, dst * TOKEN_ROWS, semy.at[b]).start()
            return c

        lax.fori_loop(0, tile, body, 0, unroll=8)

    tile_rows = pl.ds(0, tile * TOKEN_ROWS)

    def wait_gather(b):
        pltpu.make_async_copy(x_hbm.at[tile_rows], xbuf.at[b].at[tile_rows], semx.at[b]).wait()

    def wait_scatter(b):
        pltpu.make_async_copy(ybuf.at[b].at[tile_rows], y_hbm.at[tile_rows], semy.at[b]).wait()

    @pl.when(v == 0)
    def _():
        dump = pltpu.make_async_copy(
            x_hbm.at[pl.ds(0, 2 * tile * TOKEN_ROWS)],
            y_hbm.at[pl.ds(dump0 * TOKEN_ROWS, 2 * tile * TOKEN_ROWS)], semy.at[0])
        dump.start()

        def invert(grp, c):
            for u in range(INVERT_UNROLL):
                s = grp * INVERT_UNROLL + u
                k = jnp.where(s >= n_tok, 1, 0)
                tok = s - k * n_tok
                p = pos_ref[s]
                tok_ref[p] = tok
                slot_ref[p] = jnp.where(tok < n_prompt,
                                        _slot_base(k, 0, n_prompt, n_sample) + tok,
                                        _slot_base(k, 1, n_prompt, n_sample) + tok - n_prompt)
            return c

        lax.fori_loop(0, nact_ref[1], invert, 0)
        dump.wait()
        gather(0, 0)

    @pl.when(v < nact)
    def _():
        b = lax.rem(v, 2)

        @pl.when(v + 1 < nact)
        def _():
            gather(v + 1, 1 - b)

        wait_gather(b)
        xb = _load_token_major(xbuf.at[b], tile, TOKEN_PITCH).astype(BF16)
        h = _silu(_dot(xb, wg_ref[...].astype(BF16))) * _dot(xb, wu_ref[...].astype(BF16))
        y = _dot(h.astype(BF16), wd_ref[...].astype(BF16))

        @pl.when(v >= 2)
        def _():
            wait_scatter(b)

        _store_token_major(ybuf.at[b], y, TOKEN_PITCH)
        scatter(v, b)

        @pl.when(v == nact - 1)
        def _():
            @pl.when(v >= 1)
            def _():
                wait_scatter(1 - b)

            wait_scatter(b)


def _expert_ffn(pos, plan, x1t, w_gate, w_up, w_down, layer, *, n_prompt, n_sample,
                tile=EXPERT_TILE):
    d, de = w_gate.shape[2], w_gate.shape[3]
    n_visits = plan[0].shape[0]
    rows = TOP_K * (n_prompt + n_sample)
    assert n_prompt + n_sample >= 2 * tile
    wspec = lambda r, c: pl.BlockSpec(
        (None, None, r, c), lambda v, pos, vt, ve, lo, hi, na: (layer, ve[v], 0, 0))
    any_spec = pl.BlockSpec(memory_space=pl.ANY)
    return pl.pallas_call(
        functools.partial(_ffn_kernel, n_prompt=n_prompt, n_sample=n_sample, tile=tile),
        out_shape=jax.ShapeDtypeStruct(((rows + 2 * tile) * TOKEN_ROWS, LANES), F32),
        grid_spec=pltpu.PrefetchScalarGridSpec(
            num_scalar_prefetch=6,
            grid=(n_visits,),
            in_specs=[any_spec, wspec(d, de), wspec(d, de), wspec(de, d)],
            out_specs=any_spec,
            scratch_shapes=[pltpu.SMEM((rows,), I32),
                            pltpu.SMEM((rows,), I32),
                            pltpu.VMEM((2, tile * TOKEN_PITCH, LANES), F32),
                            pltpu.VMEM((2, tile * TOKEN_PITCH, LANES), F32),
                            pltpu.SemaphoreType.DMA((2,)),
                            pltpu.SemaphoreType.DMA((2,))]),
        compiler_params=_cparams("arbitrary"),
        name="moe_ffn",
    )(pos, *plan, x1t, w_gate, w_up, w_down)


def _combine_kernel(x_ref, y0_ref, y1_ref, wt_ref, g_ref, b_ref, o_ref, ob_ref):
    tile = x_ref.shape[0]
    wt = wt_ref[...]
    ffn = (wt[:, 0:1] * _load_token_major(y0_ref, tile, TOKEN_ROWS)
           + wt[:, 1:2] * _load_token_major(y1_ref, tile, TOKEN_ROWS))
    out = _layer_norm(ALPHA * x_ref[...] + ffn, g_ref[...], b_ref[...])
    o_ref[...] = out
    ob_ref[...] = out.astype(BF16)


def _combine_ln(x1, y2, wt, ln_g, ln_b, *, row0, m, slots, tile=COMBINE_TILE):
    d = x1.shape[1]
    tile = min(tile, m)
    assert row0 % tile == 0 and slots[0] % tile == 0 and slots[1] % tile == 0
    ysrc = lambda base: pl.BlockSpec((tile * TOKEN_ROWS, LANES),
                                     lambda i: (base // tile + i, 0))
    vec = pl.BlockSpec((1, d), lambda i: (0, 0))
    out = pl.BlockSpec((tile, d), lambda i: (i, 0))
    return pl.pallas_call(
        _combine_kernel,
        out_shape=(jax.ShapeDtypeStruct((m, d), F32),
                   jax.ShapeDtypeStruct((m, d), BF16)),
        grid=(m // tile,),
        in_specs=[pl.BlockSpec((tile, d), lambda i: (row0 // tile + i, 0)),
                  ysrc(slots[0]), ysrc(slots[1]),
                  pl.BlockSpec((tile, TOP_K), lambda i: (i, 0)), vec, vec],
        out_specs=(out, out),
        compiler_params=_cparams("arbitrary"),
        name="moe_combine_ln",
    )(x1, y2, y2, wt, ln_g, ln_b)


def _route_plan(eid, tile):
    n_tok = eid.shape[1]
    flat = eid.reshape(-1)
    onehot = (flat[:, None] == jnp.arange(N_EXPERTS, dtype=I32)[None, :]).astype(I32)
    csum = jnp.cumsum(onehot, axis=0)
    counts = csum[-1]
    rank = jnp.sum(csum * onehot, axis=1) - 1
    ends = jnp.cumsum(counts)
    offs = ends - counts
    pos = (offs[flat] + rank).astype(I32)
    first_tile = offs // tile
    last_tile = (ends - 1) // tile
    nvis = jnp.where(counts > 0, last_tile - first_tile + 1, 0)
    vend = jnp.cumsum(nvis)
    vbase = vend - nvis
    nact = vend[-1]
    n_visits = pl.cdiv(TOP_K * n_tok, tile) + N_EXPERTS - 1
    v = jnp.minimum(jnp.arange(n_visits, dtype=I32), nact - 1)
    e = jnp.sum((vend[None, :] <= v[:, None]).astype(I32), axis=1)
    t = first_tile[e] + (v - vbase[e])
    assert (TOP_K * n_tok) % INVERT_UNROLL == 0
    counts_smem = jnp.stack([nact.astype(I32), jnp.asarray(TOP_K * n_tok // INVERT_UNROLL, I32)])
    plan = (t.astype(I32), e.astype(I32), offs[e].astype(I32), ends[e].astype(I32), counts_smem)
    return pos, plan


def kernel(x_prompt, x_sample, state_conv, state_hgrn, cache_k_win, cache_v_win, w_in, w_out, conv_w, hg_lb_param, hg_gain, attn_sinks, ln1_g, ln1_b, w_group, b_group, w_router, b_router, w_gate, w_up, w_down, ln2_g, ln2_b):
    bsz, seq, d = x_prompt.shape
    nb = x_sample.shape[0]
    n_prompt = bsz * seq
    n_tok = n_prompt + nb
    w_buf = cache_k_win.shape[2]

    xp = x_prompt.reshape(n_prompt, d)
    xp_mm = xp
    xs = x_sample.reshape(nb, d)
    pad_rows = ROUTE_ROWS - N_GROUPS - N_EXPERTS
    w_out_b = w_out.astype(BF16)
    outs = {k: [] for k in ("cp", "cs", "hp", "hs", "kp", "ks", "vp", "vs")}
    for l in range(DEPTH):
        gain = hg_gain[l].reshape(1, HG_DK)
        sinks = attn_sinks[l].reshape(1, N_HEADS)
        rw_t = jnp.concatenate([w_group[l].T, w_router[l].T, jnp.zeros((pad_rows, d), F32)], axis=0)
        rb = jnp.concatenate([b_group[l], b_router[l], jnp.zeros((pad_rows,), F32)]).reshape(ROUTE_ROWS, 1)
        g1, b1 = ln1_g[l].reshape(1, d), ln1_b[l].reshape(1, d)
        g2, b2 = ln2_g[l].reshape(1, d), ln2_b[l].reshape(1, d)

        proj = _inproj(xp_mm, w_in, l)
        ya, ctail = _conv_prompt(proj, conv_w[l], bsz, seq)
        yb, hstate = _hgrn_prompt(proj, hg_lb_param, gain, l, bsz, seq)
        yc = _attn_prompt(proj, sinks, bsz, seq)
        x1, x1t, eid_p, ewt_p = _outproj_ln_route(
            [ya, yb, yc], xp, w_out_b, l, g1, b1, rw_t, rb, precise=False,
            x1_rows=n_tok)
        outs["cp"].append(ctail[:, SUBLANES - (CONV_W - 1):])
        outs["hp"].append(hstate)
        kv = proj.reshape(bsz, seq, IN_DIM)[:, seq - w_buf:]
        outs["kp"].append(kv[:, :, OFF_AK:OFF_AK + KV_DIM].reshape(bsz, w_buf, N_KV_HEADS, HEAD_DIM))
        outs["vp"].append(kv[:, :, OFF_AV:OFF_AV + KV_DIM].reshape(bsz, w_buf, N_KV_HEADS, HEAD_DIM))

        proj_s = _inproj3(xs, w_in, l)
        ysm, cst, hst, kst, vst = _sample_mixers(
            proj_s, state_conv[l], state_hgrn[l], cache_k_win[l], cache_v_win[l],
            conv_w[l], hg_lb_param, gain, sinks, l)
        x1, x1t, eid_s, ewt_s = _outproj_ln_route(
            [ysm.reshape(nb, d)], xs, w_out, l, g1, b1, rw_t, rb, precise=True,
            x1_rows=n_tok, x1_row0=n_prompt, x1_buf=(x1, x1t))
        outs["cs"].append(cst)
        outs["hs"].append(hst)
        outs["ks"].append(kst.reshape(nb, w_buf, N_KV_HEADS, HEAD_DIM))
        outs["vs"].append(vst.reshape(nb, w_buf, N_KV_HEADS, HEAD_DIM))

        eid = jnp.concatenate([eid_p[:TOP_K], eid_s[:TOP_K]], axis=1)
        pos, plan = _route_plan(eid, EXPERT_TILE)
        y2 = _expert_ffn(pos, plan, x1t, w_gate, w_up, w_down, l,
                         n_prompt=n_prompt, n_sample=nb)
        slots = lambda smp: [_slot_base(k, smp, n_prompt, nb) for k in range(TOP_K)]
        xp, xp_mm = _combine_ln(x1, y2, ewt_p[:TOP_K].T, g2, b2,
                                row0=0, m=n_prompt, slots=slots(0))
        xs, _ = _combine_ln(x1, y2, ewt_s[:TOP_K].T, g2, b2,
                            row0=n_prompt, m=nb, slots=slots(1))

    st = lambda k: jnp.stack(outs[k])
    return (xp.reshape(bsz, seq, d), xs.reshape(nb, 1, d), st("cp"), st("cs"),
            st("hp"), st("hs"), st("kp"), st("ks"), st("vp"), st("vs"))
```

```python
import functools

import jax
import jax.numpy as jnp
import numpy as np
from jax import lax
from jax.experimental import pallas as pl
from jax.experimental.pallas import tpu as pltpu

F32 = jnp.float32
BF16 = jnp.bfloat16
I32 = jnp.int32

D_MODEL = 2048
DEPTH = 2
CONV_DIM = 512
CONV_W = 3
HG_DIM = 512
HG_HEADS = 4
HG_DK = 128
HG_CHUNK = 16
HEAD_DIM = 64
ATTN_DIM = 1024
N_HEADS = 16
N_KV_HEADS = 4
Q_PER_KV = 4
KV_DIM = 256
WINDOW = 128
N_GROUPS = 4
EXPERTS_PER_GROUP = 4
N_EXPERTS = 16
TOP_K = 2
D_EXPERT = 512
ALPHA = (2 * DEPTH) ** 0.25
LOG2E = 1.4426950408889634
LN_EPS = 1e-5
RMS_EPS = 1e-6
IN_DIM = 5120
OFF_CB, OFF_CC, OFF_CH = 0, 512, 1024
OFF_HQ, OFF_HF, OFF_HI, OFF_HG = 1536, 2048, 2560, 3072
OFF_AQ, OFF_AK, OFF_AV = 3584, 4608, 4864

VMEM_LIMIT_BYTES = 56 * 1024 * 1024
LANES = 128
SUBLANES = 8

EXPERT_TILE = 512
COMBINE_TILE = 512
INVERT_UNROLL = 8
ROUTE_ROWS = 32


TOKEN_ROWS = D_MODEL // LANES
TOKEN_PITCH = 20


def _store_token_major(ref, x, pitch):
    n = x.shape[0]
    for c in range(TOKEN_ROWS):
        ref[pl.ds(c, n, stride=pitch), :] = x[:, c * LANES:(c + 1) * LANES]


def _zero_token_pad(ref, n, pitch):
    for c in range(TOKEN_ROWS, pitch):
        ref[pl.ds(c, n, stride=pitch), :] = jnp.zeros((n, LANES), ref.dtype)


def _load_token_major(ref, n, pitch):
    return jnp.concatenate(
        [ref[pl.ds(c, n, stride=pitch), :] for c in range(TOKEN_ROWS)], axis=1)


def _cparams(*sem):
    return pltpu.CompilerParams(dimension_semantics=sem,
                                vmem_limit_bytes=VMEM_LIMIT_BYTES)


def _dot(a, b):
    return jnp.dot(a, b, preferred_element_type=F32)


def _dot_nt(a, b):
    return lax.dot_general(a, b, (((1,), (1,)), ((), ())),
                           preferred_element_type=F32)


def _split(x):
    hi = x.astype(BF16)
    lo = (x - hi.astype(F32)).astype(BF16)
    return hi, lo


def _dot3(a, b):
    ah, al = _split(a)
    bh, bl = _split(b)
    return _dot(ah, bh) + (_dot(ah, bl) + _dot(al, bh))


def _dot3_nt(a, b):
    ah, al = _split(a)
    bh, bl = _split(b)
    return _dot_nt(ah, bh) + (_dot_nt(ah, bl) + _dot_nt(al, bh))


def _sigmoid(x):
    return 1.0 / (1.0 + jnp.exp(-x))


def _silu(x):
    return x * _sigmoid(x)


def _col_from_row(row):
    n = row.shape[1]
    eye = (lax.broadcasted_iota(I32, (n, n), 0)
           == lax.broadcasted_iota(I32, (n, n), 1))
    return jnp.sum(jnp.where(eye, row, 0.0), axis=1, keepdims=True)


def _hg_lower_bound(lbp, layer):
    m = jnp.max(lbp, axis=0, keepdims=True)
    e = jnp.exp(lbp - m)
    soft = e / jnp.sum(e, axis=0, keepdims=True)
    acc = soft[0:1]
    for i in range(1, layer + 1):
        acc = acc + soft[i:i + 1]
    return acc - soft[0:1]


def _alibi_slope(head):
    return float(2.0 ** (-8.0 * (head + 1) / N_HEADS))


def _inproj_kernel(x_ref, w_ref, o_ref, xb_ref):
    @pl.when(pl.program_id(1) == 0)
    def _():
        xb_ref[...] = x_ref[...].astype(BF16)

    o_ref[...] = _dot(xb_ref[...], w_ref[...].astype(BF16))


def _inproj(x, w, layer, *, tm=1024, tn=1024):
    m, k = x.shape
    n = w.shape[2]
    tm = min(tm, m)
    return pl.pallas_call(
        _inproj_kernel,
        out_shape=jax.ShapeDtypeStruct((m, n), F32),
        grid=(m // tm, n // tn),
        in_specs=[pl.BlockSpec((tm, k), lambda i, j: (i, 0)),
                  pl.BlockSpec((None, k, tn), lambda i, j: (layer, 0, j))],
        out_specs=pl.BlockSpec((tm, tn), lambda i, j: (i, j)),
        scratch_shapes=[pltpu.VMEM((tm, k), BF16)],
        compiler_params=_cparams("arbitrary", "arbitrary"),
        name="inproj",
    )(x, w)


def _inproj3_kernel(x_ref, w_ref, o_ref):
    o_ref[...] = _dot3(x_ref[...], w_ref[...])


def _inproj3(x, w, layer, *, tn=512):
    m, k = x.shape
    n = w.shape[2]
    return pl.pallas_call(
        _inproj3_kernel,
        out_shape=jax.ShapeDtypeStruct((m, n), F32),
        grid=(n // tn,),
        in_specs=[pl.BlockSpec((m, k), lambda j: (0, 0)),
                  pl.BlockSpec((None, k, tn), lambda j: (layer, 0, j))],
        out_specs=pl.BlockSpec((m, tn), lambda j: (0, j)),
        compiler_params=_cparams("arbitrary"),
        name="inproj_sample",
    )(x, w)


def _conv_kernel(cb_ref, cc_ref, ch_ref, w_ref, y_ref, tail_ref, carry_ref):
    i = pl.program_id(1)

    @pl.when(i == 0)
    def _():
        carry_ref[...] = jnp.zeros_like(carry_ref)

    u = cc_ref[...] * ch_ref[...]
    tl = u.shape[0]
    row = lax.broadcasted_iota(I32, u.shape, 0)
    prev1 = carry_ref[SUBLANES - 1:SUBLANES, :]
    prev2 = carry_ref[SUBLANES - 2:SUBLANES - 1, :]
    u1 = jnp.where(row == 0, prev1, pltpu.roll(u, 1, 0))
    u2 = jnp.where(row == 0, prev2, jnp.where(row == 1, prev1, pltpu.roll(u, 2, 0)))
    w = w_ref[...]
    y = w[0:1] * u2 + w[1:2] * u1 + w[2:3] * u
    y_ref[...] = (cb_ref[...] * y).astype(y_ref.dtype)
    tail = u[tl - SUBLANES:tl, :]
    carry_ref[...] = tail
    tail_ref[...] = tail


def _conv_prompt(proj, conv_w, bsz, seq, *, tl=512):
    tl = min(tl, seq)
    nt = seq // tl
    cblk = lambda c: pl.BlockSpec((tl, CONV_DIM), lambda b, i, c=c: (b * nt + i, c))
    return pl.pallas_call(
        _conv_kernel,
        out_shape=(jax.ShapeDtypeStruct((bsz * seq, CONV_DIM), BF16),
                   jax.ShapeDtypeStruct((bsz, SUBLANES, CONV_DIM), F32)),
        grid=(bsz, nt),
        in_specs=[cblk(OFF_CB // CONV_DIM), cblk(OFF_CC // CONV_DIM),
                  cblk(OFF_CH // CONV_DIM),
                  pl.BlockSpec((CONV_W, CONV_DIM), lambda b, i: (0, 0))],
        out_specs=(pl.BlockSpec((tl, CONV_DIM), lambda b, i: (b * nt + i, 0)),
                   pl.BlockSpec((None, SUBLANES, CONV_DIM), lambda b, i: (b, 0, 0))),
        scratch_shapes=[pltpu.VMEM((SUBLANES, CONV_DIM), F32)],
        compiler_params=_cparams("arbitrary", "arbitrary"),
        name="conv_prompt",
    )(proj, proj, proj, conv_w)


def _seg_cumsum(g, seg):
    row = lax.broadcasted_iota(I32, g.shape, 0) % seg
    s = 1
    while s < seg:
        g = g + jnp.where(row >= s, pltpu.roll(g, s, 0), 0.0)
        s *= 2
    return g


def _hgrn_kernel(hq_ref, hf_ref, hi_ref, hg_ref, lbp_ref, gain_ref,
                 y_ref, st_ref, cum_ref, k_ref, q_ref, o_ref, s_ref, *, layer):
    i = pl.program_id(1)
    nt = pl.num_programs(1)
    tb = hq_ref.shape[0]
    c = HG_CHUNK

    @pl.when(i == 0)
    def _():
        s_ref[...] = jnp.zeros_like(s_ref)

    lb = _hg_lower_bound(lbp_ref[...], layer)
    f = lb + (1.0 - lb) * _sigmoid(hf_ref[...])
    cum_ref[...] = _seg_cumsum(jnp.log(f) * LOG2E, c)
    k_ref[...] = 1.0 - f
    q_ref[...] = hq_ref[...] * (HG_DK ** -0.5)

    ones = jnp.ones((HG_DK, HG_DK), BF16)
    hc = c // 2
    trow = lax.broadcasted_iota(I32, (hc, HG_DK), 0)

    def chunk(ci, carry):
        r0 = pl.multiple_of(ci * c, c)
        for h in range(HG_HEADS):
            cols = slice(h * HG_DK, (h + 1) * HG_DK)
            cum = cum_ref[pl.ds(r0, c), cols]
            kc = k_ref[pl.ds(r0, c), cols]
            qc = q_ref[pl.ds(r0, c), cols]
            vc = hi_ref[pl.ds(r0, c), cols]
            parts = []
            for s in range(c):
                cs, qk_hi = cum[s:s + 1], qc[hc:] * kc[s:s + 1]
                if s < hc:
                    dlo = jnp.exp2(jnp.where(trow >= s, cum[:hc] - cs, -jnp.inf))
                    parts.append((qc[:hc] * kc[s:s + 1] * dlo).astype(BF16))
                    dhi = jnp.exp2(cum[hc:] - cs)
                else:
                    dhi = jnp.exp2(jnp.where(trow >= s - hc, cum[hc:] - cs, -jnp.inf))
                parts.append((qk_hi * dhi).astype(BF16))
            sc = _dot(jnp.concatenate(parts, axis=0), ones)
            o_lo = jnp.zeros((hc, HG_DK), F32)
            o_hi = jnp.zeros((hc, HG_DK), F32)
            r = 0
            for s in range(c):
                if s < hc:
                    o_lo = o_lo + sc[r:r + hc] * vc[s:s + 1]
                    r += hc
                o_hi = o_hi + sc[r:r + hc] * vc[s:s + 1]
                r += hc
            o = jnp.concatenate([o_lo, o_hi], axis=0)
            st = s_ref[h]
            last = cum[c - 1:c]
            o = o + _dot_nt((qc * jnp.exp2(cum)).astype(BF16), st.astype(BF16))
            kte = (kc * jnp.exp2(last - cum)).astype(BF16)
            du = lax.dot_general(vc.astype(BF16), kte, (((0,), (0,)), ((), ())),
                                 preferred_element_type=F32)
            s_ref[h] = jnp.exp2(last) * st + du
            o_ref[pl.ds(r0, c), cols] = o
        return carry

    lax.fori_loop(0, tb // c, chunk, 0, unroll=4)

    gain = gain_ref[...]
    outs = []
    for h in range(HG_HEADS):
        cols = slice(h * HG_DK, (h + 1) * HG_DK)
        o = o_ref[:, cols]
        o = o * lax.rsqrt(jnp.mean(o * o, axis=-1, keepdims=True) + RMS_EPS) * gain
        outs.append(o * _silu(hg_ref[:, cols]))
    y_ref[...] = jnp.concatenate(outs, axis=-1).astype(y_ref.dtype)

    @pl.when(i == nt - 1)
    def _():
        for h in range(HG_HEADS):
            st_ref[h] = s_ref[h].T


def _hgrn_prompt(proj, lb_param, gain, layer, bsz, seq, *, tb=256):
    tb = min(tb, seq)
    nt = seq // tb
    cblk = lambda c: pl.BlockSpec((tb, HG_DIM), lambda b, i, c=c: (b * nt + i, c))
    return pl.pallas_call(
        functools.partial(_hgrn_kernel, layer=layer),
        out_shape=(jax.ShapeDtypeStruct((bsz * seq, HG_DIM), BF16),
                   jax.ShapeDtypeStruct((bsz, HG_HEADS, HG_DK, HG_DK), F32)),
        grid=(bsz, nt),
        in_specs=[cblk(OFF_HQ // HG_DIM), cblk(OFF_HF // HG_DIM),
                  cblk(OFF_HI // HG_DIM), cblk(OFF_HG // HG_DIM),
                  pl.BlockSpec((DEPTH, HG_DIM), lambda b, i: (0, 0)),
                  pl.BlockSpec((1, HG_DK), lambda b, i: (0, 0))],
        out_specs=(pl.BlockSpec((tb, HG_DIM), lambda b, i: (b * nt + i, 0)),
                   pl.BlockSpec((None, HG_HEADS, HG_DK, HG_DK), lambda b, i: (b, 0, 0, 0))),
        scratch_shapes=[pltpu.VMEM((tb, HG_DIM), F32),
                        pltpu.VMEM((tb, HG_DIM), F32),
                        pltpu.VMEM((tb, HG_DIM), F32),
                        pltpu.VMEM((tb, HG_DIM), F32),
                        pltpu.VMEM((HG_HEADS, HG_DK, HG_DK), F32)],
        compiler_params=_cparams("arbitrary", "arbitrary"),
        name="hgrn_prompt",
    )(proj, proj, proj, proj, lb_param, gain)


def _attn_kernel(qa_ref, qb_ref, kc_ref, vc_ref, kp_ref, vp_ref, sink_ref, y_ref):
    n = pl.program_id(1)
    w = WINDOW
    qi = lax.broadcasted_iota(I32, (Q_PER_KV * w, 2 * w), 0) % w
    kj = lax.broadcasted_iota(I32, (Q_PER_KV * w, 2 * w), 1)
    dist = qi + w - kj
    valid = (dist >= 0) & (dist < w) & ((kj >= w) | (n > 0))
    distf = dist.astype(F32)
    grp = lax.broadcasted_iota(I32, (Q_PER_KV * w, 1), 0) // w
    sinks = sink_ref[...]
    outs = []
    for kv in range(N_KV_HEADS):
        q_ref = qa_ref if kv < 2 else qb_ref
        qoff = (kv % 2) * Q_PER_KV * HEAD_DIM
        q = jnp.concatenate(
            [q_ref[:, qoff + g * HEAD_DIM: qoff + (g + 1) * HEAD_DIM] for g in range(Q_PER_KV)],
            axis=0).astype(BF16)
        ks = slice(kv * HEAD_DIM, (kv + 1) * HEAD_DIM)
        k = jnp.concatenate([kp_ref[:, ks], kc_ref[:, ks]], axis=0).astype(BF16)
        v = jnp.concatenate([vp_ref[:, ks], vc_ref[:, ks]], axis=0).astype(BF16)
        slope = jnp.zeros((Q_PER_KV * w, 1), F32)
        sink = jnp.zeros((Q_PER_KV * w, 1), F32)
        for g in range(Q_PER_KV):
            hd = kv * Q_PER_KV + g
            slope = jnp.where(grp == g, _alibi_slope(hd), slope)
            sink = jnp.where(grp == g, sinks[:, hd:hd + 1], sink)
        s = _dot_nt(q, k) * (HEAD_DIM ** -0.5) - slope * distf
        s = jnp.where(valid, s, -jnp.inf)
        m = jnp.maximum(jnp.max(s, axis=-1, keepdims=True), sink)
        p = jnp.exp(s - m)
        denom = jnp.sum(p, axis=-1, keepdims=True) + jnp.exp(sink - m)
        o = _dot(p.astype(BF16), v) / denom
        for g in range(Q_PER_KV):
            outs.append(o[g * w:(g + 1) * w])
    y_ref[...] = jnp.concatenate(outs, axis=-1).astype(y_ref.dtype)


def _attn_prompt(proj, sinks, bsz, seq):
    w = WINDOW
    nb = seq // w
    half = ATTN_DIM // 2
    cur = lambda width, off: pl.BlockSpec(
        (w, width), lambda b, n: (b * nb + n, off // width))
    prev = lambda width, off: pl.BlockSpec(
        (w, width), lambda b, n: (b * nb + jnp.maximum(n - 1, 0), off // width))
    return pl.pallas_call(
        _attn_kernel,
        out_shape=jax.ShapeDtypeStruct((bsz * seq, ATTN_DIM), BF16),
        grid=(bsz, nb),
        in_specs=[cur(half, OFF_AQ), cur(half, OFF_AQ + half),
                  cur(KV_DIM, OFF_AK), cur(KV_DIM, OFF_AV),
                  prev(KV_DIM, OFF_AK), prev(KV_DIM, OFF_AV),
                  pl.BlockSpec((1, N_HEADS), lambda b, n: (0, 0))],
        out_specs=pl.BlockSpec((w, ATTN_DIM), lambda b, n: (b * nb + n, 0)),
        compiler_params=_cparams("arbitrary", "arbitrary"),
        name="attn_prompt",
    )(proj, proj, proj, proj, proj, proj, sinks)


def _sample_mix_kernel(p_ref, sc_ref, s0_ref, kc_ref, vc_ref, cw_ref, lbp_ref,
                       gain_ref, sink_ref,
                       y_ref, sco_ref, so_ref, ko_ref, vo_ref, *, layer):
    p = p_ref[...]
    seg = lambda off, n: p[:, off:off + n]
    u = seg(OFF_CC, CONV_DIM) * seg(OFF_CH, CONV_DIM)
    hist = sc_ref[...]
    cw = cw_ref[...]
    conv = cw[0:1] * hist[0:1] + cw[1:2] * hist[1:2] + cw[2:3] * u
    ya = seg(OFF_CB, CONV_DIM) * conv
    sco_ref[...] = jnp.concatenate([hist[1:2], u], axis=0)
    lb = _hg_lower_bound(lbp_ref[...], layer)
    f = lb + (1.0 - lb) * _sigmoid(seg(OFF_HF, HG_DIM))
    g = jnp.log(f)
    kk = 1.0 - f
    q = seg(OFF_HQ, HG_DIM) * (HG_DK ** -0.5)
    v = seg(OFF_HI, HG_DIM)
    gate = seg(OFF_HG, HG_DIM)
    gain = gain_ref[...]
    yb = []
    for h in range(HG_HEADS):
        cols = slice(h * HG_DK, (h + 1) * HG_DK)
        s0 = s0_ref[h]
        eg = jnp.exp(g[:, cols])
        qe_col = _col_from_row(q[:, cols] * eg)
        o = (jnp.sum(q[:, cols] * kk[:, cols], axis=-1, keepdims=True) * v[:, cols]
             + jnp.sum(qe_col * s0, axis=0, keepdims=True))
        so_ref[h] = _col_from_row(eg) * s0 + _col_from_row(kk[:, cols]) * v[:, cols]
        o = o * lax.rsqrt(jnp.mean(o * o, axis=-1, keepdims=True) + RMS_EPS) * gain
        yb.append(o * _silu(gate[:, cols]))
    w = kc_ref.shape[0]
    kcache = kc_ref[...]
    vcache = vc_ref[...]
    knew = seg(OFF_AK, KV_DIM)
    vnew = seg(OFF_AV, KV_DIM)
    aq = seg(OFF_AQ, ATTN_DIM)
    sinks = sink_ref[...]
    kj = lax.broadcasted_iota(I32, (Q_PER_KV, w), 1)
    dist = w - kj
    valid = dist < WINDOW
    gi = lax.broadcasted_iota(I32, (Q_PER_KV, 1), 0)
    yc = []
    for kv in range(N_KV_HEADS):
        ks = slice(kv * HEAD_DIM, (kv + 1) * HEAD_DIM)
        qh = jnp.concatenate(
            [aq[:, (kv * Q_PER_KV + gq) * HEAD_DIM:(kv * Q_PER_KV + gq + 1) * HEAD_DIM]
             for gq in range(Q_PER_KV)], axis=0)
        slope = jnp.zeros((Q_PER_KV, 1), F32)
        sink = jnp.zeros((Q_PER_KV, 1), F32)
        for gq in range(Q_PER_KV):
            hd = kv * Q_PER_KV + gq
            slope = jnp.where(gi == gq, _alibi_slope(hd), slope)
            sink = jnp.where(gi == gq, sinks[:, hd:hd + 1], sink)
        scale = HEAD_DIM ** -0.5
        sc = _dot3_nt(qh, kcache[:, ks]) * scale - slope * dist.astype(F32)
        sc = jnp.where(valid, sc, -jnp.inf)
        sn = jnp.sum(qh * knew[:, ks], axis=-1, keepdims=True) * scale
        m = jnp.maximum(jnp.maximum(jnp.max(sc, axis=-1, keepdims=True), sn), sink)
        pc = jnp.exp(sc - m)
        pn = jnp.exp(sn - m)
        denom = jnp.sum(pc, axis=-1, keepdims=True) + pn + jnp.exp(sink - m)
        o = (_dot3(pc, vcache[:, ks]) + pn * vnew[:, ks]) / denom
        for gq in range(Q_PER_KV):
            yc.append(o[gq:gq + 1])
    y_ref[...] = jnp.concatenate([ya] + yb + yc, axis=-1)
    row = lax.broadcasted_iota(I32, (w, KV_DIM), 0)
    ko_ref[...] = jnp.where(row == w - 1, knew, pltpu.roll(kcache, w - 1, 0))
    vo_ref[...] = jnp.where(row == w - 1, vnew, pltpu.roll(vcache, w - 1, 0))


def _sample_mixers(proj, state_conv, state_hgrn, cache_k, cache_v, conv_w,
                   lb_param, gain, sinks, layer):
    nb = proj.shape[0]
    w = cache_k.shape[2]
    per_b = lambda *shape: pl.BlockSpec((None,) + shape,
                                        lambda b: (b,) + (0,) * len(shape))
    per_lb = lambda *shape: pl.BlockSpec((None, None) + shape,
                                         lambda b: (layer, b) + (0,) * len(shape))
    whole = lambda *shape: pl.BlockSpec(shape, lambda b: (0,) * len(shape))
    return pl.pallas_call(
        functools.partial(_sample_mix_kernel, layer=layer),
        out_shape=(jax.ShapeDtypeStruct((nb, 1, D_MODEL), F32),
                   jax.ShapeDtypeStruct((nb, CONV_W - 1, CONV_DIM), F32),
                   jax.ShapeDtypeStruct((nb, HG_HEADS, HG_DK, HG_DK), F32),
                   jax.ShapeDtypeStruct((nb, w, KV_DIM), F32),
                   jax.ShapeDtypeStruct((nb, w, KV_DIM), F32)),
        grid=(nb,),
        in_specs=[per_b(1, IN_DIM), per_lb(CONV_W - 1, CONV_DIM),
                  per_lb(HG_HEADS, HG_DK, HG_DK), per_lb(w, KV_DIM), per_lb(w, KV_DIM),
                  whole(CONV_W, CONV_DIM), whole(DEPTH, HG_DIM), whole(1, HG_DK),
                  whole(1, N_HEADS)],
        out_specs=(per_b(1, D_MODEL), per_b(CONV_W - 1, CONV_DIM),
                   per_b(HG_HEADS, HG_DK, HG_DK), per_b(w, KV_DIM), per_b(w, KV_DIM)),
        compiler_params=_cparams("arbitrary"),
        name="sample_mixers",
    )(proj.reshape(nb, 1, IN_DIM), state_conv, state_hgrn,
      cache_k.reshape(DEPTH, nb, w, KV_DIM), cache_v.reshape(DEPTH, nb, w, KV_DIM),
      conv_w, lb_param, gain, sinks)


def _layer_norm(z, g, b):
    mu = jnp.mean(z, axis=-1, keepdims=True)
    zc = z - mu
    var = jnp.mean(zc * zc, axis=-1, keepdims=True)
    return zc * lax.rsqrt(var + LN_EPS) * g + b


def _route(logits, bias):
    lg = logits + bias
    rowv = lambda r: lg[r:r + 1]
    best, gidx = rowv(0), jnp.zeros_like(rowv(0), dtype=I32)
    for r in range(1, N_GROUPS):
        upd = rowv(r) > best
        best = jnp.where(upd, rowv(r), best)
        gidx = jnp.where(upd, r, gidx)
    gden = sum(jnp.exp(rowv(r) - best) for r in range(N_GROUPS))
    gprob = 1.0 / gden
    ev = []
    for j in range(EXPERTS_PER_GROUP):
        val = rowv(N_GROUPS + j)
        for grp in range(1, N_GROUPS):
            val = jnp.where(gidx == grp, rowv(N_GROUPS + grp * EXPERTS_PER_GROUP + j), val)
        ev.append(val)
    v1, j1 = ev[0], jnp.zeros_like(gidx)
    for j in range(1, EXPERTS_PER_GROUP):
        upd = ev[j] > v1
        v1 = jnp.where(upd, ev[j], v1)
        j1 = jnp.where(upd, j, j1)
    v2, j2 = jnp.full_like(v1, -jnp.inf), jnp.zeros_like(gidx)
    for j in range(EXPERTS_PER_GROUP):
        upd = (j1 != j) & (ev[j] > v2)
        v2 = jnp.where(upd, ev[j], v2)
        j2 = jnp.where(upd, j, j2)
    e2 = jnp.exp(v2 - v1)
    w1 = gprob / (1.0 + e2)
    w2 = gprob * e2 / (1.0 + e2)
    base = gidx * EXPERTS_PER_GROUP
    return (jnp.concatenate([base + j1, base + j2], axis=0),
            jnp.concatenate([w1, w2], axis=0))


def _outproj_kernel(*refs, n_y, precise, n_real):
    y_refs = refs[:n_y]
    x_ref, w_ref, g_ref, b_ref, rw_ref, rb_ref = refs[n_y:n_y + 6]
    x1t_ref, eid_ref, ewt_ref, acc_ref = refs[-4:]
    i = pl.program_id(0)
    j = pl.program_id(1)
    n_j = pl.num_programs(1)
    tn = w_ref.shape[1]
    mm = _dot3 if precise else _dot

    @pl.when(i < n_real)
    def _():
        off = 0
        part = None
        for y_ref in y_refs:
            kk = y_ref.shape[1]
            t = mm(y_ref[...], w_ref[off:off + kk, :])
            part = t if part is None else part + t
            off += kk
        acc_ref[:, pl.ds(pl.multiple_of(j * tn, tn), tn)] = part

    @pl.when((i >= n_real) & (j == n_j - 1))
    def _():
        x1t_ref[...] = jnp.zeros_like(x1t_ref)

    @pl.when((i < n_real) & (j == n_j - 1))
    def _():
        z = ALPHA * x_ref[...] + acc_ref[...]
        x1 = _layer_norm(z, g_ref[...], b_ref[...])
        _store_token_major(x1t_ref, x1, TOKEN_PITCH)
        _zero_token_pad(x1t_ref, x1.shape[0], TOKEN_PITCH)
        hi, lo = _split(x1)
        rw = rw_ref[...]
        rwh, rwl = _split(rw)
        logits = _dot_nt(rwh, hi) + (_dot_nt(rwl, hi) + _dot_nt(rwh, lo))
        eid, ewt = _route(logits, rb_ref[...])
        t = eid.shape[1]
        eid_ref[...] = jnp.concatenate([eid, jnp.zeros((SUBLANES - TOP_K, t), I32)], axis=0)
        ewt_ref[...] = jnp.concatenate([ewt, jnp.zeros((SUBLANES - TOP_K, t), F32)], axis=0)


def _outproj_ln_route(ys, x, w_out, layer, ln_g, ln_b, rw_t, rb, *, precise,
                      x1_rows, x1_row0=0, x1_buf=None, tm=512, tn=1024):
    m, d = x.shape
    tm = min(tm, m)
    assert x1_row0 % tm == 0
    n_y = len(ys)
    n_real = m // tm
    n_j = d // tn
    n_i = n_real if x1_buf is not None else pl.cdiv(x1_rows, tm)
    real = lambda i: jnp.minimum(i, n_real - 1)
    y_specs = [pl.BlockSpec((tm, y.shape[1]), lambda i, j: (real(i), 0)) for y in ys]
    vec = lambda: pl.BlockSpec((1, d), lambda i, j: (0, 0))
    operands = list(ys) + [x, w_out, ln_g, ln_b, rw_t, rb]
    in_specs = y_specs + [pl.BlockSpec((tm, d), lambda i, j: (real(i), 0)),
                          pl.BlockSpec((None, d, tn),
                                       lambda i, j: (layer, 0, jnp.where(i < n_real, j, n_j - 1))),
                          vec(), vec(),
                          pl.BlockSpec((ROUTE_ROWS, d), lambda i, j: (0, 0)),
                          pl.BlockSpec((ROUTE_ROWS, 1), lambda i, j: (0, 0))]
    aliases = {}
    if x1_buf is not None:
        aliases = {len(operands): 0}
        operands.append(x1_buf)
        in_specs.append(pl.BlockSpec(memory_space=pl.ANY))
    return pl.pallas_call(
        functools.partial(_outproj_kernel, n_y=n_y, precise=precise, n_real=n_real),
        out_shape=(jax.ShapeDtypeStruct((x1_rows * TOKEN_PITCH, LANES), F32),
                   jax.ShapeDtypeStruct((SUBLANES, m), I32),
                   jax.ShapeDtypeStruct((SUBLANES, m), F32)),
        grid=(n_i, n_j),
        in_specs=in_specs,
        out_specs=(pl.BlockSpec((tm * TOKEN_PITCH, LANES), lambda i, j: (x1_row0 // tm + i, 0)),
                   pl.BlockSpec((SUBLANES, tm), lambda i, j: (0, real(i))),
                   pl.BlockSpec((SUBLANES, tm), lambda i, j: (0, real(i)))),
        scratch_shapes=[pltpu.VMEM((tm, d), F32)],
        input_output_aliases=aliases,
        compiler_params=_cparams("arbitrary", "arbitrary"),
        name="outproj_ln_route",
    )(*operands)


def _slot_base(k, is_sample, n_prompt, n_sample):
    return k * n_prompt + is_sample * (TOP_K * n_prompt + k * (n_sample - n_prompt))


def _token_copy(src_ref, src_row, dst_ref, dst_row, sem):
    return pltpu.make_async_copy(src_ref.at[pl.ds(src_row, TOKEN_ROWS)],
                                 dst_ref.at[pl.ds(dst_row, TOKEN_ROWS)], sem)


def _ffn_kernel(pos_ref, vt_ref, ve_ref, lo_ref, hi_ref, nact_ref,
                x_hbm, wg_ref, wu_ref, wd_ref, y_hbm,
                tok_ref, slot_ref, xbuf, ybuf, wgb, wub, wdb, semx, semy,
                *, n_prompt, n_sample, tile):
    v = pl.program_id(0)
    nact = nact_ref[0]
    n_tok = n_prompt + n_sample
    rows = TOP_K * n_tok
    dump0 = rows

    def gather(u, b):
        t0 = vt_ref[u] * tile

        def body(i, c):
            tok = tok_ref[jnp.minimum(t0 + i, rows - 1)]
            _token_copy(x_hbm, tok * TOKEN_PITCH, xbuf.at[b], i * TOKEN_PITCH, semx.at[b]).start()
            return c

        lax.fori_loop(0, tile, body, 0, unroll=8)

    def scatter(u):
        t0 = vt_ref[u] * tile
        lo, hi = lo_ref[u], hi_ref[u]

        def body(i, c):
            p = t0 + i
            mine = (p >= lo) & (p < hi)
            dst = jnp.where(mine, slot_ref[jnp.minimum(p, rows - 1)], dump0 + i)
            _token_copy(ybuf, i * TOKEN_PITCH, y_hbm, dst * TOKEN_ROWS, semy).start()
            return c

        lax.fori_loop(0, tile, body, 0, unroll=8)

    tile_rows = pl.ds(0, tile * TOKEN_ROWS)

    def wait_gather(b):
        pltpu.make_async_copy(x_hbm.at[tile_rows], xbuf.at[b].at[tile_rows], semx.at[b]).wait()

    def wait_scatter():
        pltpu.make_async_copy(ybuf.at[tile_rows], y_hbm.at[tile_rows], semy).wait()

    @pl.when(v == 0)
    def _():
        dump = pltpu.make_async_copy(
            x_hbm.at[tile_rows], y_hbm.at[pl.ds(dump0 * TOKEN_ROWS, tile * TOKEN_ROWS)], semy)
        dump.start()

        def invert(grp, c):
            for u in range(INVERT_UNROLL):
                s = grp * INVERT_UNROLL + u
                k = jnp.where(s >= n_tok, 1, 0)
                tok = s - k * n_tok
                p = pos_ref[s]
                tok_ref[p] = tok
                slot_ref[p] = jnp.where(tok < n_prompt,
                                        _slot_base(k, 0, n_prompt, n_sample) + tok,
                                        _slot_base(k, 1, n_prompt, n_sample) + tok - n_prompt)
            return c

        lax.fori_loop(0, nact_ref[1], invert, 0)
        dump.wait()
        gather(0, 0)

    @pl.when(v < nact)
    def _():
        b = lax.rem(v, 2)

        @pl.when(v + 1 < nact)
        def _():
            gather(v + 1, 1 - b)

        @pl.when((v == 0) | (ve_ref[v] != ve_ref[jnp.maximum(v - 1, 0)]))
        def _():
            wgb[...] = wg_ref[...].astype(BF16)
            wub[...] = wu_ref[...].astype(BF16)
            wdb[...] = wd_ref[...].astype(BF16)

        wait_gather(b)
        xb = _load_token_major(xbuf.at[b], tile, TOKEN_PITCH).astype(BF16)
        h = _silu(_dot(xb, wgb[...])) * _dot(xb, wub[...])
        y = _dot(h.astype(BF16), wdb[...])

        @pl.when(v >= 1)
        def _():
            wait_scatter()

        _store_token_major(ybuf, y, TOKEN_PITCH)
        scatter(v)

        @pl.when(v == nact - 1)
        def _():
            wait_scatter()


def _expert_ffn(pos, plan, x1t, w_gate, w_up, w_down, layer, *, n_prompt, n_sample,
                tile=EXPERT_TILE):
    d, de = w_gate.shape[2], w_gate.shape[3]
    n_visits = plan[0].shape[0]
    rows = TOP_K * (n_prompt + n_sample)
    assert n_prompt + n_sample >= tile
    wspec = lambda r, c: pl.BlockSpec(
        (None, None, r, c), lambda v, pos, vt, ve, lo, hi, na: (layer, ve[v], 0, 0))
    any_spec = pl.BlockSpec(memory_space=pl.ANY)
    return pl.pallas_call(
        functools.partial(_ffn_kernel, n_prompt=n_prompt, n_sample=n_sample, tile=tile),
        out_shape=jax.ShapeDtypeStruct(((rows + tile) * TOKEN_ROWS, LANES), F32),
        grid_spec=pltpu.PrefetchScalarGridSpec(
            num_scalar_prefetch=6,
            grid=(n_visits,),
            in_specs=[any_spec, wspec(d, de), wspec(d, de), wspec(de, d)],
            out_specs=any_spec,
            scratch_shapes=[pltpu.SMEM((rows,), I32),
                            pltpu.SMEM((rows,), I32),
                            pltpu.VMEM((2, tile * TOKEN_PITCH, LANES), F32),
                            pltpu.VMEM((tile * TOKEN_PITCH, LANES), F32),
                            pltpu.VMEM((d, de), BF16),
                            pltpu.VMEM((d, de), BF16),
                            pltpu.VMEM((de, d), BF16),
                            pltpu.SemaphoreType.DMA((2,)),
                            pltpu.SemaphoreType.DMA(())]),
        compiler_params=_cparams("arbitrary"),
        name="moe_ffn",
    )(pos, *plan, x1t, w_gate, w_up, w_down)


def _combine_kernel(x_ref, y0_ref, y1_ref, wt_ref, g_ref, b_ref, o_ref, ob_ref):
    tile = o_ref.shape[0]
    wt = wt_ref[...]
    ffn = (wt[:, 0:1] * _load_token_major(y0_ref, tile, TOKEN_ROWS)
           + wt[:, 1:2] * _load_token_major(y1_ref, tile, TOKEN_ROWS))
    x1 = _load_token_major(x_ref, tile, TOKEN_PITCH)
    out = _layer_norm(ALPHA * x1 + ffn, g_ref[...], b_ref[...])
    o_ref[...] = out
    ob_ref[...] = out.astype(BF16)


def _combine_ln(x1t, y2, wt, ln_g, ln_b, *, row0, m, slots, tile=COMBINE_TILE):
    d = ln_g.shape[1]
    tile = min(tile, m)
    assert row0 % tile == 0 and slots[0] % tile == 0 and slots[1] % tile == 0
    src = lambda base, pitch: pl.BlockSpec((tile * pitch, LANES),
                                           lambda i: (base // tile + i, 0))
    vec = pl.BlockSpec((1, d), lambda i: (0, 0))
    out = pl.BlockSpec((tile, d), lambda i: (i, 0))
    return pl.pallas_call(
        _combine_kernel,
        out_shape=(jax.ShapeDtypeStruct((m, d), F32),
                   jax.ShapeDtypeStruct((m, d), BF16)),
        grid=(m // tile,),
        in_specs=[src(row0, TOKEN_PITCH), src(slots[0], TOKEN_ROWS), src(slots[1], TOKEN_ROWS),
                  pl.BlockSpec((tile, TOP_K), lambda i: (i, 0)), vec, vec],
        out_specs=(out, out),
        compiler_params=_cparams("arbitrary"),
        name="moe_combine_ln",
    )(x1t, y2, y2, wt, ln_g, ln_b)


def _route_plan(eid, tile):
    n_tok = eid.shape[1]
    flat = eid.reshape(-1)
    onehot = (flat[:, None] == jnp.arange(N_EXPERTS, dtype=I32)[None, :]).astype(I32)
    blk = 256
    n_blk = pl.cdiv(TOP_K * n_tok, blk)
    oh = jnp.pad(onehot.astype(F32), ((0, n_blk * blk - TOP_K * n_tok), (0, 0)))
    oh = oh.reshape(n_blk, blk, N_EXPERTS)
    tril = (jnp.arange(blk)[:, None] >= jnp.arange(blk)[None, :]).astype(F32)
    within = jnp.einsum("ij,bjk->bik", tril, oh)
    blk_tot = within[:, -1, :]
    blk_off = jnp.cumsum(blk_tot, axis=0) - blk_tot
    csum = (within + blk_off[:, None, :]).reshape(n_blk * blk, N_EXPERTS)[:TOP_K * n_tok]
    csum = csum.astype(I32)
    counts = csum[-1]
    rank = jnp.sum(csum * onehot, axis=1) - 1
    ends = jnp.cumsum(counts)
    offs = ends - counts
    pos = (offs[flat] + rank).astype(I32)
    first_tile = offs // tile
    last_tile = (ends - 1) // tile
    nvis = jnp.where(counts > 0, last_tile - first_tile + 1, 0)
    vend = jnp.cumsum(nvis)
    vbase = vend - nvis
    nact = vend[-1]
    n_visits = pl.cdiv(TOP_K * n_tok, tile) + N_EXPERTS - 1
    v = jnp.minimum(jnp.arange(n_visits, dtype=I32), nact - 1)
    e = jnp.sum((vend[None, :] <= v[:, None]).astype(I32), axis=1)
    t = first_tile[e] + (v - vbase[e])
    assert (TOP_K * n_tok) % INVERT_UNROLL == 0
    counts_smem = jnp.stack([nact.astype(I32), jnp.asarray(TOP_K * n_tok // INVERT_UNROLL, I32)])
    plan = (t.astype(I32), e.astype(I32), offs[e].astype(I32), ends[e].astype(I32), counts_smem)
    return pos, plan


def kernel(x_prompt, x_sample, state_conv, state_hgrn, cache_k_win, cache_v_win, w_in, w_out, conv_w, hg_lb_param, hg_gain, attn_sinks, ln1_g, ln1_b, w_group, b_group, w_router, b_router, w_gate, w_up, w_down, ln2_g, ln2_b):
    bsz, seq, d = x_prompt.shape
    nb = x_sample.shape[0]
    n_prompt = bsz * seq
    n_tok = n_prompt + nb
    w_buf = cache_k_win.shape[2]

    xp = x_prompt.reshape(n_prompt, d)
    xp_mm = xp
    xs = x_sample.reshape(nb, d)
    pad_rows = ROUTE_ROWS - N_GROUPS - N_EXPERTS
    w_out_b = w_out.astype(BF16)
    outs = {k: [] for k in ("cp", "cs", "hp", "hs", "kp", "ks", "vp", "vs")}
    for l in range(DEPTH):
        gain = hg_gain[l].reshape(1, HG_DK)
        sinks = attn_sinks[l].reshape(1, N_HEADS)
        rw_t = jnp.concatenate([w_group[l].T, w_router[l].T, jnp.zeros((pad_rows, d), F32)], axis=0)
        rb = jnp.concatenate([b_group[l], b_router[l], jnp.zeros((pad_rows,), F32)]).reshape(ROUTE_ROWS, 1)
        g1, b1 = ln1_g[l].reshape(1, d), ln1_b[l].reshape(1, d)
        g2, b2 = ln2_g[l].reshape(1, d), ln2_b[l].reshape(1, d)

        proj = _inproj(xp_mm, w_in, l)
        ya, ctail = _conv_prompt(proj, conv_w[l], bsz, seq)
        yb, hstate = _hgrn_prompt(proj, hg_lb_param, gain, l, bsz, seq)
        yc = _attn_prompt(proj, sinks, bsz, seq)
        x1t, eid_p, ewt_p = _outproj_ln_route(
            [ya, yb, yc], xp, w_out_b, l, g1, b1, rw_t, rb, precise=False,
            x1_rows=n_tok, tn=d)
        outs["cp"].append(ctail[:, SUBLANES - (CONV_W - 1):])
        outs["hp"].append(hstate)
        kv = proj.reshape(bsz, seq, IN_DIM)[:, seq - w_buf:]
        outs["kp"].append(kv[:, :, OFF_AK:OFF_AK + KV_DIM].reshape(bsz, w_buf, N_KV_HEADS, HEAD_DIM))
        outs["vp"].append(kv[:, :, OFF_AV:OFF_AV + KV_DIM].reshape(bsz, w_buf, N_KV_HEADS, HEAD_DIM))

        proj_s = _inproj3(xs, w_in, l)
        ysm, cst, hst, kst, vst = _sample_mixers(
            proj_s, state_conv, state_hgrn, cache_k_win, cache_v_win,
            conv_w[l], hg_lb_param, gain, sinks, l)
        x1t, eid_s, ewt_s = _outproj_ln_route(
            [ysm.reshape(nb, d)], xs, w_out, l, g1, b1, rw_t, rb, precise=True,
            x1_rows=n_tok, x1_row0=n_prompt, x1_buf=x1t)
        outs["cs"].append(cst)
        outs["hs"].append(hst)
        outs["ks"].append(kst.reshape(nb, w_buf, N_KV_HEADS, HEAD_DIM))
        outs["vs"].append(vst.reshape(nb, w_buf, N_KV_HEADS, HEAD_DIM))

        eid = jnp.concatenate([eid_p[:TOP_K], eid_s[:TOP_K]], axis=1)
        pos, plan = _route_plan(eid, EXPERT_TILE)
        y2 = _expert_ffn(pos, plan, x1t, w_gate, w_up, w_down, l,
                         n_prompt=n_prompt, n_sample=nb)
        slots = lambda smp: [_slot_base(k, smp, n_prompt, nb) for k in range(TOP_K)]
        xp, xp_mm = _combine_ln(x1t, y2, ewt_p[:TOP_K].T, g2, b2,
                                row0=0, m=n_prompt, slots=slots(0))
        xs, _ = _combine_ln(x1t, y2, ewt_s[:TOP_K].T, g2, b2,
                            row0=n_prompt, m=nb, slots=slots(1))

    st = lambda k: jnp.stack(outs[k])
    return (xp.reshape(bsz, seq, d), xs.reshape(nb, 1, d), st("cp"), st("cs"),
            st("hp"), st("hs"), st("kp"), st("ks"), st("vp"), st("vs"))
```

```python
import functools

import jax
import jax.numpy as jnp
import numpy as np
from jax import lax
from jax.experimental import pallas as pl
from jax.experimental.pallas import tpu as pltpu

F32 = jnp.float32
BF16 = jnp.bfloat16
I32 = jnp.int32

D_MODEL = 2048
DEPTH = 2
CONV_DIM = 512
CONV_W = 3
HG_DIM = 512
HG_HEADS = 4
HG_DK = 128
HG_CHUNK = 16
HEAD_DIM = 64
ATTN_DIM = 1024
N_HEADS = 16
N_KV_HEADS = 4
Q_PER_KV = 4
KV_DIM = 256
WINDOW = 128
N_GROUPS = 4
EXPERTS_PER_GROUP = 4
N_EXPERTS = 16
TOP_K = 2
D_EXPERT = 512
ALPHA = (2 * DEPTH) ** 0.25
LOG2E = 1.4426950408889634
LN_EPS = 1e-5
RMS_EPS = 1e-6
IN_DIM = 5120
OFF_CB, OFF_CC, OFF_CH = 0, 512, 1024
OFF_HQ, OFF_HF, OFF_HI, OFF_HG = 1536, 2048, 2560, 3072
OFF_AQ, OFF_AK, OFF_AV = 3584, 4608, 4864

VMEM_LIMIT_BYTES = 56 * 1024 * 1024
LANES = 128
SUBLANES = 8

EXPERT_TILE = 256
FFN_PIECE_COLS = 256
COMBINE_TILE = 512
INVERT_UNROLL = 8
DMA_UNROLL = 8
ROUTE_ROWS = 32


TOKEN_ROWS = D_MODEL // LANES
TOKEN_PITCH = 20


def _store_token_major(ref, x, pitch):
    n = x.shape[0]
    for c in range(TOKEN_ROWS):
        ref[pl.ds(c, n, stride=pitch), :] = x[:, c * LANES:(c + 1) * LANES]


def _zero_token_pad(ref, n, pitch):
    for c in range(TOKEN_ROWS, pitch):
        ref[pl.ds(c, n, stride=pitch), :] = jnp.zeros((n, LANES), ref.dtype)


def _load_token_major(ref, n, pitch):
    return jnp.concatenate(
        [ref[pl.ds(c, n, stride=pitch), :] for c in range(TOKEN_ROWS)], axis=1)


def _cparams(*sem):
    return pltpu.CompilerParams(dimension_semantics=sem,
                                vmem_limit_bytes=VMEM_LIMIT_BYTES)


def _dot(a, b):
    return jnp.dot(a, b, preferred_element_type=F32)


def _dot_nt(a, b):
    return lax.dot_general(a, b, (((1,), (1,)), ((), ())),
                           preferred_element_type=F32)


def _split(x):
    hi = x.astype(BF16)
    lo = (x - hi.astype(F32)).astype(BF16)
    return hi, lo


def _dot3(a, b):
    ah, al = _split(a)
    bh, bl = _split(b)
    return _dot(ah, bh) + (_dot(ah, bl) + _dot(al, bh))


def _dot3_nt(a, b):
    ah, al = _split(a)
    bh, bl = _split(b)
    return _dot_nt(ah, bh) + (_dot_nt(ah, bl) + _dot_nt(al, bh))


def _sigmoid(x):
    return 1.0 / (1.0 + jnp.exp(-x))


def _silu(x):
    return x * _sigmoid(x)


def _col_from_row(row):
    n = row.shape[1]
    eye = (lax.broadcasted_iota(I32, (n, n), 0)
           == lax.broadcasted_iota(I32, (n, n), 1))
    return jnp.sum(jnp.where(eye, row, 0.0), axis=1, keepdims=True)


def _hg_lower_bound(lbp, layer):
    m = jnp.max(lbp, axis=0, keepdims=True)
    e = jnp.exp(lbp - m)
    soft = e / jnp.sum(e, axis=0, keepdims=True)
    acc = soft[0:1]
    for i in range(1, layer + 1):
        acc = acc + soft[i:i + 1]
    return acc - soft[0:1]


def _alibi_slope(head):
    return float(2.0 ** (-8.0 * (head + 1) / N_HEADS))


def _inproj_kernel(x_ref, w_ref, o_ref, xb_ref):
    @pl.when(pl.program_id(1) == 0)
    def _():
        xb_ref[...] = x_ref[...].astype(BF16)

    o_ref[...] = _dot(xb_ref[...], w_ref[...].astype(BF16))


def _inproj(x, w, layer, *, tm=1024, tn=1024):
    m, k = x.shape
    n = w.shape[2]
    tm = min(tm, m)
    return pl.pallas_call(
        _inproj_kernel,
        out_shape=jax.ShapeDtypeStruct((m, n), F32),
        grid=(m // tm, n // tn),
        in_specs=[pl.BlockSpec((tm, k), lambda i, j: (i, 0)),
                  pl.BlockSpec((None, k, tn), lambda i, j: (layer, 0, j))],
        out_specs=pl.BlockSpec((tm, tn), lambda i, j: (i, j)),
        scratch_shapes=[pltpu.VMEM((tm, k), BF16)],
        compiler_params=_cparams("arbitrary", "arbitrary"),
        name="inproj",
    )(x, w)


def _inproj3_kernel(x_ref, w_ref, o_ref):
    o_ref[...] = _dot3(x_ref[...], w_ref[...])


def _inproj3(x, w, layer, *, tn=512):
    m, k = x.shape
    n = w.shape[2]
    return pl.pallas_call(
        _inproj3_kernel,
        out_shape=jax.ShapeDtypeStruct((m, n), F32),
        grid=(n // tn,),
        in_specs=[pl.BlockSpec((m, k), lambda j: (0, 0)),
                  pl.BlockSpec((None, k, tn), lambda j: (layer, 0, j))],
        out_specs=pl.BlockSpec((m, tn), lambda j: (0, j)),
        compiler_params=_cparams("arbitrary"),
        name="inproj_sample",
    )(x, w)


def _conv_kernel(cb_ref, cc_ref, ch_ref, w_ref, y_ref, tail_ref, carry_ref):
    i = pl.program_id(1)

    @pl.when(i == 0)
    def _():
        carry_ref[...] = jnp.zeros_like(carry_ref)

    u = cc_ref[...] * ch_ref[...]
    tl = u.shape[0]
    row = lax.broadcasted_iota(I32, u.shape, 0)
    prev1 = carry_ref[SUBLANES - 1:SUBLANES, :]
    prev2 = carry_ref[SUBLANES - 2:SUBLANES - 1, :]
    u1 = jnp.where(row == 0, prev1, pltpu.roll(u, 1, 0))
    u2 = jnp.where(row == 0, prev2, jnp.where(row == 1, prev1, pltpu.roll(u, 2, 0)))
    w = w_ref[...]
    y = w[0:1] * u2 + w[1:2] * u1 + w[2:3] * u
    y_ref[...] = (cb_ref[...] * y).astype(y_ref.dtype)
    tail = u[tl - SUBLANES:tl, :]
    carry_ref[...] = tail
    tail_ref[...] = tail


def _conv_prompt(proj, conv_w, bsz, seq, *, tl=512):
    tl = min(tl, seq)
    nt = seq // tl
    cblk = lambda c: pl.BlockSpec((tl, CONV_DIM), lambda b, i, c=c: (b * nt + i, c))
    return pl.pallas_call(
        _conv_kernel,
        out_shape=(jax.ShapeDtypeStruct((bsz * seq, CONV_DIM), BF16),
                   jax.ShapeDtypeStruct((bsz, SUBLANES, CONV_DIM), F32)),
        grid=(bsz, nt),
        in_specs=[cblk(OFF_CB // CONV_DIM), cblk(OFF_CC // CONV_DIM),
                  cblk(OFF_CH // CONV_DIM),
                  pl.BlockSpec((CONV_W, CONV_DIM), lambda b, i: (0, 0))],
        out_specs=(pl.BlockSpec((tl, CONV_DIM), lambda b, i: (b * nt + i, 0)),
                   pl.BlockSpec((None, SUBLANES, CONV_DIM), lambda b, i: (b, 0, 0))),
        scratch_shapes=[pltpu.VMEM((SUBLANES, CONV_DIM), F32)],
        compiler_params=_cparams("arbitrary", "arbitrary"),
        name="conv_prompt",
    )(proj, proj, proj, conv_w)


def _seg_cumsum(g, seg):
    row = lax.broadcasted_iota(I32, g.shape, 0) % seg
    s = 1
    while s < seg:
        g = g + jnp.where(row >= s, pltpu.roll(g, s, 0), 0.0)
        s *= 2
    return g


def _hgrn_kernel(hq_ref, hf_ref, hi_ref, hg_ref, lbp_ref, gain_ref,
                 y_ref, st_ref, cum_ref, k_ref, q_ref, o_ref, s_ref, *, layer):
    i = pl.program_id(1)
    nt = pl.num_programs(1)
    tb = hq_ref.shape[0]
    c = HG_CHUNK

    @pl.when(i == 0)
    def _():
        s_ref[...] = jnp.zeros_like(s_ref)

    lb = _hg_lower_bound(lbp_ref[...], layer)
    f = lb + (1.0 - lb) * _sigmoid(hf_ref[...])
    cum_ref[...] = _seg_cumsum(jnp.log(f) * LOG2E, c)
    k_ref[...] = 1.0 - f
    q_ref[...] = hq_ref[...] * (HG_DK ** -0.5)

    ones = jnp.ones((HG_DK, HG_DK), BF16)
    hc = c // 2
    trow = lax.broadcasted_iota(I32, (hc, HG_DK), 0)

    def chunk(ci, carry):
        r0 = pl.multiple_of(ci * c, c)
        for h in range(HG_HEADS):
            cols = slice(h * HG_DK, (h + 1) * HG_DK)
            cum = cum_ref[pl.ds(r0, c), cols]
            kc = k_ref[pl.ds(r0, c), cols]
            qc = q_ref[pl.ds(r0, c), cols]
            vc = hi_ref[pl.ds(r0, c), cols]
            parts = []
            for s in range(c):
                cs, qk_hi = cum[s:s + 1], qc[hc:] * kc[s:s + 1]
                if s < hc:
                    dlo = jnp.exp2(jnp.where(trow >= s, cum[:hc] - cs, -jnp.inf))
                    parts.append((qc[:hc] * kc[s:s + 1] * dlo).astype(BF16))
                    dhi = jnp.exp2(cum[hc:] - cs)
                else:
                    dhi = jnp.exp2(jnp.where(trow >= s - hc, cum[hc:] - cs, -jnp.inf))
                parts.append((qk_hi * dhi).astype(BF16))
            sc = _dot(jnp.concatenate(parts, axis=0), ones)
            o_lo = jnp.zeros((hc, HG_DK), F32)
            o_hi = jnp.zeros((hc, HG_DK), F32)
            r = 0
            for s in range(c):
                if s < hc:
                    o_lo = o_lo + sc[r:r + hc] * vc[s:s + 1]
                    r += hc
                o_hi = o_hi + sc[r:r + hc] * vc[s:s + 1]
                r += hc
            o = jnp.concatenate([o_lo, o_hi], axis=0)
            st = s_ref[h]
            last = cum[c - 1:c]
            o = o + _dot_nt((qc * jnp.exp2(cum)).astype(BF16), st.astype(BF16))
            kte = (kc * jnp.exp2(last - cum)).astype(BF16)
            du = lax.dot_general(vc.astype(BF16), kte, (((0,), (0,)), ((), ())),
                                 preferred_element_type=F32)
            s_ref[h] = jnp.exp2(last) * st + du
            o_ref[pl.ds(r0, c), cols] = o
        return carry

    lax.fori_loop(0, tb // c, chunk, 0, unroll=4)

    gain = gain_ref[...]
    outs = []
    for h in range(HG_HEADS):
        cols = slice(h * HG_DK, (h + 1) * HG_DK)
        o = o_ref[:, cols]
        o = o * lax.rsqrt(jnp.mean(o * o, axis=-1, keepdims=True) + RMS_EPS) * gain
        outs.append(o * _silu(hg_ref[:, cols]))
    y_ref[...] = jnp.concatenate(outs, axis=-1).astype(y_ref.dtype)

    @pl.when(i == nt - 1)
    def _():
        for h in range(HG_HEADS):
            st_ref[h] = s_ref[h].T


def _hgrn_prompt(proj, lb_param, gain, layer, bsz, seq, *, tb=256):
    tb = min(tb, seq)
    nt = seq // tb
    cblk = lambda c: pl.BlockSpec((tb, HG_DIM), lambda b, i, c=c: (b * nt + i, c))
    return pl.pallas_call(
        functools.partial(_hgrn_kernel, layer=layer),
        out_shape=(jax.ShapeDtypeStruct((bsz * seq, HG_DIM), BF16),
                   jax.ShapeDtypeStruct((bsz, HG_HEADS, HG_DK, HG_DK), F32)),
        grid=(bsz, nt),
        in_specs=[cblk(OFF_HQ // HG_DIM), cblk(OFF_HF // HG_DIM),
                  cblk(OFF_HI // HG_DIM), cblk(OFF_HG // HG_DIM),
                  pl.BlockSpec((DEPTH, HG_DIM), lambda b, i: (0, 0)),
                  pl.BlockSpec((1, HG_DK), lambda b, i: (0, 0))],
        out_specs=(pl.BlockSpec((tb, HG_DIM), lambda b, i: (b * nt + i, 0)),
                   pl.BlockSpec((None, HG_HEADS, HG_DK, HG_DK), lambda b, i: (b, 0, 0, 0))),
        scratch_shapes=[pltpu.VMEM((tb, HG_DIM), F32),
                        pltpu.VMEM((tb, HG_DIM), F32),
                        pltpu.VMEM((tb, HG_DIM), F32),
                        pltpu.VMEM((tb, HG_DIM), F32),
                        pltpu.VMEM((HG_HEADS, HG_DK, HG_DK), F32)],
        compiler_params=_cparams("arbitrary", "arbitrary"),
        name="hgrn_prompt",
    )(proj, proj, proj, proj, lb_param, gain)


def _attn_kernel(qa_ref, qb_ref, kc_ref, vc_ref, kp_ref, vp_ref, sink_ref, y_ref,
                 bias_ref, sinkcol_ref):
    n = pl.program_id(1)
    w = WINDOW
    qi = lax.broadcasted_iota(I32, (Q_PER_KV * w, 2 * w), 0) % w
    kj = lax.broadcasted_iota(I32, (Q_PER_KV * w, 2 * w), 1)

    @pl.when((pl.program_id(0) == 0) & (n == 0))
    def _():
        dist = qi + w - kj
        valid = (dist >= 0) & (dist < w)
        distf = dist.astype(F32)
        for kv in range(N_KV_HEADS):
            grp = lax.broadcasted_iota(I32, (Q_PER_KV * w, 1), 0) // w
            slope = jnp.zeros((Q_PER_KV * w, 1), F32)
            for g in range(Q_PER_KV):
                slope = jnp.where(grp == g, _alibi_slope(kv * Q_PER_KV + g), slope)
            bias = jnp.where(valid, -slope * distf, -jnp.inf)
            bias_ref[1, pl.ds(kv * Q_PER_KV * w, Q_PER_KV * w), :] = bias
            bias_ref[0, pl.ds(kv * Q_PER_KV * w, Q_PER_KV * w), :] = jnp.where(
                kj >= w, bias, -jnp.inf)
        for h in range(N_HEADS):
            sinkcol_ref[pl.ds(h * w, w), :] = jnp.broadcast_to(sink_ref[:, h:h + 1], (w, LANES))

    low = lax.broadcasted_iota(I32, (1, LANES), 1) < HEAD_DIM
    ones = jnp.ones((2 * w, LANES), BF16)
    scores, values = [], []
    for kv in range(N_KV_HEADS):
        q_ref = qa_ref if kv < 2 else qb_ref
        qoff = (kv % 2) * Q_PER_KV * HEAD_DIM
        kt = slice((kv // 2) * LANES, (kv // 2 + 1) * LANES)

        def both_halves(prev_ref, cur_ref):
            t = jnp.concatenate([prev_ref[:, kt], cur_ref[:, kt]], axis=0)
            r = pltpu.roll(t, HEAD_DIM, 1)
            return (jnp.where(low, t, r) if kv % 2 == 0 else jnp.where(low, r, t)).astype(BF16)

        q_parts = []
        for g in range(Q_PER_KV):
            qt = q_ref[:, qoff + (g // 2) * LANES: qoff + (g // 2 + 1) * LANES]
            q_parts.append(jnp.where(low if g % 2 == 0 else ~low, qt, 0.0))
        q = jnp.concatenate(q_parts, axis=0).astype(BF16)
        scores.append(_dot_nt(q, both_halves(kp_ref, kc_ref)))
        values.append(both_halves(vp_ref, vc_ref))
    s = jnp.concatenate(scores, axis=0) * (HEAD_DIM ** -0.5) + bias_ref[jnp.minimum(n, 1)]
    sink = sinkcol_ref[...]
    m = jnp.maximum(jnp.max(s, axis=-1, keepdims=True), sink)
    p = jnp.exp(s - jnp.concatenate([m, m], axis=1)).astype(BF16)
    rows = Q_PER_KV * w
    outs = []
    for kv in range(N_KV_HEADS):
        pk = p[kv * rows:(kv + 1) * rows]
        sl = slice(kv * rows, (kv + 1) * rows)
        denom = _dot(pk, ones) + jnp.exp(sink[sl] - m[sl])
        o = _dot(pk, values[kv]) / denom
        for g in range(0, Q_PER_KV, 2):
            outs.append(jnp.where(low, o[g * w:(g + 1) * w], o[(g + 1) * w:(g + 2) * w]))
    y_ref[...] = jnp.concatenate(outs, axis=-1).astype(y_ref.dtype)


def _attn_prompt(proj, sinks, bsz, seq):
    w = WINDOW
    nb = seq // w
    half = ATTN_DIM // 2
    cur = lambda width, off: pl.BlockSpec(
        (w, width), lambda b, n: (b * nb + n, off // width))
    prev = lambda width, off: pl.BlockSpec(
        (w, width), lambda b, n: (b * nb + jnp.maximum(n - 1, 0), off // width))
    return pl.pallas_call(
        _attn_kernel,
        out_shape=jax.ShapeDtypeStruct((bsz * seq, ATTN_DIM), BF16),
        grid=(bsz, nb),
        in_specs=[cur(half, OFF_AQ), cur(half, OFF_AQ + half),
                  cur(KV_DIM, OFF_AK), cur(KV_DIM, OFF_AV),
                  prev(KV_DIM, OFF_AK), prev(KV_DIM, OFF_AV),
                  pl.BlockSpec((1, N_HEADS), lambda b, n: (0, 0))],
        out_specs=pl.BlockSpec((w, ATTN_DIM), lambda b, n: (b * nb + n, 0)),
        scratch_shapes=[pltpu.VMEM((2, N_HEADS * w, 2 * w), F32),
                        pltpu.VMEM((N_HEADS * w, LANES), F32)],
        compiler_params=_cparams("arbitrary", "arbitrary"),
        name="attn_prompt",
    )(proj, proj, proj, proj, proj, proj, sinks)


def _sample_mix_kernel(p_ref, sc_ref, s0_ref, kc_ref, vc_ref, cw_ref, lbp_ref,
                       gain_ref, sink_ref,
                       y_ref, sco_ref, so_ref, ko_ref, vo_ref, *, layer):
    p = p_ref[...]
    seg = lambda off, n: p[:, off:off + n]
    u = seg(OFF_CC, CONV_DIM) * seg(OFF_CH, CONV_DIM)
    hist = sc_ref[...]
    cw = cw_ref[...]
    conv = cw[0:1] * hist[0:1] + cw[1:2] * hist[1:2] + cw[2:3] * u
    ya = seg(OFF_CB, CONV_DIM) * conv
    sco_ref[...] = jnp.concatenate([hist[1:2], u], axis=0)
    lb = _hg_lower_bound(lbp_ref[...], layer)
    f = lb + (1.0 - lb) * _sigmoid(seg(OFF_HF, HG_DIM))
    g = jnp.log(f)
    kk = 1.0 - f
    q = seg(OFF_HQ, HG_DIM) * (HG_DK ** -0.5)
    v = seg(OFF_HI, HG_DIM)
    gate = seg(OFF_HG, HG_DIM)
    gain = gain_ref[...]
    yb = []
    for h in range(HG_HEADS):
        cols = slice(h * HG_DK, (h + 1) * HG_DK)
        s0 = s0_ref[h]
        eg = jnp.exp(g[:, cols])
        qe_col = _col_from_row(q[:, cols] * eg)
        o = (jnp.sum(q[:, cols] * kk[:, cols], axis=-1, keepdims=True) * v[:, cols]
             + jnp.sum(qe_col * s0, axis=0, keepdims=True))
        so_ref[h] = _col_from_row(eg) * s0 + _col_from_row(kk[:, cols]) * v[:, cols]
        o = o * lax.rsqrt(jnp.mean(o * o, axis=-1, keepdims=True) + RMS_EPS) * gain
        yb.append(o * _silu(gate[:, cols]))
    w = kc_ref.shape[0]
    kcache = kc_ref[...]
    vcache = vc_ref[...]
    knew = seg(OFF_AK, KV_DIM)
    vnew = seg(OFF_AV, KV_DIM)
    aq = seg(OFF_AQ, ATTN_DIM)
    sinks = sink_ref[...]
    kj = lax.broadcasted_iota(I32, (Q_PER_KV, w), 1)
    dist = w - kj
    valid = dist < WINDOW
    gi = lax.broadcasted_iota(I32, (Q_PER_KV, 1), 0)
    yc = []
    for kv in range(N_KV_HEADS):
        ks = slice(kv * HEAD_DIM, (kv + 1) * HEAD_DIM)
        qh = jnp.concatenate(
            [aq[:, (kv * Q_PER_KV + gq) * HEAD_DIM:(kv * Q_PER_KV + gq + 1) * HEAD_DIM]
             for gq in range(Q_PER_KV)], axis=0)
        slope = jnp.zeros((Q_PER_KV, 1), F32)
        sink = jnp.zeros((Q_PER_KV, 1), F32)
        for gq in range(Q_PER_KV):
            hd = kv * Q_PER_KV + gq
            slope = jnp.where(gi == gq, _alibi_slope(hd), slope)
            sink = jnp.where(gi == gq, sinks[:, hd:hd + 1], sink)
        scale = HEAD_DIM ** -0.5
        sc = _dot3_nt(qh, kcache[:, ks]) * scale - slope * dist.astype(F32)
        sc = jnp.where(valid, sc, -jnp.inf)
        sn = jnp.sum(qh * knew[:, ks], axis=-1, keepdims=True) * scale
        m = jnp.maximum(jnp.maximum(jnp.max(sc, axis=-1, keepdims=True), sn), sink)
        pc = jnp.exp(sc - m)
        pn = jnp.exp(sn - m)
        denom = jnp.sum(pc, axis=-1, keepdims=True) + pn + jnp.exp(sink - m)
        o = (_dot3(pc, vcache[:, ks]) + pn * vnew[:, ks]) / denom
        for gq in range(Q_PER_KV):
            yc.append(o[gq:gq + 1])
    y_ref[...] = jnp.concatenate([ya] + yb + yc, axis=-1)
    row = lax.broadcasted_iota(I32, (w, KV_DIM), 0)
    ko_ref[...] = jnp.where(row == w - 1, knew, pltpu.roll(kcache, w - 1, 0))
    vo_ref[...] = jnp.where(row == w - 1, vnew, pltpu.roll(vcache, w - 1, 0))


def _sample_mixers(proj, state_conv, state_hgrn, cache_k, cache_v, conv_w,
                   lb_param, gain, sinks, layer):
    nb = proj.shape[0]
    w = cache_k.shape[2]
    per_b = lambda *shape: pl.BlockSpec((None,) + shape,
                                        lambda b: (b,) + (0,) * len(shape))
    per_lb = lambda *shape: pl.BlockSpec((None, None) + shape,
                                         lambda b: (layer, b) + (0,) * len(shape))
    whole = lambda *shape: pl.BlockSpec(shape, lambda b: (0,) * len(shape))
    return pl.pallas_call(
        functools.partial(_sample_mix_kernel, layer=layer),
        out_shape=(jax.ShapeDtypeStruct((nb, 1, D_MODEL), F32),
                   jax.ShapeDtypeStruct((nb, CONV_W - 1, CONV_DIM), F32),
                   jax.ShapeDtypeStruct((nb, HG_HEADS, HG_DK, HG_DK), F32),
                   jax.ShapeDtypeStruct((nb, w, KV_DIM), F32),
                   jax.ShapeDtypeStruct((nb, w, KV_DIM), F32)),
        grid=(nb,),
        in_specs=[per_b(1, IN_DIM), per_lb(CONV_W - 1, CONV_DIM),
                  per_lb(HG_HEADS, HG_DK, HG_DK), per_lb(w, KV_DIM), per_lb(w, KV_DIM),
                  whole(CONV_W, CONV_DIM), whole(DEPTH, HG_DIM), whole(1, HG_DK),
                  whole(1, N_HEADS)],
        out_specs=(per_b(1, D_MODEL), per_b(CONV_W - 1, CONV_DIM),
                   per_b(HG_HEADS, HG_DK, HG_DK), per_b(w, KV_DIM), per_b(w, KV_DIM)),
        compiler_params=_cparams("arbitrary"),
        name="sample_mixers",
    )(proj.reshape(nb, 1, IN_DIM), state_conv, state_hgrn,
      cache_k.reshape(DEPTH, nb, w, KV_DIM), cache_v.reshape(DEPTH, nb, w, KV_DIM),
      conv_w, lb_param, gain, sinks)


def _layer_norm(z, g, b):
    mu = jnp.mean(z, axis=-1, keepdims=True)
    zc = z - mu
    var = jnp.mean(zc * zc, axis=-1, keepdims=True)
    return zc * lax.rsqrt(var + LN_EPS) * g + b


def _route(logits, bias):
    lg = logits + bias
    rowv = lambda r: lg[r:r + 1]
    best, gidx = rowv(0), jnp.zeros_like(rowv(0), dtype=I32)
    for r in range(1, N_GROUPS):
        upd = rowv(r) > best
        best = jnp.where(upd, rowv(r), best)
        gidx = jnp.where(upd, r, gidx)
    gden = sum(jnp.exp(rowv(r) - best) for r in range(N_GROUPS))
    gprob = 1.0 / gden
    ev = []
    for j in range(EXPERTS_PER_GROUP):
        val = rowv(N_GROUPS + j)
        for grp in range(1, N_GROUPS):
            val = jnp.where(gidx == grp, rowv(N_GROUPS + grp * EXPERTS_PER_GROUP + j), val)
        ev.append(val)
    v1, j1 = ev[0], jnp.zeros_like(gidx)
    for j in range(1, EXPERTS_PER_GROUP):
        upd = ev[j] > v1
        v1 = jnp.where(upd, ev[j], v1)
        j1 = jnp.where(upd, j, j1)
    v2, j2 = jnp.full_like(v1, -jnp.inf), jnp.zeros_like(gidx)
    for j in range(EXPERTS_PER_GROUP):
        upd = (j1 != j) & (ev[j] > v2)
        v2 = jnp.where(upd, ev[j], v2)
        j2 = jnp.where(upd, j, j2)
    e2 = jnp.exp(v2 - v1)
    w1 = gprob / (1.0 + e2)
    w2 = gprob * e2 / (1.0 + e2)
    base = gidx * EXPERTS_PER_GROUP
    return (jnp.concatenate([base + j1, base + j2], axis=0),
            jnp.concatenate([w1, w2], axis=0))


def _outproj_kernel(*refs, n_y, precise, n_real):
    y_refs = refs[:n_y]
    x_ref, w_ref, g_ref, b_ref, rw_ref, rb_ref = refs[n_y:n_y + 6]
    x1t_ref, eid_ref, ewt_ref, acc_ref = refs[-4:]
    i = pl.program_id(0)
    j = pl.program_id(1)
    n_j = pl.num_programs(1)
    tn = w_ref.shape[1]
    mm = _dot3 if precise else _dot

    @pl.when(i < n_real)
    def _():
        off = 0
        part = None
        for y_ref in y_refs:
            kk = y_ref.shape[1]
            t = mm(y_ref[...], w_ref[off:off + kk, :])
            part = t if part is None else part + t
            off += kk
        acc_ref[:, pl.ds(pl.multiple_of(j * tn, tn), tn)] = part

    @pl.when((i >= n_real) & (j == n_j - 1))
    def _():
        x1t_ref[...] = jnp.zeros_like(x1t_ref)

    @pl.when((i < n_real) & (j == n_j - 1))
    def _():
        z = ALPHA * x_ref[...] + acc_ref[...]
        x1 = _layer_norm(z, g_ref[...], b_ref[...])
        _store_token_major(x1t_ref, x1, TOKEN_PITCH)
        _zero_token_pad(x1t_ref, x1.shape[0], TOKEN_PITCH)
        hi, lo = _split(x1)
        rw = rw_ref[...]
        rwh, rwl = _split(rw)
        logits = _dot_nt(rwh, hi) + (_dot_nt(rwl, hi) + _dot_nt(rwh, lo))
        eid, ewt = _route(logits, rb_ref[...])
        t = eid.shape[1]
        eid_ref[...] = jnp.concatenate([eid, jnp.zeros((SUBLANES - TOP_K, t), I32)], axis=0)
        ewt_ref[...] = jnp.concatenate([ewt, jnp.zeros((SUBLANES - TOP_K, t), F32)], axis=0)


def _outproj_ln_route(ys, x, w_out, layer, ln_g, ln_b, rw_t, rb, *, precise,
                      x1_rows, x1_row0=0, x1_buf=None, tm=512, tn=1024):
    m, d = x.shape
    tm = min(tm, m)
    assert x1_row0 % tm == 0
    n_y = len(ys)
    n_real = m // tm
    n_j = d // tn
    n_i = n_real if x1_buf is not None else pl.cdiv(x1_rows, tm)
    real = lambda i: jnp.minimum(i, n_real - 1)
    y_specs = [pl.BlockSpec((tm, y.shape[1]), lambda i, j: (real(i), 0)) for y in ys]
    vec = lambda: pl.BlockSpec((1, d), lambda i, j: (0, 0))
    operands = list(ys) + [x, w_out, ln_g, ln_b, rw_t, rb]
    in_specs = y_specs + [pl.BlockSpec((tm, d), lambda i, j: (real(i), 0)),
                          pl.BlockSpec((None, d, tn),
                                       lambda i, j: (layer, 0, jnp.where(i < n_real, j, n_j - 1))),
                          vec(), vec(),
                          pl.BlockSpec((ROUTE_ROWS, d), lambda i, j: (0, 0)),
                          pl.BlockSpec((ROUTE_ROWS, 1), lambda i, j: (0, 0))]
    aliases = {}
    if x1_buf is not None:
        aliases = {len(operands): 0}
        operands.append(x1_buf)
        in_specs.append(pl.BlockSpec(memory_space=pl.ANY))
    return pl.pallas_call(
        functools.partial(_outproj_kernel, n_y=n_y, precise=precise, n_real=n_real),
        out_shape=(jax.ShapeDtypeStruct((x1_rows * TOKEN_PITCH, LANES), F32),
                   jax.ShapeDtypeStruct((SUBLANES, m), I32),
                   jax.ShapeDtypeStruct((SUBLANES, m), F32)),
        grid=(n_i, n_j),
        in_specs=in_specs,
        out_specs=(pl.BlockSpec((tm * TOKEN_PITCH, LANES), lambda i, j: (x1_row0 // tm + i, 0)),
                   pl.BlockSpec((SUBLANES, tm), lambda i, j: (0, real(i))),
                   pl.BlockSpec((SUBLANES, tm), lambda i, j: (0, real(i)))),
        scratch_shapes=[pltpu.VMEM((tm, d), F32)],
        input_output_aliases=aliases,
        compiler_params=_cparams("arbitrary", "arbitrary"),
        name="outproj_ln_route",
    )(*operands)


def _slot_base(k, is_sample, n_prompt, n_sample):
    return k * n_prompt + is_sample * (TOP_K * n_prompt + k * (n_sample - n_prompt))


def _token_copy(src_ref, src_row, dst_ref, dst_row, sem):
    return pltpu.make_async_copy(src_ref.at[pl.ds(src_row, TOKEN_ROWS)],
                                 dst_ref.at[pl.ds(dst_row, TOKEN_ROWS)], sem)


def _ffn_kernel(pos_ref, vt_ref, ve_ref, lo_ref, hi_ref, nact_ref,
                x_hbm, wg_ref, wu_ref, wd_ref, y_hbm,
                src_ref, dst_ref, xbuf, ybuf, wgb, wub, wdb, semx, semy,
                *, n_prompt, n_sample, tile):
    v = pl.program_id(0)
    nact = nact_ref[0]
    n_tok = n_prompt + n_sample
    rows = TOP_K * n_tok
    n_tab = src_ref.shape[0]
    dump_row0 = rows * TOKEN_ROWS
    col = FFN_PIECE_COLS
    d, de = wgb.shape

    def gather_one(t0, b, i, prio):
        _token_copy(x_hbm, src_ref[t0 + i], xbuf.at[b], i * TOKEN_PITCH,
                    semx.at[b]).start(priority=prio)

    def scatter_one(t0, lo, hi, live, b, i, prio):
        p = t0 + i
        mine = (p >= lo) & (p < hi) & live
        dst = jnp.where(mine, dst_ref[p], dump_row0 + i * TOKEN_ROWS)
        _token_copy(ybuf.at[b], i * TOKEN_PITCH, y_hbm, dst, semy).start(priority=prio)

    def looped(one):
        def body(grp, c):
            for u in range(DMA_UNROLL):
                one(grp * DMA_UNROLL + u, u % 2)
            return c

        lax.fori_loop(0, tile // DMA_UNROLL, body, 0)

    tile_rows = pl.ds(0, tile * TOKEN_ROWS)

    def wait_gather(b):
        pltpu.make_async_copy(x_hbm.at[tile_rows], xbuf.at[b].at[tile_rows], semx.at[b]).wait()

    def wait_scatter():
        pltpu.make_async_copy(ybuf.at[0].at[tile_rows], y_hbm.at[tile_rows], semy).wait()

    @pl.when(v == 0)
    def _():
        dump = pltpu.make_async_copy(
            x_hbm.at[tile_rows], y_hbm.at[pl.ds(dump_row0, tile * TOKEN_ROWS)], semy)
        dump.start()

        def invert_segment(k, smp, trips):
            s0 = k * n_tok + smp * n_prompt
            tok0 = smp * n_prompt
            slot0 = _slot_base(k, smp, n_prompt, n_sample)

            def body(grp, c):
                j0 = grp * INVERT_UNROLL
                src0 = (tok0 + j0) * TOKEN_PITCH
                dst0 = (slot0 + j0) * TOKEN_ROWS
                for u in range(INVERT_UNROLL):
                    p = pos_ref[s0 + j0 + u]
                    src_ref[p] = src0 + u * TOKEN_PITCH
                    dst_ref[p] = dst0 + u * TOKEN_ROWS
                return c

            lax.fori_loop(0, trips, body, 0)

        for k in range(TOP_K):
            invert_segment(k, 0, nact_ref[1])
            invert_segment(k, 1, nact_ref[2])
        for p in range(rows, n_tab):
            src_ref[p] = 0
            dst_ref[p] = dump_row0
        ybuf[...] = jnp.zeros_like(ybuf)
        dump.wait()
        t0 = vt_ref[0] * tile
        looped(lambda i, prio: gather_one(t0, 0, i, prio))

    @pl.when(v < nact)
    def _():
        b = lax.rem(v, 2)

        @pl.when((v == 0) | (ve_ref[v] != ve_ref[jnp.maximum(v - 1, 0)]))
        def _():
            wgb[...] = wg_ref[...].astype(BF16)
            wub[...] = wu_ref[...].astype(BF16)
            wdb[...] = wd_ref[...].astype(BF16)

        wait_gather(b)

        @pl.when(v >= 1)
        def _():
            wait_scatter()

        t0n = vt_ref[jnp.minimum(v + 1, nact - 1)] * tile
        vp = jnp.maximum(v - 1, 0)
        t0p, lop, hip, live = vt_ref[vp] * tile, lo_ref[vp], hi_ref[vp], v >= 1
        pending = iter(range(tile))

        def issue(count):
            for _ in range(count):
                i = next(pending, None)
                if i is not None:
                    gather_one(t0n, 1 - b, i, 0)
                    scatter_one(t0p, lop, hip, live, 1 - b, i, 1)

        n_pieces = 2 * (de // col) + d // col
        per_piece = pl.cdiv(tile, n_pieces)
        xb = _load_token_major(xbuf.at[b], tile, TOKEN_PITCH).astype(BF16)
        hs = []
        for c in range(de // col):
            g = _dot(xb, wgb[:, c * col:(c + 1) * col])
            issue(per_piece)
            u = _dot(xb, wub[:, c * col:(c + 1) * col])
            issue(per_piece)
            hs.append((_silu(g) * u).astype(BF16))
        hb = jnp.concatenate(hs, axis=1)
        yb = ybuf.at[b]
        for c in range(d // col):
            y = _dot(hb, wdb[:, c * col:(c + 1) * col])
            for r in range(col // LANES):
                yb[pl.ds(c * (col // LANES) + r, tile, stride=TOKEN_PITCH), :] = (
                    y[:, r * LANES:(r + 1) * LANES])
            issue(per_piece)
        issue(tile)

        @pl.when(v == nact - 1)
        def _():
            t0, lo, hi = vt_ref[v] * tile, lo_ref[v], hi_ref[v]
            wait_scatter()
            looped(lambda i, prio: scatter_one(t0, lo, hi, True, b, i, prio))
            wait_scatter()
            wait_gather(1 - b)


def _expert_ffn(pos, plan, x1t, w_gate, w_up, w_down, layer, *, n_prompt, n_sample,
                tile=EXPERT_TILE):
    d, de = w_gate.shape[2], w_gate.shape[3]
    n_visits = plan[0].shape[0]
    rows = TOP_K * (n_prompt + n_sample)
    assert n_prompt + n_sample >= tile
    n_tab = pl.cdiv(rows, tile) * tile
    wspec = lambda r, c: pl.BlockSpec(
        (None, None, r, c), lambda v, pos, vt, ve, lo, hi, na: (layer, ve[v], 0, 0))
    any_spec = pl.BlockSpec(memory_space=pl.ANY)
    return pl.pallas_call(
        functools.partial(_ffn_kernel, n_prompt=n_prompt, n_sample=n_sample, tile=tile),
        out_shape=jax.ShapeDtypeStruct(((rows + tile) * TOKEN_ROWS, LANES), F32),
        grid_spec=pltpu.PrefetchScalarGridSpec(
            num_scalar_prefetch=6,
            grid=(n_visits,),
            in_specs=[any_spec, wspec(d, de), wspec(d, de), wspec(de, d)],
            out_specs=any_spec,
            scratch_shapes=[pltpu.SMEM((n_tab,), I32),
                            pltpu.SMEM((n_tab,), I32),
                            pltpu.VMEM((2, tile * TOKEN_PITCH, LANES), F32),
                            pltpu.VMEM((2, tile * TOKEN_PITCH, LANES), F32),
                            pltpu.VMEM((d, de), BF16),
                            pltpu.VMEM((d, de), BF16),
                            pltpu.VMEM((de, d), BF16),
                            pltpu.SemaphoreType.DMA((2,)),
                            pltpu.SemaphoreType.DMA(())]),
        compiler_params=_cparams("arbitrary"),
        name="moe_ffn",
    )(pos, *plan, x1t, w_gate, w_up, w_down)


def _combine_kernel(x_ref, y0_ref, y1_ref, wt_ref, g_ref, b_ref, o_ref, ob_ref):
    tile = o_ref.shape[0]
    wt = wt_ref[...]
    ffn = (wt[:, 0:1] * _load_token_major(y0_ref, tile, TOKEN_ROWS)
           + wt[:, 1:2] * _load_token_major(y1_ref, tile, TOKEN_ROWS))
    x1 = _load_token_major(x_ref, tile, TOKEN_PITCH)
    out = _layer_norm(ALPHA * x1 + ffn, g_ref[...], b_ref[...])
    o_ref[...] = out
    ob_ref[...] = out.astype(BF16)


def _combine_ln(x1t, y2, wt, ln_g, ln_b, *, row0, m, slots, tile=COMBINE_TILE):
    d = ln_g.shape[1]
    tile = min(tile, m)
    assert row0 % tile == 0 and slots[0] % tile == 0 and slots[1] % tile == 0
    src = lambda base, pitch: pl.BlockSpec((tile * pitch, LANES),
                                           lambda i: (base // tile + i, 0))
    vec = pl.BlockSpec((1, d), lambda i: (0, 0))
    out = pl.BlockSpec((tile, d), lambda i: (i, 0))
    return pl.pallas_call(
        _combine_kernel,
        out_shape=(jax.ShapeDtypeStruct((m, d), F32),
                   jax.ShapeDtypeStruct((m, d), BF16)),
        grid=(m // tile,),
        in_specs=[src(row0, TOKEN_PITCH), src(slots[0], TOKEN_ROWS), src(slots[1], TOKEN_ROWS),
                  pl.BlockSpec((tile, TOP_K), lambda i: (i, 0)), vec, vec],
        out_specs=(out, out),
        compiler_params=_cparams("arbitrary"),
        name="moe_combine_ln",
    )(x1t, y2, y2, wt, ln_g, ln_b)


def _route_plan(eid, tile, n_prompt):
    n_tok = eid.shape[1]
    flat = eid.reshape(-1)
    onehot = (flat[:, None] == jnp.arange(N_EXPERTS, dtype=I32)[None, :]).astype(I32)
    blk = 256
    n_blk = pl.cdiv(TOP_K * n_tok, blk)
    oh = jnp.pad(onehot.astype(F32), ((0, n_blk * blk - TOP_K * n_tok), (0, 0)))
    oh = oh.reshape(n_blk, blk, N_EXPERTS)
    tril = (jnp.arange(blk)[:, None] >= jnp.arange(blk)[None, :]).astype(F32)
    within = jnp.einsum("ij,bjk->bik", tril, oh)
    blk_tot = within[:, -1, :]
    blk_off = jnp.cumsum(blk_tot, axis=0) - blk_tot
    csum = (within + blk_off[:, None, :]).reshape(n_blk * blk, N_EXPERTS)[:TOP_K * n_tok]
    csum = csum.astype(I32)
    counts = csum[-1]
    rank = jnp.sum(csum * onehot, axis=1) - 1
    ends = jnp.cumsum(counts)
    offs = ends - counts
    pos = (offs[flat] + rank).astype(I32)
    first_tile = offs // tile
    last_tile = (ends - 1) // tile
    nvis = jnp.where(counts > 0, last_tile - first_tile + 1, 0)
    vend = jnp.cumsum(nvis)
    vbase = vend - nvis
    nact = vend[-1]
    n_visits = pl.cdiv(TOP_K * n_tok, tile) + N_EXPERTS - 1
    v = jnp.minimum(jnp.arange(n_visits, dtype=I32), nact - 1)
    e = jnp.sum((vend[None, :] <= v[:, None]).astype(I32), axis=1)
    t = first_tile[e] + (v - vbase[e])
    assert n_prompt % INVERT_UNROLL == 0 and (n_tok - n_prompt) % INVERT_UNROLL == 0
    counts_smem = jnp.stack([nact.astype(I32),
                             jnp.asarray(n_prompt // INVERT_UNROLL, I32),
                             jnp.asarray((n_tok - n_prompt) // INVERT_UNROLL, I32)])
    plan = (t.astype(I32), e.astype(I32), offs[e].astype(I32), ends[e].astype(I32), counts_smem)
    return pos, plan


def kernel(x_prompt, x_sample, state_conv, state_hgrn, cache_k_win, cache_v_win, w_in, w_out, conv_w, hg_lb_param, hg_gain, attn_sinks, ln1_g, ln1_b, w_group, b_group, w_router, b_router, w_gate, w_up, w_down, ln2_g, ln2_b):
    bsz, seq, d = x_prompt.shape
    nb = x_sample.shape[0]
    n_prompt = bsz * seq
    n_tok = n_prompt + nb
    w_buf = cache_k_win.shape[2]

    xp = x_prompt.reshape(n_prompt, d)
    xp_mm = xp
    xs = x_sample.reshape(nb, d)
    pad_rows = ROUTE_ROWS - N_GROUPS - N_EXPERTS
    w_out_b = w_out.astype(BF16)
    outs = {k: [] for k in ("cp", "cs", "hp", "hs", "kp", "ks", "vp", "vs")}
    for l in range(DEPTH):
        gain = hg_gain[l].reshape(1, HG_DK)
        sinks = attn_sinks[l].reshape(1, N_HEADS)
        rw_t = jnp.concatenate([w_group[l].T, w_router[l].T, jnp.zeros((pad_rows, d), F32)], axis=0)
        rb = jnp.concatenate([b_group[l], b_router[l], jnp.zeros((pad_rows,), F32)]).reshape(ROUTE_ROWS, 1)
        g1, b1 = ln1_g[l].reshape(1, d), ln1_b[l].reshape(1, d)
        g2, b2 = ln2_g[l].reshape(1, d), ln2_b[l].reshape(1, d)

        if xp_mm.dtype == BF16:
            proj = _inproj(xp_mm, w_in, l, tm=2048, tn=512)
        else:
            proj = _inproj(xp_mm, w_in, l)
        ya, ctail = _conv_prompt(proj, conv_w[l], bsz, seq)
        yb, hstate = _hgrn_prompt(proj, hg_lb_param, gain, l, bsz, seq)
        yc = _attn_prompt(proj, sinks, bsz, seq)
        x1t, eid_p, ewt_p = _outproj_ln_route(
            [ya, yb, yc], xp, w_out_b, l, g1, b1, rw_t, rb, precise=False,
            x1_rows=n_tok, tn=d)
        outs["cp"].append(ctail[:, SUBLANES - (CONV_W - 1):])
        outs["hp"].append(hstate)
        kv = proj.reshape(bsz, seq, IN_DIM)[:, seq - w_buf:]
        outs["kp"].append(kv[:, :, OFF_AK:OFF_AK + KV_DIM].reshape(bsz, w_buf, N_KV_HEADS, HEAD_DIM))
        outs["vp"].append(kv[:, :, OFF_AV:OFF_AV + KV_DIM].reshape(bsz, w_buf, N_KV_HEADS, HEAD_DIM))

        proj_s = _inproj3(xs, w_in, l)
        ysm, cst, hst, kst, vst = _sample_mixers(
            proj_s, state_conv, state_hgrn, cache_k_win, cache_v_win,
            conv_w[l], hg_lb_param, gain, sinks, l)
        x1t, eid_s, ewt_s = _outproj_ln_route(
            [ysm.reshape(nb, d)], xs, w_out, l, g1, b1, rw_t, rb, precise=True,
            x1_rows=n_tok, x1_row0=n_prompt, x1_buf=x1t)
        outs["cs"].append(cst)
        outs["hs"].append(hst)
        outs["ks"].append(kst.reshape(nb, w_buf, N_KV_HEADS, HEAD_DIM))
        outs["vs"].append(vst.reshape(nb, w_buf, N_KV_HEADS, HEAD_DIM))

        eid = jnp.concatenate([eid_p[:TOP_K], eid_s[:TOP_K]], axis=1)
        pos, plan = _route_plan(eid, EXPERT_TILE, n_prompt)
        y2 = _expert_ffn(pos, plan, x1t, w_gate, w_up, w_down, l,
                         n_prompt=n_prompt, n_sample=nb)
        slots = lambda smp: [_slot_base(k, smp, n_prompt, nb) for k in range(TOP_K)]
        xp, xp_mm = _combine_ln(x1t, y2, ewt_p[:TOP_K].T, g2, b2,
                                row0=0, m=n_prompt, slots=slots(0))
        xs, _ = _combine_ln(x1t, y2, ewt_s[:TOP_K].T, g2, b2,
                            row0=n_prompt, m=nb, slots=slots(1))

    st = lambda k: jnp.stack(outs[k])
    return (xp.reshape(bsz, seq, d), xs.reshape(nb, 1, d), st("cp"), st("cs"),
            st("hp"), st("hs"), st("kp"), st("ks"), st("vp"), st("vs"))
```

```python
import functools

import jax
import jax.numpy as jnp
import numpy as np
from jax import lax
from jax.experimental import pallas as pl
from jax.experimental.pallas import tpu as pltpu

F32 = jnp.float32
BF16 = jnp.bfloat16
I32 = jnp.int32

D_MODEL = 2048
DEPTH = 2
CONV_DIM = 512
CONV_W = 3
HG_DIM = 512
HG_HEADS = 4
HG_DK = 128
HG_CHUNK = 16
HEAD_DIM = 64
ATTN_DIM = 1024
N_HEADS = 16
N_KV_HEADS = 4
Q_PER_KV = 4
KV_DIM = 256
WINDOW = 128
N_GROUPS = 4
EXPERTS_PER_GROUP = 4
N_EXPERTS = 16
TOP_K = 2
D_EXPERT = 512
ALPHA = (2 * DEPTH) ** 0.25
LOG2E = 1.4426950408889634
LN_EPS = 1e-5
RMS_EPS = 1e-6
IN_DIM = 5120
OFF_CB, OFF_CC, OFF_CH = 0, 512, 1024
OFF_HQ, OFF_HF, OFF_HI, OFF_HG = 1536, 2048, 2560, 3072
OFF_AQ, OFF_AK, OFF_AV = 3584, 4608, 4864

VMEM_LIMIT_BYTES = 56 * 1024 * 1024
LANES = 128
SUBLANES = 8

EXPERT_TILE = 256
FFN_PIECE_COLS = 256
COMBINE_TILE = 512
INVERT_UNROLL = 8
DMA_UNROLL = 8
SAMPLE_ROWS = 4
ROUTE_ROWS = 32


TOKEN_ROWS = D_MODEL // LANES
TOKEN_PITCH = 20


def _store_token_major(ref, x, pitch):
    n = x.shape[0]
    for c in range(TOKEN_ROWS):
        ref[pl.ds(c, n, stride=pitch), :] = x[:, c * LANES:(c + 1) * LANES]


def _zero_token_pad(ref, n, pitch):
    for c in range(TOKEN_ROWS, pitch):
        ref[pl.ds(c, n, stride=pitch), :] = jnp.zeros((n, LANES), ref.dtype)


def _load_token_major(ref, n, pitch):
    return jnp.concatenate(
        [ref[pl.ds(c, n, stride=pitch), :] for c in range(TOKEN_ROWS)], axis=1)


def _cparams(*sem):
    return pltpu.CompilerParams(dimension_semantics=sem,
                                vmem_limit_bytes=VMEM_LIMIT_BYTES)


def _dot(a, b):
    return jnp.dot(a, b, preferred_element_type=F32)


def _dot_nt(a, b):
    return lax.dot_general(a, b, (((1,), (1,)), ((), ())),
                           preferred_element_type=F32)


def _split(x):
    hi = x.astype(BF16)
    lo = (x - hi.astype(F32)).astype(BF16)
    return hi, lo


def _dot3(a, b):
    ah, al = _split(a)
    bh, bl = _split(b)
    return _dot(ah, bh) + (_dot(ah, bl) + _dot(al, bh))


def _dot3_nt(a, b):
    ah, al = _split(a)
    bh, bl = _split(b)
    return _dot_nt(ah, bh) + (_dot_nt(ah, bl) + _dot_nt(al, bh))


def _sigmoid(x):
    return 1.0 / (1.0 + jnp.exp(-x))


def _silu(x):
    return x * _sigmoid(x)


def _col_from_row(row):
    n = row.shape[1]
    eye = (lax.broadcasted_iota(I32, (n, n), 0)
           == lax.broadcasted_iota(I32, (n, n), 1))
    return jnp.sum(jnp.where(eye, row, 0.0), axis=1, keepdims=True)


def _hg_lower_bound(lbp, layer):
    m = jnp.max(lbp, axis=0, keepdims=True)
    e = jnp.exp(lbp - m)
    soft = e / jnp.sum(e, axis=0, keepdims=True)
    acc = soft[0:1]
    for i in range(1, layer + 1):
        acc = acc + soft[i:i + 1]
    return acc - soft[0:1]


def _alibi_slope(head):
    return float(2.0 ** (-8.0 * (head + 1) / N_HEADS))


def _inproj_kernel(x_ref, w_ref, o_ref, xb_ref):
    @pl.when(pl.program_id(1) == 0)
    def _():
        xb_ref[...] = x_ref[...].astype(BF16)

    o_ref[...] = _dot(xb_ref[...], w_ref[...].astype(BF16))


def _inproj(x, w, layer, *, tm=1024, tn=1024):
    m, k = x.shape
    n = w.shape[2]
    tm = min(tm, m)
    x_mode = dict(pipeline_mode=pl.Buffered(1)) if x.dtype == F32 else {}
    return pl.pallas_call(
        _inproj_kernel,
        out_shape=jax.ShapeDtypeStruct((m, n), F32),
        grid=(m // tm, n // tn),
        in_specs=[pl.BlockSpec((tm, k), lambda i, j: (i, 0), **x_mode),
                  pl.BlockSpec((None, k, tn), lambda i, j: (layer, 0, j))],
        out_specs=pl.BlockSpec((tm, tn), lambda i, j: (i, j)),
        scratch_shapes=[pltpu.VMEM((tm, k), BF16)],
        compiler_params=_cparams("arbitrary", "arbitrary"),
        name="inproj",
    )(x, w)


def _inproj3_kernel(x_ref, w_ref, o_ref):
    o_ref[...] = _dot3(x_ref[...], w_ref[...])


def _inproj3(x, w, layer, *, tn=512):
    m, k = x.shape
    n = w.shape[2]
    return pl.pallas_call(
        _inproj3_kernel,
        out_shape=jax.ShapeDtypeStruct((m, n), F32),
        grid=(n // tn,),
        in_specs=[pl.BlockSpec((m, k), lambda j: (0, 0)),
                  pl.BlockSpec((None, k, tn), lambda j: (layer, 0, j))],
        out_specs=pl.BlockSpec((m, tn), lambda j: (0, j)),
        compiler_params=_cparams("arbitrary"),
        name="inproj_sample",
    )(x, w)


def _conv_kernel(cb_ref, cc_ref, ch_ref, w_ref, y_ref, tail_ref, carry_ref):
    i = pl.program_id(1)

    @pl.when(i == 0)
    def _():
        carry_ref[...] = jnp.zeros_like(carry_ref)

    u = cc_ref[...] * ch_ref[...]
    tl = u.shape[0]
    row = lax.broadcasted_iota(I32, u.shape, 0)
    prev1 = carry_ref[SUBLANES - 1:SUBLANES, :]
    prev2 = carry_ref[SUBLANES - 2:SUBLANES - 1, :]
    u1 = jnp.where(row == 0, prev1, pltpu.roll(u, 1, 0))
    u2 = jnp.where(row == 0, prev2, jnp.where(row == 1, prev1, pltpu.roll(u, 2, 0)))
    w = w_ref[...]
    y = w[0:1] * u2 + w[1:2] * u1 + w[2:3] * u
    y_ref[...] = (cb_ref[...] * y).astype(y_ref.dtype)
    tail = u[tl - SUBLANES:tl, :]
    carry_ref[...] = tail
    tail_ref[...] = tail


def _conv_prompt(proj, conv_w, bsz, seq, *, tl=512):
    tl = min(tl, seq)
    nt = seq // tl
    cblk = lambda c: pl.BlockSpec((tl, CONV_DIM), lambda b, i, c=c: (b * nt + i, c))
    return pl.pallas_call(
        _conv_kernel,
        out_shape=(jax.ShapeDtypeStruct((bsz * seq, CONV_DIM), BF16),
                   jax.ShapeDtypeStruct((bsz, SUBLANES, CONV_DIM), F32)),
        grid=(bsz, nt),
        in_specs=[cblk(OFF_CB // CONV_DIM), cblk(OFF_CC // CONV_DIM),
                  cblk(OFF_CH // CONV_DIM),
                  pl.BlockSpec((CONV_W, CONV_DIM), lambda b, i: (0, 0))],
        out_specs=(pl.BlockSpec((tl, CONV_DIM), lambda b, i: (b * nt + i, 0)),
                   pl.BlockSpec((None, SUBLANES, CONV_DIM), lambda b, i: (b, 0, 0))),
        scratch_shapes=[pltpu.VMEM((SUBLANES, CONV_DIM), F32)],
        compiler_params=_cparams("arbitrary", "arbitrary"),
        name="conv_prompt",
    )(proj, proj, proj, conv_w)


def _seg_cumsum(g, seg):
    row = lax.broadcasted_iota(I32, g.shape, 0) % seg
    s = 1
    while s < seg:
        g = g + jnp.where(row >= s, pltpu.roll(g, s, 0), 0.0)
        s *= 2
    return g


def _hgrn_kernel(hq_ref, hf_ref, hi_ref, hg_ref, lbp_ref, gain_ref,
                 y_ref, st_ref, cum_ref, k_ref, q_ref, o_ref, s_ref, *, layer):
    i = pl.program_id(1)
    nt = pl.num_programs(1)
    tb = hq_ref.shape[0]
    c = HG_CHUNK

    @pl.when(i == 0)
    def _():
        s_ref[...] = jnp.zeros_like(s_ref)

    lb = _hg_lower_bound(lbp_ref[...], layer)
    f = lb + (1.0 - lb) * _sigmoid(hf_ref[...])
    cum_ref[...] = _seg_cumsum(jnp.log(f) * LOG2E, c)
    k_ref[...] = 1.0 - f
    q_ref[...] = hq_ref[...] * (HG_DK ** -0.5)

    ones = jnp.ones((HG_DK, HG_DK), BF16)
    hc = c // 2
    trow = lax.broadcasted_iota(I32, (hc, HG_DK), 0)

    def chunk(ci, carry):
        r0 = pl.multiple_of(ci * c, c)
        for h in range(HG_HEADS):
            cols = slice(h * HG_DK, (h + 1) * HG_DK)
            cum = cum_ref[pl.ds(r0, c), cols]
            kc = k_ref[pl.ds(r0, c), cols]
            qc = q_ref[pl.ds(r0, c), cols]
            vc = hi_ref[pl.ds(r0, c), cols]
            parts = []
            for s in range(c):
                cs, qk_hi = cum[s:s + 1], qc[hc:] * kc[s:s + 1]
                if s < hc:
                    dlo = jnp.exp2(jnp.where(trow >= s, cum[:hc] - cs, -jnp.inf))
                    parts.append((qc[:hc] * kc[s:s + 1] * dlo).astype(BF16))
                    dhi = jnp.exp2(cum[hc:] - cs)
                else:
                    dhi = jnp.exp2(jnp.where(trow >= s - hc, cum[hc:] - cs, -jnp.inf))
                parts.append((qk_hi * dhi).astype(BF16))
            sc = _dot(jnp.concatenate(parts, axis=0), ones)
            o_lo = jnp.zeros((hc, HG_DK), F32)
            o_hi = jnp.zeros((hc, HG_DK), F32)
            r = 0
            for s in range(c):
                if s < hc:
                    o_lo = o_lo + sc[r:r + hc] * vc[s:s + 1]
                    r += hc
                o_hi = o_hi + sc[r:r + hc] * vc[s:s + 1]
                r += hc
            o = jnp.concatenate([o_lo, o_hi], axis=0)
            st = s_ref[h]
            last = cum[c - 1:c]
            o = o + _dot_nt((qc * jnp.exp2(cum)).astype(BF16), st.astype(BF16))
            kte = (kc * jnp.exp2(last - cum)).astype(BF16)
            du = lax.dot_general(vc.astype(BF16), kte, (((0,), (0,)), ((), ())),
                                 preferred_element_type=F32)
            s_ref[h] = jnp.exp2(last) * st + du
            o_ref[pl.ds(r0, c), cols] = o
        return carry

    lax.fori_loop(0, tb // c, chunk, 0, unroll=4)

    gain = gain_ref[...]
    outs = []
    for h in range(HG_HEADS):
        cols = slice(h * HG_DK, (h + 1) * HG_DK)
        o = o_ref[:, cols]
        o = o * lax.rsqrt(jnp.mean(o * o, axis=-1, keepdims=True) + RMS_EPS) * gain
        outs.append(o * _silu(hg_ref[:, cols]))
    y_ref[...] = jnp.concatenate(outs, axis=-1).astype(y_ref.dtype)

    @pl.when(i == nt - 1)
    def _():
        for h in range(HG_HEADS):
            st_ref[h] = s_ref[h].T


def _hgrn_prompt(proj, lb_param, gain, layer, bsz, seq, *, tb=512):
    tb = min(tb, seq)
    nt = seq // tb
    cblk = lambda c: pl.BlockSpec((tb, HG_DIM), lambda b, i, c=c: (b * nt + i, c))
    return pl.pallas_call(
        functools.partial(_hgrn_kernel, layer=layer),
        out_shape=(jax.ShapeDtypeStruct((bsz * seq, HG_DIM), BF16),
                   jax.ShapeDtypeStruct((bsz, HG_HEADS, HG_DK, HG_DK), F32)),
        grid=(bsz, nt),
        in_specs=[cblk(OFF_HQ // HG_DIM), cblk(OFF_HF // HG_DIM),
                  cblk(OFF_HI // HG_DIM), cblk(OFF_HG // HG_DIM),
                  pl.BlockSpec((DEPTH, HG_DIM), lambda b, i: (0, 0)),
                  pl.BlockSpec((1, HG_DK), lambda b, i: (0, 0))],
        out_specs=(pl.BlockSpec((tb, HG_DIM), lambda b, i: (b * nt + i, 0)),
                   pl.BlockSpec((None, HG_HEADS, HG_DK, HG_DK), lambda b, i: (b, 0, 0, 0))),
        scratch_shapes=[pltpu.VMEM((tb, HG_DIM), F32),
                        pltpu.VMEM((tb, HG_DIM), F32),
                        pltpu.VMEM((tb, HG_DIM), F32),
                        pltpu.VMEM((tb, HG_DIM), F32),
                        pltpu.VMEM((HG_HEADS, HG_DK, HG_DK), F32)],
        compiler_params=_cparams("arbitrary", "arbitrary"),
        name="hgrn_prompt",
    )(proj, proj, proj, proj, lb_param, gain)


def _attn_kernel(qa_ref, qb_ref, kc_ref, vc_ref, kp_ref, vp_ref, sink_ref, y_ref,
                 bias_ref, sinkcol_ref):
    n = pl.program_id(1)
    w = WINDOW
    qi = lax.broadcasted_iota(I32, (Q_PER_KV * w, 2 * w), 0) % w
    kj = lax.broadcasted_iota(I32, (Q_PER_KV * w, 2 * w), 1)

    @pl.when((pl.program_id(0) == 0) & (n == 0))
    def _():
        dist = qi + w - kj
        valid = (dist >= 0) & (dist < w)
        distf = dist.astype(F32)
        for kv in range(N_KV_HEADS):
            grp = lax.broadcasted_iota(I32, (Q_PER_KV * w, 1), 0) // w
            slope = jnp.zeros((Q_PER_KV * w, 1), F32)
            for g in range(Q_PER_KV):
                slope = jnp.where(grp == g, _alibi_slope(kv * Q_PER_KV + g), slope)
            bias = jnp.where(valid, -slope * distf, -jnp.inf)
            bias_ref[1, pl.ds(kv * Q_PER_KV * w, Q_PER_KV * w), :] = bias
            bias_ref[0, pl.ds(kv * Q_PER_KV * w, Q_PER_KV * w), :] = jnp.where(
                kj >= w, bias, -jnp.inf)
        for h in range(N_HEADS):
            sinkcol_ref[pl.ds(h * w, w), :] = jnp.broadcast_to(sink_ref[:, h:h + 1], (w, LANES))

    low = lax.broadcasted_iota(I32, (1, LANES), 1) < HEAD_DIM
    ones = jnp.ones((2 * w, LANES), BF16)
    scores, values = [], []
    for kv in range(N_KV_HEADS):
        q_ref = qa_ref if kv < 2 else qb_ref
        qoff = (kv % 2) * Q_PER_KV * HEAD_DIM
        kt = slice((kv // 2) * LANES, (kv // 2 + 1) * LANES)

        def both_halves(prev_ref, cur_ref):
            t = jnp.concatenate([prev_ref[:, kt], cur_ref[:, kt]], axis=0)
            r = pltpu.roll(t, HEAD_DIM, 1)
            return (jnp.where(low, t, r) if kv % 2 == 0 else jnp.where(low, r, t)).astype(BF16)

        q_parts = []
        for g in range(Q_PER_KV):
            qt = q_ref[:, qoff + (g // 2) * LANES: qoff + (g // 2 + 1) * LANES]
            q_parts.append(jnp.where(low if g % 2 == 0 else ~low, qt, 0.0))
        q = jnp.concatenate(q_parts, axis=0).astype(BF16)
        scores.append(_dot_nt(q, both_halves(kp_ref, kc_ref)))
        values.append(both_halves(vp_ref, vc_ref))
    s = jnp.concatenate(scores, axis=0) * (HEAD_DIM ** -0.5) + bias_ref[jnp.minimum(n, 1)]
    sink = sinkcol_ref[...]
    m = jnp.maximum(jnp.max(s, axis=-1, keepdims=True), sink)
    p = jnp.exp(s - jnp.concatenate([m, m], axis=1)).astype(BF16)
    rows = Q_PER_KV * w
    outs = []
    for kv in range(N_KV_HEADS):
        pk = p[kv * rows:(kv + 1) * rows]
        sl = slice(kv * rows, (kv + 1) * rows)
        denom = _dot(pk, ones) + jnp.exp(sink[sl] - m[sl])
        o = _dot(pk, values[kv]) / denom
        for g in range(0, Q_PER_KV, 2):
            outs.append(jnp.where(low, o[g * w:(g + 1) * w], o[(g + 1) * w:(g + 2) * w]))
    y_ref[...] = jnp.concatenate(outs, axis=-1).astype(y_ref.dtype)


def _attn_prompt(proj, sinks, bsz, seq):
    w = WINDOW
    nb = seq // w
    half = ATTN_DIM // 2
    cur = lambda width, off: pl.BlockSpec(
        (w, width), lambda b, n: (b * nb + n, off // width))
    prev = lambda width, off: pl.BlockSpec(
        (w, width), lambda b, n: (b * nb + jnp.maximum(n - 1, 0), off // width))
    return pl.pallas_call(
        _attn_kernel,
        out_shape=jax.ShapeDtypeStruct((bsz * seq, ATTN_DIM), BF16),
        grid=(bsz, nb),
        in_specs=[cur(half, OFF_AQ), cur(half, OFF_AQ + half),
                  cur(KV_DIM, OFF_AK), cur(KV_DIM, OFF_AV),
                  prev(KV_DIM, OFF_AK), prev(KV_DIM, OFF_AV),
                  pl.BlockSpec((1, N_HEADS), lambda b, n: (0, 0))],
        out_specs=pl.BlockSpec((w, ATTN_DIM), lambda b, n: (b * nb + n, 0)),
        scratch_shapes=[pltpu.VMEM((2, N_HEADS * w, 2 * w), F32),
                        pltpu.VMEM((N_HEADS * w, LANES), F32)],
        compiler_params=_cparams("arbitrary", "arbitrary"),
        name="attn_prompt",
    )(proj, proj, proj, proj, proj, proj, sinks)


def _sample_mix_kernel(p_ref, sc_ref, s0_ref, kc_ref, vc_ref, cw_ref, lbp_ref,
                       gain_ref, sink_ref,
                       y_ref, sco_ref, so_ref, ko_ref, vo_ref, *, layer):
    for r in range(p_ref.shape[0]):
        _sample_mix_one(p_ref.at[r], sc_ref.at[r], s0_ref.at[r], kc_ref.at[r], vc_ref.at[r],
                        cw_ref, lbp_ref, gain_ref, sink_ref,
                        y_ref.at[r], sco_ref.at[r], so_ref.at[r], ko_ref.at[r], vo_ref.at[r],
                        layer)


def _sample_mix_one(p_ref, sc_ref, s0_ref, kc_ref, vc_ref, cw_ref, lbp_ref, gain_ref, sink_ref,
                    y_ref, sco_ref, so_ref, ko_ref, vo_ref, layer):
    p = p_ref[...]
    seg = lambda off, n: p[:, off:off + n]
    u = seg(OFF_CC, CONV_DIM) * seg(OFF_CH, CONV_DIM)
    hist = sc_ref[...]
    cw = cw_ref[...]
    conv = cw[0:1] * hist[0:1] + cw[1:2] * hist[1:2] + cw[2:3] * u
    ya = seg(OFF_CB, CONV_DIM) * conv
    sco_ref[...] = jnp.concatenate([hist[1:2], u], axis=0)
    lb = _hg_lower_bound(lbp_ref[...], layer)
    f = lb + (1.0 - lb) * _sigmoid(seg(OFF_HF, HG_DIM))
    g = jnp.log(f)
    kk = 1.0 - f
    q = seg(OFF_HQ, HG_DIM) * (HG_DK ** -0.5)
    v = seg(OFF_HI, HG_DIM)
    gate = seg(OFF_HG, HG_DIM)
    gain = gain_ref[...]
    yb = []
    for h in range(HG_HEADS):
        cols = slice(h * HG_DK, (h + 1) * HG_DK)
        s0 = s0_ref[h]
        eg = jnp.exp(g[:, cols])
        qe_col = _col_from_row(q[:, cols] * eg)
        o = (jnp.sum(q[:, cols] * kk[:, cols], axis=-1, keepdims=True) * v[:, cols]
             + jnp.sum(qe_col * s0, axis=0, keepdims=True))
        so_ref[h] = _col_from_row(eg) * s0 + _col_from_row(kk[:, cols]) * v[:, cols]
        o = o * lax.rsqrt(jnp.mean(o * o, axis=-1, keepdims=True) + RMS_EPS) * gain
        yb.append(o * _silu(gate[:, cols]))
    w = kc_ref.shape[0]
    kcache = kc_ref[...]
    vcache = vc_ref[...]
    knew = seg(OFF_AK, KV_DIM)
    vnew = seg(OFF_AV, KV_DIM)
    aq = seg(OFF_AQ, ATTN_DIM)
    sinks = sink_ref[...]
    kj = lax.broadcasted_iota(I32, (Q_PER_KV, w), 1)
    dist = w - kj
    valid = dist < WINDOW
    gi = lax.broadcasted_iota(I32, (Q_PER_KV, 1), 0)
    yc = []
    for kv in range(N_KV_HEADS):
        ks = slice(kv * HEAD_DIM, (kv + 1) * HEAD_DIM)
        qh = jnp.concatenate(
            [aq[:, (kv * Q_PER_KV + gq) * HEAD_DIM:(kv * Q_PER_KV + gq + 1) * HEAD_DIM]
             for gq in range(Q_PER_KV)], axis=0)
        slope = jnp.zeros((Q_PER_KV, 1), F32)
        sink = jnp.zeros((Q_PER_KV, 1), F32)
        for gq in range(Q_PER_KV):
            hd = kv * Q_PER_KV + gq
            slope = jnp.where(gi == gq, _alibi_slope(hd), slope)
            sink = jnp.where(gi == gq, sinks[:, hd:hd + 1], sink)
        scale = HEAD_DIM ** -0.5
        sc = _dot3_nt(qh, kcache[:, ks]) * scale - slope * dist.astype(F32)
        sc = jnp.where(valid, sc, -jnp.inf)
        sn = jnp.sum(qh * knew[:, ks], axis=-1, keepdims=True) * scale
        m = jnp.maximum(jnp.maximum(jnp.max(sc, axis=-1, keepdims=True), sn), sink)
        pc = jnp.exp(sc - m)
        pn = jnp.exp(sn - m)
        denom = jnp.sum(pc, axis=-1, keepdims=True) + pn + jnp.exp(sink - m)
        o = (_dot3(pc, vcache[:, ks]) + pn * vnew[:, ks]) / denom
        for gq in range(Q_PER_KV):
            yc.append(o[gq:gq + 1])
    y_ref[...] = jnp.concatenate([ya] + yb + yc, axis=-1)
    row = lax.broadcasted_iota(I32, (w, KV_DIM), 0)
    ko_ref[...] = jnp.where(row == w - 1, knew, pltpu.roll(kcache, w - 1, 0))
    vo_ref[...] = jnp.where(row == w - 1, vnew, pltpu.roll(vcache, w - 1, 0))


def _sample_mixers(proj, state_conv, state_hgrn, cache_k, cache_v, conv_w,
                   lb_param, gain, sinks, layer):
    nb = proj.shape[0]
    w = cache_k.shape[2]
    rows = SAMPLE_ROWS
    assert nb % rows == 0
    per_b = lambda *shape: pl.BlockSpec((rows,) + shape,
                                        lambda b: (b,) + (0,) * len(shape))
    per_lb = lambda *shape: pl.BlockSpec((None, rows) + shape,
                                         lambda b: (layer, b) + (0,) * len(shape))
    whole = lambda *shape: pl.BlockSpec(shape, lambda b: (0,) * len(shape))
    return pl.pallas_call(
        functools.partial(_sample_mix_kernel, layer=layer),
        out_shape=(jax.ShapeDtypeStruct((nb, 1, D_MODEL), F32),
                   jax.ShapeDtypeStruct((nb, CONV_W - 1, CONV_DIM), F32),
                   jax.ShapeDtypeStruct((nb, HG_HEADS, HG_DK, HG_DK), F32),
                   jax.ShapeDtypeStruct((nb, w, KV_DIM), F32),
                   jax.ShapeDtypeStruct((nb, w, KV_DIM), F32)),
        grid=(nb // rows,),
        in_specs=[per_b(1, IN_DIM), per_lb(CONV_W - 1, CONV_DIM),
                  per_lb(HG_HEADS, HG_DK, HG_DK), per_lb(w, KV_DIM), per_lb(w, KV_DIM),
                  whole(CONV_W, CONV_DIM), whole(DEPTH, HG_DIM), whole(1, HG_DK),
                  whole(1, N_HEADS)],
        out_specs=(per_b(1, D_MODEL), per_b(CONV_W - 1, CONV_DIM),
                   per_b(HG_HEADS, HG_DK, HG_DK), per_b(w, KV_DIM), per_b(w, KV_DIM)),
        compiler_params=_cparams("arbitrary"),
        name="sample_mixers",
    )(proj.reshape(nb, 1, IN_DIM), state_conv, state_hgrn,
      cache_k.reshape(DEPTH, nb, w, KV_DIM), cache_v.reshape(DEPTH, nb, w, KV_DIM),
      conv_w, lb_param, gain, sinks)


def _layer_norm(z, g, b):
    mu = jnp.mean(z, axis=-1, keepdims=True)
    zc = z - mu
    var = jnp.mean(zc * zc, axis=-1, keepdims=True)
    return zc * lax.rsqrt(var + LN_EPS) * g + b


def _route(logits, bias):
    lg = logits + bias
    rowv = lambda r: lg[r:r + 1]
    best, gidx = rowv(0), jnp.zeros_like(rowv(0), dtype=I32)
    for r in range(1, N_GROUPS):
        upd = rowv(r) > best
        best = jnp.where(upd, rowv(r), best)
        gidx = jnp.where(upd, r, gidx)
    gden = sum(jnp.exp(rowv(r) - best) for r in range(N_GROUPS))
    gprob = 1.0 / gden
    ev = []
    for j in range(EXPERTS_PER_GROUP):
        val = rowv(N_GROUPS + j)
        for grp in range(1, N_GROUPS):
            val = jnp.where(gidx == grp, rowv(N_GROUPS + grp * EXPERTS_PER_GROUP + j), val)
        ev.append(val)
    v1, j1 = ev[0], jnp.zeros_like(gidx)
    for j in range(1, EXPERTS_PER_GROUP):
        upd = ev[j] > v1
        v1 = jnp.where(upd, ev[j], v1)
        j1 = jnp.where(upd, j, j1)
    v2, j2 = jnp.full_like(v1, -jnp.inf), jnp.zeros_like(gidx)
    for j in range(EXPERTS_PER_GROUP):
        upd = (j1 != j) & (ev[j] > v2)
        v2 = jnp.where(upd, ev[j], v2)
        j2 = jnp.where(upd, j, j2)
    e2 = jnp.exp(v2 - v1)
    w1 = gprob / (1.0 + e2)
    w2 = gprob * e2 / (1.0 + e2)
    base = gidx * EXPERTS_PER_GROUP
    return (jnp.concatenate([base + j1, base + j2], axis=0),
            jnp.concatenate([w1, w2], axis=0))


def _outproj_kernel(*refs, n_y, precise, n_real, n_j):
    y_refs = refs[:n_y]
    x_ref, w_ref, g_ref, b_ref, rw_ref, rb_ref = refs[n_y:n_y + 6]
    x1t_ref, eid_ref, ewt_ref, acc_ref = refs[-4:]
    i = pl.program_id(0)
    j = pl.program_id(1)
    tn = w_ref.shape[1]
    mm = _dot3 if precise else _dot

    def mix():
        y = jnp.concatenate([y_ref[...] for y_ref in y_refs], axis=1)
        return mm(y, w_ref[...])

    if n_j > 1:
        @pl.when(i < n_real)
        def _():
            acc_ref[:, pl.ds(pl.multiple_of(j * tn, tn), tn)] = mix()

    @pl.when((i >= n_real) & (j == n_j - 1))
    def _():
        x1t_ref[...] = jnp.zeros_like(x1t_ref)

    @pl.when((i < n_real) & (j == n_j - 1))
    def _():
        z = ALPHA * x_ref[...] + (acc_ref[...] if n_j > 1 else mix())
        x1 = _layer_norm(z, g_ref[...], b_ref[...])
        _store_token_major(x1t_ref, x1, TOKEN_PITCH)
        _zero_token_pad(x1t_ref, x1.shape[0], TOKEN_PITCH)
        hi, lo = _split(x1)
        rwh, rwl = _split(rw_ref[...])
        both = _dot_nt(jnp.concatenate([rwh, rwl], axis=0), hi)
        logits = both[:ROUTE_ROWS] + (both[ROUTE_ROWS:] + _dot_nt(rwh, lo))
        eid, ewt = _route(logits, rb_ref[...])
        t = eid.shape[1]
        eid_ref[...] = jnp.concatenate([eid, jnp.zeros((SUBLANES - TOP_K, t), I32)], axis=0)
        ewt_ref[...] = jnp.concatenate([ewt, jnp.zeros((SUBLANES - TOP_K, t), F32)], axis=0)


def _outproj_ln_route(ys, x, w_out, layer, ln_g, ln_b, rw_t, rb, *, precise,
                      x1_rows, x1_row0=0, x1_buf=None, tm=512, tn=1024):
    m, d = x.shape
    tm = min(tm, m)
    assert x1_row0 % tm == 0
    n_y = len(ys)
    n_real = m // tm
    n_j = d // tn
    n_i = n_real if x1_buf is not None else pl.cdiv(x1_rows, tm)
    real = lambda i: jnp.minimum(i, n_real - 1)
    y_specs = [pl.BlockSpec((tm, y.shape[1]), lambda i, j: (real(i), 0)) for y in ys]
    vec = lambda: pl.BlockSpec((1, d), lambda i, j: (0, 0))
    operands = list(ys) + [x, w_out, ln_g, ln_b, rw_t, rb]
    in_specs = y_specs + [pl.BlockSpec((tm, d), lambda i, j: (real(i), 0)),
                          pl.BlockSpec((None, d, tn),
                                       lambda i, j: (layer, 0, jnp.where(i < n_real, j, n_j - 1))),
                          vec(), vec(),
                          pl.BlockSpec((ROUTE_ROWS, d), lambda i, j: (0, 0)),
                          pl.BlockSpec((ROUTE_ROWS, 1), lambda i, j: (0, 0))]
    aliases = {}
    if x1_buf is not None:
        aliases = {len(operands): 0}
        operands.append(x1_buf)
        in_specs.append(pl.BlockSpec(memory_space=pl.ANY))
    return pl.pallas_call(
        functools.partial(_outproj_kernel, n_y=n_y, precise=precise, n_real=n_real, n_j=n_j),
        out_shape=(jax.ShapeDtypeStruct((x1_rows * TOKEN_PITCH, LANES), F32),
                   jax.ShapeDtypeStruct((SUBLANES, m), I32),
                   jax.ShapeDtypeStruct((SUBLANES, m), F32)),
        grid=(n_i, n_j),
        in_specs=in_specs,
        out_specs=(pl.BlockSpec((tm * TOKEN_PITCH, LANES), lambda i, j: (x1_row0 // tm + i, 0)),
                   pl.BlockSpec((SUBLANES, tm), lambda i, j: (0, real(i))),
                   pl.BlockSpec((SUBLANES, tm), lambda i, j: (0, real(i)))),
        scratch_shapes=[pltpu.VMEM((tm, d) if n_j > 1 else (SUBLANES, LANES), F32)],
        input_output_aliases=aliases,
        compiler_params=_cparams("arbitrary", "arbitrary"),
        name="outproj_ln_route",
    )(*operands)


def _slot_base(k, is_sample, n_prompt, n_sample):
    return k * n_prompt + is_sample * (TOP_K * n_prompt + k * (n_sample - n_prompt))


def _token_copy(src_ref, src_row, dst_ref, dst_row, sem):
    return pltpu.make_async_copy(src_ref.at[pl.ds(src_row, TOKEN_ROWS)],
                                 dst_ref.at[pl.ds(dst_row, TOKEN_ROWS)], sem)


def _ffn_kernel(pos_ref, vt_ref, ve_ref, lo_ref, hi_ref, nact_ref,
                x_hbm, wg_ref, wu_ref, wd_ref, y_hbm,
                src_ref, dst_ref, xbuf, ybuf, wgb, wub, wdb, semx, semy,
                *, n_prompt, n_sample, tile):
    v = pl.program_id(0)
    nact = nact_ref[0]
    n_tok = n_prompt + n_sample
    rows = TOP_K * n_tok
    n_tab = src_ref.shape[0]
    dump_row0 = rows * TOKEN_ROWS
    col = FFN_PIECE_COLS
    d, de = wgb.shape

    def gather_one(t0, b, i, prio):
        _token_copy(x_hbm, src_ref[t0 + i], xbuf.at[b], i * TOKEN_PITCH,
                    semx.at[b]).start(priority=prio)

    def scatter_one(t0, b, i, prio):
        _token_copy(ybuf.at[b], i * TOKEN_PITCH, y_hbm, dst_ref[t0 + i],
                    semy).start(priority=prio)

    def looped(one):
        def body(grp, c):
            for u in range(DMA_UNROLL):
                one(grp * DMA_UNROLL + u, u % 2)
            return c

        lax.fori_loop(0, tile // DMA_UNROLL, body, 0)

    tile_rows = pl.ds(0, tile * TOKEN_ROWS)

    def wait_gather(b):
        pltpu.make_async_copy(x_hbm.at[tile_rows], xbuf.at[b].at[tile_rows], semx.at[b]).wait()

    def wait_scatter():
        pltpu.make_async_copy(ybuf.at[0].at[tile_rows], y_hbm.at[tile_rows], semy).wait()

    @pl.when(v == 0)
    def _():
        dump = pltpu.make_async_copy(
            x_hbm.at[tile_rows], y_hbm.at[pl.ds(dump_row0, tile * TOKEN_ROWS)], semy)
        dump.start()

        def invert_segment(k, smp, trips):
            s0 = k * n_tok + smp * n_prompt
            tok0 = smp * n_prompt
            slot0 = _slot_base(k, smp, n_prompt, n_sample)

            def body(grp, c):
                j0 = grp * INVERT_UNROLL
                src0 = (tok0 + j0) * TOKEN_PITCH
                dst0 = (slot0 + j0) * TOKEN_ROWS
                for u in range(INVERT_UNROLL):
                    p = pos_ref[s0 + j0 + u]
                    src_ref[p] = src0 + u * TOKEN_PITCH
                    dst_ref[p] = dst0 + u * TOKEN_ROWS
                return c

            lax.fori_loop(0, trips, body, 0)

        for k in range(TOP_K):
            invert_segment(k, 0, nact_ref[1])
            invert_segment(k, 1, nact_ref[2])
        for p in range(rows, n_tab):
            src_ref[p] = 0
            dst_ref[p] = dump_row0 + (p - rows) * TOKEN_ROWS
        ybuf[...] = jnp.zeros_like(ybuf)
        dump.wait()
        looped(lambda i, prio: gather_one(0, 0, i, prio))

    def visit(do_gather, do_scatter):
        t = vt_ref[v]
        b = lax.rem(t, 2)
        first = (v == 0) | (vt_ref[jnp.maximum(v - 1, 0)] != t)
        pending = iter(range(tile))

        def issue(count):
            for _ in range(count):
                i = next(pending, None)
                if i is not None and do_gather:
                    gather_one((t + 1) * tile, 1 - b, i, i % 2)
                if i is not None and do_scatter:
                    scatter_one((t - 1) * tile, 1 - b, i, (i + 1) % 2)

        row = t * tile + lax.broadcasted_iota(I32, (tile, 1), 0)
        keep = first | ((row >= lo_ref[v]) & (row < hi_ref[v]))
        n_pieces = 2 * (de // col) + d // col
        per_piece = pl.cdiv(tile, n_pieces)
        xb = _load_token_major(xbuf.at[b], tile, TOKEN_PITCH).astype(BF16)
        hs = []
        for c in range(de // col):
            g = _dot(xb, wgb[:, c * col:(c + 1) * col])
            issue(per_piece)
            u = _dot(xb, wub[:, c * col:(c + 1) * col])
            issue(per_piece)
            hs.append((_silu(g) * u).astype(BF16))
        hb = jnp.concatenate(hs, axis=1)
        yb = ybuf.at[b]
        for c in range(d // col):
            y = _dot(hb, wdb[:, c * col:(c + 1) * col])
            for r in range(col // LANES):
                rows_r = pl.ds(c * (col // LANES) + r, tile, stride=TOKEN_PITCH)
                yb[rows_r, :] = jnp.where(keep, y[:, r * LANES:(r + 1) * LANES], yb[rows_r, :])
            issue(per_piece)
        issue(tile)

    @pl.when(v < nact)
    def _():
        t = vt_ref[v]
        b = lax.rem(t, 2)
        final = v == nact - 1
        first = (v == 0) | (vt_ref[jnp.maximum(v - 1, 0)] != t)
        last = final | (vt_ref[jnp.minimum(v + 1, nact - 1)] != t)

        @pl.when((v == 0) | (ve_ref[v] != ve_ref[jnp.maximum(v - 1, 0)]))
        def _():
            wgb[...] = wg_ref[...].astype(BF16)
            wub[...] = wu_ref[...].astype(BF16)
            wdb[...] = wd_ref[...].astype(BF16)

        @pl.when(first)
        def _():
            wait_gather(b)

        @pl.when(first & (t >= 2))
        def _():
            wait_scatter()

        want_gather = last & jnp.logical_not(final)
        want_scatter = first & (t >= 1)
        for do_gather in (True, False):
            for do_scatter in (True, False):
                pl.when((want_gather == do_gather) & (want_scatter == do_scatter))(
                    functools.partial(visit, do_gather, do_scatter))

        @pl.when(final)
        def _():
            looped(lambda i, prio: scatter_one(t * tile, b, i, prio))
            wait_scatter()

            @pl.when(t >= 1)
            def _():
                wait_scatter()


def _expert_ffn(pos, plan, x1t, w_gate, w_up, w_down, layer, *, n_prompt, n_sample,
                tile=EXPERT_TILE):
    d, de = w_gate.shape[2], w_gate.shape[3]
    n_visits = plan[0].shape[0]
    rows = TOP_K * (n_prompt + n_sample)
    assert n_prompt + n_sample >= tile
    n_tab = pl.cdiv(rows, tile) * tile
    wspec = lambda r, c: pl.BlockSpec(
        (None, None, r, c), lambda v, pos, vt, ve, lo, hi, na: (layer, ve[v], 0, 0))
    any_spec = pl.BlockSpec(memory_space=pl.ANY)
    return pl.pallas_call(
        functools.partial(_ffn_kernel, n_prompt=n_prompt, n_sample=n_sample, tile=tile),
        out_shape=jax.ShapeDtypeStruct(((rows + tile) * TOKEN_ROWS, LANES), F32),
        grid_spec=pltpu.PrefetchScalarGridSpec(
            num_scalar_prefetch=6,
            grid=(n_visits,),
            in_specs=[any_spec, wspec(d, de), wspec(d, de), wspec(de, d)],
            out_specs=any_spec,
            scratch_shapes=[pltpu.SMEM((n_tab,), I32),
                            pltpu.SMEM((n_tab,), I32),
                            pltpu.VMEM((2, tile * TOKEN_PITCH, LANES), F32),
                            pltpu.VMEM((2, tile * TOKEN_PITCH, LANES), F32),
                            pltpu.VMEM((d, de), BF16),
                            pltpu.VMEM((d, de), BF16),
                            pltpu.VMEM((de, d), BF16),
                            pltpu.SemaphoreType.DMA((2,)),
                            pltpu.SemaphoreType.DMA(())]),
        compiler_params=_cparams("arbitrary"),
        name="moe_ffn",
    )(pos, *plan, x1t, w_gate, w_up, w_down)


def _combine_kernel(x_ref, y0_ref, y1_ref, wt_ref, g_ref, b_ref, o_ref, *maybe_ob_ref):
    tile = o_ref.shape[0]
    wt = wt_ref[...]
    ffn = (wt[:, 0:1] * _load_token_major(y0_ref, tile, TOKEN_ROWS)
           + wt[:, 1:2] * _load_token_major(y1_ref, tile, TOKEN_ROWS))
    x1 = _load_token_major(x_ref, tile, TOKEN_PITCH)
    out = _layer_norm(ALPHA * x1 + ffn, g_ref[...], b_ref[...])
    o_ref[...] = out
    for ob_ref in maybe_ob_ref:
        ob_ref[...] = out.astype(BF16)


def _combine_ln(x1t, y2, wt, ln_g, ln_b, *, row0, m, slots, with_bf16, tile=COMBINE_TILE):
    d = ln_g.shape[1]
    tile = min(tile, m)
    assert row0 % tile == 0 and slots[0] % tile == 0 and slots[1] % tile == 0
    src = lambda base, pitch: pl.BlockSpec((tile * pitch, LANES),
                                           lambda i: (base // tile + i, 0))
    vec = pl.BlockSpec((1, d), lambda i: (0, 0))
    out = pl.BlockSpec((tile, d), lambda i: (i, 0))
    dtypes = (F32, BF16) if with_bf16 else (F32,)
    return pl.pallas_call(
        _combine_kernel,
        out_shape=tuple(jax.ShapeDtypeStruct((m, d), dt) for dt in dtypes),
        grid=(m // tile,),
        in_specs=[src(row0, TOKEN_PITCH), src(slots[0], TOKEN_ROWS), src(slots[1], TOKEN_ROWS),
                  pl.BlockSpec((tile, TOP_K), lambda i: (i, 0)), vec, vec],
        out_specs=tuple(out for _ in dtypes),
        compiler_params=_cparams("arbitrary"),
        name="moe_combine_ln",
    )(x1t, y2, y2, wt, ln_g, ln_b)


def _sortpos_kernel(e_ref, pos_ref, cnt_ref, cum_ref, *, chunk):
    n_chunks = e_ref.shape[1] // chunk
    tri = (lax.broadcasted_iota(I32, (chunk, chunk), 0)
           <= lax.broadcasted_iota(I32, (chunk, chunk), 1)).astype(BF16)
    row = lax.broadcasted_iota(I32, (N_EXPERTS, chunk), 0)
    hot = lambda c: row == e_ref[:, c * chunk:(c + 1) * chunk]
    starts = []
    running = jnp.zeros((N_EXPERTS, LANES), F32)
    for c in range(n_chunks):
        cum = _dot(hot(c).astype(BF16), tri)
        cum_ref[c] = cum
        starts.append(running)
        running = running + cum[:, chunk - 1:chunk]
    cnt_ref[...] = running.astype(I32)
    first_row = _seg_cumsum(running, N_EXPERTS) - running
    for c in range(n_chunks):
        before = (first_row + starts[c])[:, 0:1] + cum_ref[c]
        pos_ref[:, c * chunk:(c + 1) * chunk] = (
            jnp.sum(jnp.where(hot(c), before, 0.0), axis=0, keepdims=True) - 1.0).astype(I32)


def _sort_positions(flat, *, chunk=512):
    n = flat.shape[0]
    n_chunks = pl.cdiv(n, chunk)
    e = jnp.pad(flat, (0, n_chunks * chunk - n), constant_values=N_EXPERTS)
    pos, cnt = pl.pallas_call(
        functools.partial(_sortpos_kernel, chunk=chunk),
        out_shape=(jax.ShapeDtypeStruct((1, n_chunks * chunk), I32),
                   jax.ShapeDtypeStruct((N_EXPERTS, LANES), I32)),
        scratch_shapes=[pltpu.VMEM((n_chunks, N_EXPERTS, chunk), F32)],
        compiler_params=pltpu.CompilerParams(vmem_limit_bytes=VMEM_LIMIT_BYTES),
        name="moe_sort_positions",
    )(e.reshape(1, n_chunks * chunk))
    return pos[0, :n], cnt[:, 0]


def _route_plan(eid, tile, n_prompt):
    n_tok = eid.shape[1]
    pos, counts = _sort_positions(eid.reshape(-1))
    ends = jnp.cumsum(counts)
    offs = ends - counts
    first_tile = offs // tile
    last_tile = (ends - 1) // tile
    nvis = jnp.where(counts > 0, last_tile - first_tile + 1, 0)
    vend = jnp.cumsum(nvis)
    vbase = vend - nvis
    nact = vend[-1]
    n_visits = pl.cdiv(TOP_K * n_tok, tile) + N_EXPERTS - 1
    v = jnp.minimum(jnp.arange(n_visits, dtype=I32), nact - 1)
    e = jnp.sum((vend[None, :] <= v[:, None]).astype(I32), axis=1)
    hot = (e[:, None] == jnp.arange(N_EXPERTS, dtype=I32)[None, :]).astype(I32)
    look = lambda table: jnp.sum(hot * table[None, :].astype(I32), axis=1)
    t = look(first_tile) + (v - look(vbase))
    assert n_prompt % INVERT_UNROLL == 0 and (n_tok - n_prompt) % INVERT_UNROLL == 0
    counts_smem = jnp.stack([nact.astype(I32),
                             jnp.asarray(n_prompt // INVERT_UNROLL, I32),
                             jnp.asarray((n_tok - n_prompt) // INVERT_UNROLL, I32)])
    plan = (t.astype(I32), e.astype(I32), look(offs), look(ends), counts_smem)
    return pos, plan


def kernel(x_prompt, x_sample, state_conv, state_hgrn, cache_k_win, cache_v_win, w_in, w_out, conv_w, hg_lb_param, hg_gain, attn_sinks, ln1_g, ln1_b, w_group, b_group, w_router, b_router, w_gate, w_up, w_down, ln2_g, ln2_b):
    bsz, seq, d = x_prompt.shape
    nb = x_sample.shape[0]
    n_prompt = bsz * seq
    n_tok = n_prompt + nb
    w_buf = cache_k_win.shape[2]

    xp = x_prompt.reshape(n_prompt, d)
    xp_mm = xp
    xs = x_sample.reshape(nb, d)
    pad_rows = ROUTE_ROWS - N_GROUPS - N_EXPERTS
    w_out_b = w_out.astype(BF16)
    outs = {k: [] for k in ("cp", "cs", "hp", "hs", "kp", "ks", "vp", "vs")}
    for l in range(DEPTH):
        gain = hg_gain[l].reshape(1, HG_DK)
        sinks = attn_sinks[l].reshape(1, N_HEADS)
        rw_t = jnp.concatenate([w_group[l].T, w_router[l].T, jnp.zeros((pad_rows, d), F32)], axis=0)
        rb = jnp.concatenate([b_group[l], b_router[l], jnp.zeros((pad_rows,), F32)]).reshape(ROUTE_ROWS, 1)
        g1, b1 = ln1_g[l].reshape(1, d), ln1_b[l].reshape(1, d)
        g2, b2 = ln2_g[l].reshape(1, d), ln2_b[l].reshape(1, d)

        proj = _inproj(xp_mm, w_in, l, tm=2048, tn=512)
        ya, ctail = _conv_prompt(proj, conv_w[l], bsz, seq)
        yb, hstate = _hgrn_prompt(proj, hg_lb_param, gain, l, bsz, seq)
        yc = _attn_prompt(proj, sinks, bsz, seq)
        x1t, eid_p, ewt_p = _outproj_ln_route(
            [ya, yb, yc], xp, w_out_b, l, g1, b1, rw_t, rb, precise=False,
            x1_rows=n_tok, tn=d)
        outs["cp"].append(ctail[:, SUBLANES - (CONV_W - 1):])
        outs["hp"].append(hstate)
        kv = proj.reshape(bsz, seq, IN_DIM)[:, seq - w_buf:]
        outs["kp"].append(kv[:, :, OFF_AK:OFF_AK + KV_DIM].reshape(bsz, w_buf, N_KV_HEADS, HEAD_DIM))
        outs["vp"].append(kv[:, :, OFF_AV:OFF_AV + KV_DIM].reshape(bsz, w_buf, N_KV_HEADS, HEAD_DIM))

        proj_s = _inproj3(xs, w_in, l)
        ysm, cst, hst, kst, vst = _sample_mixers(
            proj_s, state_conv, state_hgrn, cache_k_win, cache_v_win,
            conv_w[l], hg_lb_param, gain, sinks, l)
        x1t, eid_s, ewt_s = _outproj_ln_route(
            [ysm.reshape(nb, d)], xs, w_out, l, g1, b1, rw_t, rb, precise=True,
            x1_rows=n_tok, x1_row0=n_prompt, x1_buf=x1t)
        outs["cs"].append(cst)
        outs["hs"].append(hst)
        outs["ks"].append(kst.reshape(nb, w_buf, N_KV_HEADS, HEAD_DIM))
        outs["vs"].append(vst.reshape(nb, w_buf, N_KV_HEADS, HEAD_DIM))

        eid = jnp.concatenate([eid_p[:TOP_K], eid_s[:TOP_K]], axis=1)
        pos, plan = _route_plan(eid, EXPERT_TILE, n_prompt)
        y2 = _expert_ffn(pos, plan, x1t, w_gate, w_up, w_down, l,
                         n_prompt=n_prompt, n_sample=nb)
        slots = lambda smp: [_slot_base(k, smp, n_prompt, nb) for k in range(TOP_K)]
        xp, *xp_bf16 = _combine_ln(x1t, y2, ewt_p[:TOP_K].T, g2, b2, row0=0, m=n_prompt,
                                   slots=slots(0), with_bf16=l + 1 < DEPTH)
        xp_mm = xp_bf16[0] if xp_bf16 else xp
        xs, = _combine_ln(x1t, y2, ewt_s[:TOP_K].T, g2, b2, row0=n_prompt, m=nb,
                          slots=slots(1), with_bf16=False)

    st = lambda k: jnp.stack(outs[k])
    return (xp.reshape(bsz, seq, d), xs.reshape(nb, 1, d), st("cp"), st("cs"),
            st("hp"), st("hs"), st("kp"), st("ks"), st("vp"), st("vs"))
```

```python
import functools

import jax
import jax.numpy as jnp
import numpy as np
from jax import lax
from jax.experimental import pallas as pl
from jax.experimental.pallas import tpu as pltpu

F32 = jnp.float32
BF16 = jnp.bfloat16
I32 = jnp.int32

D_MODEL = 2048
DEPTH = 2
CONV_DIM = 512
CONV_W = 3
HG_DIM = 512
HG_HEADS = 4
HG_DK = 128
HG_CHUNK = 16
HEAD_DIM = 64
ATTN_DIM = 1024
N_HEADS = 16
N_KV_HEADS = 4
Q_PER_KV = 4
KV_DIM = 256
WINDOW = 128
N_GROUPS = 4
EXPERTS_PER_GROUP = 4
N_EXPERTS = 16
TOP_K = 2
D_EXPERT = 512
ALPHA = (2 * DEPTH) ** 0.25
LOG2E = 1.4426950408889634
LN_EPS = 1e-5
RMS_EPS = 1e-6
IN_DIM = 5120
OFF_CB, OFF_CC, OFF_CH = 0, 512, 1024
OFF_HQ, OFF_HF, OFF_HI, OFF_HG = 1536, 2048, 2560, 3072
OFF_AQ, OFF_AK, OFF_AV = 3584, 4608, 4864

VMEM_LIMIT_BYTES = 56 * 1024 * 1024
LANES = 128
SUBLANES = 8

EXPERT_TILE = 256
FFN_PIECE_COLS = 256
COMBINE_TILE = 512
INVERT_UNROLL = 8
DMA_UNROLL = 8
SAMPLE_ROWS = 4
ROUTE_ROWS = 32


TOKEN_ROWS = D_MODEL // LANES
TOKEN_PITCH = 20


def _store_token_major(ref, x, pitch):
    n = x.shape[0]
    for c in range(TOKEN_ROWS):
        ref[pl.ds(c, n, stride=pitch), :] = x[:, c * LANES:(c + 1) * LANES]


def _zero_token_pad(ref, n, pitch):
    for c in range(TOKEN_ROWS, pitch):
        ref[pl.ds(c, n, stride=pitch), :] = jnp.zeros((n, LANES), ref.dtype)


def _load_token_major(ref, n, pitch):
    return jnp.concatenate(
        [ref[pl.ds(c, n, stride=pitch), :] for c in range(TOKEN_ROWS)], axis=1)


def _cparams(*sem):
    return pltpu.CompilerParams(dimension_semantics=sem,
                                vmem_limit_bytes=VMEM_LIMIT_BYTES)


def _dot(a, b):
    return jnp.dot(a, b, preferred_element_type=F32)


def _dot_nt(a, b):
    return lax.dot_general(a, b, (((1,), (1,)), ((), ())),
                           preferred_element_type=F32)


def _split(x):
    hi = x.astype(BF16)
    lo = (x - hi.astype(F32)).astype(BF16)
    return hi, lo


def _dot3(a, b):
    ah, al = _split(a)
    bh, bl = _split(b)
    return _dot(ah, bh) + (_dot(ah, bl) + _dot(al, bh))


def _dot3_nt(a, b):
    ah, al = _split(a)
    bh, bl = _split(b)
    return _dot_nt(ah, bh) + (_dot_nt(ah, bl) + _dot_nt(al, bh))


def _sigmoid(x):
    return 1.0 / (1.0 + jnp.exp(-x))


def _silu(x):
    return x * _sigmoid(x)


def _col_from_row(row):
    n = row.shape[1]
    eye = (lax.broadcasted_iota(I32, (n, n), 0)
           == lax.broadcasted_iota(I32, (n, n), 1))
    return jnp.sum(jnp.where(eye, row, 0.0), axis=1, keepdims=True)


def _hg_lower_bound(lbp, layer):
    m = jnp.max(lbp, axis=0, keepdims=True)
    e = jnp.exp(lbp - m)
    soft = e / jnp.sum(e, axis=0, keepdims=True)
    acc = soft[0:1]
    for i in range(1, layer + 1):
        acc = acc + soft[i:i + 1]
    return acc - soft[0:1]


def _alibi_slope(head):
    return float(2.0 ** (-8.0 * (head + 1) / N_HEADS))


def _inproj_kernel(x_ref, w_ref, o_ref, xb_ref):
    @pl.when(pl.program_id(1) == 0)
    def _():
        xb_ref[...] = x_ref[...].astype(BF16)

    o_ref[...] = _dot(xb_ref[...], w_ref[...].astype(BF16))


def _inproj(x, w, layer, *, tm=1024, tn=1024):
    m, k = x.shape
    n = w.shape[2]
    tm = min(tm, m)
    x_mode = dict(pipeline_mode=pl.Buffered(1)) if x.dtype == F32 else {}
    return pl.pallas_call(
        _inproj_kernel,
        out_shape=jax.ShapeDtypeStruct((m, n), F32),
        grid=(m // tm, n // tn),
        in_specs=[pl.BlockSpec((tm, k), lambda i, j: (i, 0), **x_mode),
                  pl.BlockSpec((None, k, tn), lambda i, j: (layer, 0, j))],
        out_specs=pl.BlockSpec((tm, tn), lambda i, j: (i, j)),
        scratch_shapes=[pltpu.VMEM((tm, k), BF16)],
        compiler_params=_cparams("arbitrary", "arbitrary"),
        name="inproj",
    )(x, w)


def _inproj3_kernel(x_ref, w_ref, o_ref):
    o_ref[...] = _dot3(x_ref[...], w_ref[...])


def _inproj3(x, w, layer, *, tn=512):
    m, k = x.shape
    n = w.shape[2]
    return pl.pallas_call(
        _inproj3_kernel,
        out_shape=jax.ShapeDtypeStruct((m, n), F32),
        grid=(n // tn,),
        in_specs=[pl.BlockSpec((m, k), lambda j: (0, 0)),
                  pl.BlockSpec((None, k, tn), lambda j: (layer, 0, j))],
        out_specs=pl.BlockSpec((m, tn), lambda j: (0, j)),
        compiler_params=_cparams("arbitrary"),
        name="inproj_sample",
    )(x, w)


def _conv_kernel(cb_ref, cc_ref, ch_ref, w_ref, y_ref, tail_ref, carry_ref):
    i = pl.program_id(1)

    @pl.when(i == 0)
    def _():
        carry_ref[...] = jnp.zeros_like(carry_ref)

    u = cc_ref[...] * ch_ref[...]
    tl = u.shape[0]
    row = lax.broadcasted_iota(I32, u.shape, 0)
    prev1 = carry_ref[SUBLANES - 1:SUBLANES, :]
    prev2 = carry_ref[SUBLANES - 2:SUBLANES - 1, :]
    u1 = jnp.where(row == 0, prev1, pltpu.roll(u, 1, 0))
    u2 = jnp.where(row == 0, prev2, jnp.where(row == 1, prev1, pltpu.roll(u, 2, 0)))
    w = w_ref[...]
    y = w[0:1] * u2 + w[1:2] * u1 + w[2:3] * u
    y_ref[...] = (cb_ref[...] * y).astype(y_ref.dtype)
    tail = u[tl - SUBLANES:tl, :]
    carry_ref[...] = tail
    tail_ref[...] = tail


def _conv_prompt(proj, conv_w, bsz, seq, *, tl=512):
    tl = min(tl, seq)
    nt = seq // tl
    cblk = lambda c: pl.BlockSpec((tl, CONV_DIM), lambda b, i, c=c: (b * nt + i, c))
    return pl.pallas_call(
        _conv_kernel,
        out_shape=(jax.ShapeDtypeStruct((bsz * seq, CONV_DIM), BF16),
                   jax.ShapeDtypeStruct((bsz, SUBLANES, CONV_DIM), F32)),
        grid=(bsz, nt),
        in_specs=[cblk(OFF_CB // CONV_DIM), cblk(OFF_CC // CONV_DIM),
                  cblk(OFF_CH // CONV_DIM),
                  pl.BlockSpec((CONV_W, CONV_DIM), lambda b, i: (0, 0))],
        out_specs=(pl.BlockSpec((tl, CONV_DIM), lambda b, i: (b * nt + i, 0)),
                   pl.BlockSpec((None, SUBLANES, CONV_DIM), lambda b, i: (b, 0, 0))),
        scratch_shapes=[pltpu.VMEM((SUBLANES, CONV_DIM), F32)],
        compiler_params=_cparams("arbitrary", "arbitrary"),
        name="conv_prompt",
    )(proj, proj, proj, conv_w)


def _seg_cumsum(g, seg):
    row = lax.broadcasted_iota(I32, g.shape, 0) % seg
    s = 1
    while s < seg:
        g = g + jnp.where(row >= s, pltpu.roll(g, s, 0), 0.0)
        s *= 2
    return g


def _hgrn_kernel(hq_ref, hf_ref, hi_ref, hg_ref, lbp_ref, gain_ref,
                 y_ref, st_ref, cum_ref, k_ref, q_ref, o_ref, s_ref, *, layer):
    i = pl.program_id(1)
    nt = pl.num_programs(1)
    tb = hq_ref.shape[0]
    c = HG_CHUNK

    @pl.when(i == 0)
    def _():
        s_ref[...] = jnp.zeros_like(s_ref)

    lb = _hg_lower_bound(lbp_ref[...], layer)
    f = lb + (1.0 - lb) * _sigmoid(hf_ref[...])
    cum_ref[...] = _seg_cumsum(jnp.log(f) * LOG2E, c)
    k_ref[...] = 1.0 - f
    q_ref[...] = hq_ref[...] * (HG_DK ** -0.5)

    ones = jnp.ones((HG_DK, HG_DK), BF16)
    hc = c // 2
    trow = lax.broadcasted_iota(I32, (hc, HG_DK), 0)

    def chunk(ci, carry):
        r0 = pl.multiple_of(ci * c, c)
        for h in range(HG_HEADS):
            cols = slice(h * HG_DK, (h + 1) * HG_DK)
            cum = cum_ref[pl.ds(r0, c), cols]
            kc = k_ref[pl.ds(r0, c), cols]
            qc = q_ref[pl.ds(r0, c), cols]
            vc = hi_ref[pl.ds(r0, c), cols]
            parts = []
            for s in range(c):
                cs, qk_hi = cum[s:s + 1], qc[hc:] * kc[s:s + 1]
                if s < hc:
                    dlo = jnp.exp2(jnp.where(trow >= s, cum[:hc] - cs, -jnp.inf))
                    parts.append((qc[:hc] * kc[s:s + 1] * dlo).astype(BF16))
                    dhi = jnp.exp2(cum[hc:] - cs)
                else:
                    dhi = jnp.exp2(jnp.where(trow >= s - hc, cum[hc:] - cs, -jnp.inf))
                parts.append((qk_hi * dhi).astype(BF16))
            sc = _dot(jnp.concatenate(parts, axis=0), ones)
            o_lo = jnp.zeros((hc, HG_DK), F32)
            o_hi = jnp.zeros((hc, HG_DK), F32)
            r = 0
            for s in range(c):
                if s < hc:
                    o_lo = o_lo + sc[r:r + hc] * vc[s:s + 1]
                    r += hc
                o_hi = o_hi + sc[r:r + hc] * vc[s:s + 1]
                r += hc
            o = jnp.concatenate([o_lo, o_hi], axis=0)
            st = s_ref[h]
            last = cum[c - 1:c]
            o = o + _dot_nt((qc * jnp.exp2(cum)).astype(BF16), st.astype(BF16))
            kte = (kc * jnp.exp2(last - cum)).astype(BF16)
            du = lax.dot_general(vc.astype(BF16), kte, (((0,), (0,)), ((), ())),
                                 preferred_element_type=F32)
            s_ref[h] = jnp.exp2(last) * st + du
            o_ref[pl.ds(r0, c), cols] = o
        return carry

    lax.fori_loop(0, tb // c, chunk, 0, unroll=4)

    gain = gain_ref[...]
    outs = []
    for h in range(HG_HEADS):
        cols = slice(h * HG_DK, (h + 1) * HG_DK)
        o = o_ref[:, cols]
        o = o * lax.rsqrt(jnp.mean(o * o, axis=-1, keepdims=True) + RMS_EPS) * gain
        outs.append(o * _silu(hg_ref[:, cols]))
    y_ref[...] = jnp.concatenate(outs, axis=-1).astype(y_ref.dtype)

    @pl.when(i == nt - 1)
    def _():
        for h in range(HG_HEADS):
            st_ref[h] = s_ref[h].T


def _hgrn_prompt(proj, lb_param, gain, layer, bsz, seq, *, tb=512):
    tb = min(tb, seq)
    nt = seq // tb
    cblk = lambda c: pl.BlockSpec((tb, HG_DIM), lambda b, i, c=c: (b * nt + i, c))
    return pl.pallas_call(
        functools.partial(_hgrn_kernel, layer=layer),
        out_shape=(jax.ShapeDtypeStruct((bsz * seq, HG_DIM), BF16),
                   jax.ShapeDtypeStruct((bsz, HG_HEADS, HG_DK, HG_DK), F32)),
        grid=(bsz, nt),
        in_specs=[cblk(OFF_HQ // HG_DIM), cblk(OFF_HF // HG_DIM),
                  cblk(OFF_HI // HG_DIM), cblk(OFF_HG // HG_DIM),
                  pl.BlockSpec((DEPTH, HG_DIM), lambda b, i: (0, 0)),
                  pl.BlockSpec((1, HG_DK), lambda b, i: (0, 0))],
        out_specs=(pl.BlockSpec((tb, HG_DIM), lambda b, i: (b * nt + i, 0)),
                   pl.BlockSpec((None, HG_HEADS, HG_DK, HG_DK), lambda b, i: (b, 0, 0, 0))),
        scratch_shapes=[pltpu.VMEM((tb, HG_DIM), F32),
                        pltpu.VMEM((tb, HG_DIM), F32),
                        pltpu.VMEM((tb, HG_DIM), F32),
                        pltpu.VMEM((tb, HG_DIM), F32),
                        pltpu.VMEM((HG_HEADS, HG_DK, HG_DK), F32)],
        compiler_params=_cparams("arbitrary", "arbitrary"),
        name="hgrn_prompt",
    )(proj, proj, proj, proj, lb_param, gain)


def _attn_kernel(qa_ref, qb_ref, kc_ref, vc_ref, kp_ref, vp_ref, sink_ref, y_ref,
                 bias_ref, sinkcol_ref):
    n = pl.program_id(1)
    w = WINDOW
    qi = lax.broadcasted_iota(I32, (Q_PER_KV * w, 2 * w), 0) % w
    kj = lax.broadcasted_iota(I32, (Q_PER_KV * w, 2 * w), 1)

    @pl.when((pl.program_id(0) == 0) & (n == 0))
    def _():
        dist = qi + w - kj
        valid = (dist >= 0) & (dist < w)
        distf = dist.astype(F32)
        for kv in range(N_KV_HEADS):
            grp = lax.broadcasted_iota(I32, (Q_PER_KV * w, 1), 0) // w
            slope = jnp.zeros((Q_PER_KV * w, 1), F32)
            for g in range(Q_PER_KV):
                slope = jnp.where(grp == g, _alibi_slope(kv * Q_PER_KV + g), slope)
            bias = jnp.where(valid, -slope * distf, -jnp.inf)
            bias_ref[1, pl.ds(kv * Q_PER_KV * w, Q_PER_KV * w), :] = bias
            bias_ref[0, pl.ds(kv * Q_PER_KV * w, Q_PER_KV * w), :] = jnp.where(
                kj >= w, bias, -jnp.inf)
        for h in range(N_HEADS):
            sinkcol_ref[pl.ds(h * w, w), :] = jnp.broadcast_to(sink_ref[:, h:h + 1], (w, LANES))

    low = lax.broadcasted_iota(I32, (1, LANES), 1) < HEAD_DIM
    ones = jnp.ones((2 * w, LANES), BF16)
    scores, values = [], []
    for kv in range(N_KV_HEADS):
        q_ref = qa_ref if kv < 2 else qb_ref
        qoff = (kv % 2) * Q_PER_KV * HEAD_DIM
        kt = slice((kv // 2) * LANES, (kv // 2 + 1) * LANES)

        def both_halves(prev_ref, cur_ref):
            t = jnp.concatenate([prev_ref[:, kt], cur_ref[:, kt]], axis=0)
            r = pltpu.roll(t, HEAD_DIM, 1)
            return (jnp.where(low, t, r) if kv % 2 == 0 else jnp.where(low, r, t)).astype(BF16)

        q_parts = []
        for g in range(Q_PER_KV):
            qt = q_ref[:, qoff + (g // 2) * LANES: qoff + (g // 2 + 1) * LANES]
            q_parts.append(jnp.where(low if g % 2 == 0 else ~low, qt, 0.0))
        q = jnp.concatenate(q_parts, axis=0).astype(BF16)
        scores.append(_dot_nt(q, both_halves(kp_ref, kc_ref)))
        values.append(both_halves(vp_ref, vc_ref))
    s = jnp.concatenate(scores, axis=0) * (HEAD_DIM ** -0.5) + bias_ref[jnp.minimum(n, 1)]
    sink = sinkcol_ref[...]
    m = jnp.maximum(jnp.max(s, axis=-1, keepdims=True), sink)
    p = jnp.exp(s - jnp.concatenate([m, m], axis=1)).astype(BF16)
    rows = Q_PER_KV * w
    outs = []
    for kv in range(N_KV_HEADS):
        pk = p[kv * rows:(kv + 1) * rows]
        sl = slice(kv * rows, (kv + 1) * rows)
        denom = _dot(pk, ones) + jnp.exp(sink[sl] - m[sl])
        o = _dot(pk, values[kv]) / denom
        for g in range(0, Q_PER_KV, 2):
            outs.append(jnp.where(low, o[g * w:(g + 1) * w], o[(g + 1) * w:(g + 2) * w]))
    y_ref[...] = jnp.concatenate(outs, axis=-1).astype(y_ref.dtype)


def _attn_prompt(proj, sinks, bsz, seq):
    w = WINDOW
    nb = seq // w
    half = ATTN_DIM // 2
    cur = lambda width, off: pl.BlockSpec(
        (w, width), lambda b, n: (b * nb + n, off // width))
    prev = lambda width, off: pl.BlockSpec(
        (w, width), lambda b, n: (b * nb + jnp.maximum(n - 1, 0), off // width))
    return pl.pallas_call(
        _attn_kernel,
        out_shape=jax.ShapeDtypeStruct((bsz * seq, ATTN_DIM), BF16),
        grid=(bsz, nb),
        in_specs=[cur(half, OFF_AQ), cur(half, OFF_AQ + half),
                  cur(KV_DIM, OFF_AK), cur(KV_DIM, OFF_AV),
                  prev(KV_DIM, OFF_AK), prev(KV_DIM, OFF_AV),
                  pl.BlockSpec((1, N_HEADS), lambda b, n: (0, 0))],
        out_specs=pl.BlockSpec((w, ATTN_DIM), lambda b, n: (b * nb + n, 0)),
        scratch_shapes=[pltpu.VMEM((2, N_HEADS * w, 2 * w), F32),
                        pltpu.VMEM((N_HEADS * w, LANES), F32)],
        compiler_params=_cparams("arbitrary", "arbitrary"),
        name="attn_prompt",
    )(proj, proj, proj, proj, proj, proj, sinks)


def _sample_mix_kernel(p_ref, sc_ref, s0_ref, kc_ref, vc_ref, cw_ref, lbp_ref,
                       gain_ref, sink_ref,
                       y_ref, sco_ref, so_ref, ko_ref, vo_ref, *, layer):
    for r in range(p_ref.shape[0]):
        _sample_mix_one(p_ref.at[r], sc_ref.at[r], s0_ref.at[r], kc_ref.at[r], vc_ref.at[r],
                        cw_ref, lbp_ref, gain_ref, sink_ref,
                        y_ref.at[r], sco_ref.at[r], so_ref.at[r], ko_ref.at[r], vo_ref.at[r],
                        layer)


def _sample_mix_one(p_ref, sc_ref, s0_ref, kc_ref, vc_ref, cw_ref, lbp_ref, gain_ref, sink_ref,
                    y_ref, sco_ref, so_ref, ko_ref, vo_ref, layer):
    p = p_ref[...]
    seg = lambda off, n: p[:, off:off + n]
    u = seg(OFF_CC, CONV_DIM) * seg(OFF_CH, CONV_DIM)
    hist = sc_ref[...]
    cw = cw_ref[...]
    conv = cw[0:1] * hist[0:1] + cw[1:2] * hist[1:2] + cw[2:3] * u
    ya = seg(OFF_CB, CONV_DIM) * conv
    sco_ref[...] = jnp.concatenate([hist[1:2], u], axis=0)
    lb = _hg_lower_bound(lbp_ref[...], layer)
    f = lb + (1.0 - lb) * _sigmoid(seg(OFF_HF, HG_DIM))
    g = jnp.log(f)
    kk = 1.0 - f
    q = seg(OFF_HQ, HG_DIM) * (HG_DK ** -0.5)
    v = seg(OFF_HI, HG_DIM)
    gate = seg(OFF_HG, HG_DIM)
    gain = gain_ref[...]
    yb = []
    for h in range(HG_HEADS):
        cols = slice(h * HG_DK, (h + 1) * HG_DK)
        s0 = s0_ref[h]
        eg = jnp.exp(g[:, cols])
        qe_col = _col_from_row(q[:, cols] * eg)
        o = (jnp.sum(q[:, cols] * kk[:, cols], axis=-1, keepdims=True) * v[:, cols]
             + jnp.sum(qe_col * s0, axis=0, keepdims=True))
        so_ref[h] = _col_from_row(eg) * s0 + _col_from_row(kk[:, cols]) * v[:, cols]
        o = o * lax.rsqrt(jnp.mean(o * o, axis=-1, keepdims=True) + RMS_EPS) * gain
        yb.append(o * _silu(gate[:, cols]))
    w = kc_ref.shape[0]
    kcache = kc_ref[...]
    vcache = vc_ref[...]
    knew = seg(OFF_AK, KV_DIM)
    vnew = seg(OFF_AV, KV_DIM)
    aq = seg(OFF_AQ, ATTN_DIM)
    sinks = sink_ref[...]
    kj = lax.broadcasted_iota(I32, (N_HEADS, w), 1)
    dist = w - kj
    valid = dist < WINDOW
    q_rows = []
    for hd in range(N_HEADS):
        kv = hd // Q_PER_KV
        piece = aq[:, hd * HEAD_DIM:(hd + 1) * HEAD_DIM]
        parts = ([jnp.zeros((1, kv * HEAD_DIM), F32)] if kv else []) + [piece]
        if kv + 1 < N_KV_HEADS:
            parts.append(jnp.zeros((1, (N_KV_HEADS - kv - 1) * HEAD_DIM), F32))
        q_rows.append(jnp.concatenate(parts, axis=1))
    q_all = jnp.concatenate(q_rows, axis=0)
    head = lax.broadcasted_iota(I32, (N_HEADS, 1), 0).astype(F32)
    slope = jnp.exp2(-8.0 * (head + 1.0) / N_HEADS)
    sink = _col_from_row(sinks)
    scale = HEAD_DIM ** -0.5
    sc = _dot3_nt(q_all, kcache) * scale - slope * dist.astype(F32)
    sc = jnp.where(valid, sc, -jnp.inf)
    sn = jnp.sum(q_all * knew, axis=-1, keepdims=True) * scale
    m = jnp.maximum(jnp.maximum(jnp.max(sc, axis=-1, keepdims=True), sn), sink)
    pc = jnp.exp(sc - m)
    pn = jnp.exp(sn - m)
    denom = jnp.sum(pc, axis=-1, keepdims=True) + pn + jnp.exp(sink - m)
    o_all = (_dot3(pc, vcache) + pn * vnew) / denom
    yc = [o_all[hd:hd + 1, (hd // Q_PER_KV) * HEAD_DIM:(hd // Q_PER_KV + 1) * HEAD_DIM]
          for hd in range(N_HEADS)]
    y_ref[...] = jnp.concatenate([ya] + yb + yc, axis=-1)
    row = lax.broadcasted_iota(I32, (w, KV_DIM), 0)
    ko_ref[...] = jnp.where(row == w - 1, knew, pltpu.roll(kcache, w - 1, 0))
    vo_ref[...] = jnp.where(row == w - 1, vnew, pltpu.roll(vcache, w - 1, 0))


def _sample_mixers(proj, state_conv, state_hgrn, cache_k, cache_v, conv_w,
                   lb_param, gain, sinks, layer):
    nb = proj.shape[0]
    w = cache_k.shape[2]
    rows = SAMPLE_ROWS
    assert nb % rows == 0
    per_b = lambda *shape: pl.BlockSpec((rows,) + shape,
                                        lambda b: (b,) + (0,) * len(shape))
    per_lb = lambda *shape: pl.BlockSpec((None, rows) + shape,
                                         lambda b: (layer, b) + (0,) * len(shape))
    whole = lambda *shape: pl.BlockSpec(shape, lambda b: (0,) * len(shape))
    return pl.pallas_call(
        functools.partial(_sample_mix_kernel, layer=layer),
        out_shape=(jax.ShapeDtypeStruct((nb, 1, D_MODEL), F32),
                   jax.ShapeDtypeStruct((nb, CONV_W - 1, CONV_DIM), F32),
                   jax.ShapeDtypeStruct((nb, HG_HEADS, HG_DK, HG_DK), F32),
                   jax.ShapeDtypeStruct((nb, w, KV_DIM), F32),
                   jax.ShapeDtypeStruct((nb, w, KV_DIM), F32)),
        grid=(nb // rows,),
        in_specs=[per_b(1, IN_DIM), per_lb(CONV_W - 1, CONV_DIM),
                  per_lb(HG_HEADS, HG_DK, HG_DK), per_lb(w, KV_DIM), per_lb(w, KV_DIM),
                  whole(CONV_W, CONV_DIM), whole(DEPTH, HG_DIM), whole(1, HG_DK),
                  whole(1, N_HEADS)],
        out_specs=(per_b(1, D_MODEL), per_b(CONV_W - 1, CONV_DIM),
                   per_b(HG_HEADS, HG_DK, HG_DK), per_b(w, KV_DIM), per_b(w, KV_DIM)),
        compiler_params=_cparams("arbitrary"),
        name="sample_mixers",
    )(proj.reshape(nb, 1, IN_DIM), state_conv, state_hgrn,
      cache_k.reshape(DEPTH, nb, w, KV_DIM), cache_v.reshape(DEPTH, nb, w, KV_DIM),
      conv_w, lb_param, gain, sinks)


def _layer_norm(z, g, b):
    mu = jnp.mean(z, axis=-1, keepdims=True)
    zc = z - mu
    var = jnp.mean(zc * zc, axis=-1, keepdims=True)
    return zc * lax.rsqrt(var + LN_EPS) * g + b


def _route(logits, bias):
    lg = logits + bias
    rowv = lambda r: lg[r:r + 1]
    best, gidx = rowv(0), jnp.zeros_like(rowv(0), dtype=I32)
    for r in range(1, N_GROUPS):
        upd = rowv(r) > best
        best = jnp.where(upd, rowv(r), best)
        gidx = jnp.where(upd, r, gidx)
    gden = sum(jnp.exp(rowv(r) - best) for r in range(N_GROUPS))
    gprob = 1.0 / gden
    ev = []
    for j in range(EXPERTS_PER_GROUP):
        val = rowv(N_GROUPS + j)
        for grp in range(1, N_GROUPS):
            val = jnp.where(gidx == grp, rowv(N_GROUPS + grp * EXPERTS_PER_GROUP + j), val)
        ev.append(val)
    v1, j1 = ev[0], jnp.zeros_like(gidx)
    for j in range(1, EXPERTS_PER_GROUP):
        upd = ev[j] > v1
        v1 = jnp.where(upd, ev[j], v1)
        j1 = jnp.where(upd, j, j1)
    v2, j2 = jnp.full_like(v1, -jnp.inf), jnp.zeros_like(gidx)
    for j in range(EXPERTS_PER_GROUP):
        upd = (j1 != j) & (ev[j] > v2)
        v2 = jnp.where(upd, ev[j], v2)
        j2 = jnp.where(upd, j, j2)
    e2 = jnp.exp(v2 - v1)
    w1 = gprob / (1.0 + e2)
    w2 = gprob * e2 / (1.0 + e2)
    base = gidx * EXPERTS_PER_GROUP
    return (jnp.concatenate([base + j1, base + j2], axis=0),
            jnp.concatenate([w1, w2], axis=0))


def _outproj_kernel(*refs, n_y, precise, n_real, n_j):
    y_refs = refs[:n_y]
    x_ref, w_ref, g_ref, b_ref, rw_ref, rb_ref = refs[n_y:n_y + 6]
    x1t_ref, eid_ref, ewt_ref, acc_ref = refs[-4:]
    i = pl.program_id(0)
    j = pl.program_id(1)
    tn = w_ref.shape[1]
    mm = _dot3 if precise else _dot

    def mix():
        y = jnp.concatenate([y_ref[...] for y_ref in y_refs], axis=1)
        return mm(y, w_ref[...])

    if n_j > 1:
        @pl.when(i < n_real)
        def _():
            acc_ref[:, pl.ds(pl.multiple_of(j * tn, tn), tn)] = mix()

    @pl.when((i >= n_real) & (j == n_j - 1))
    def _():
        x1t_ref[...] = jnp.zeros_like(x1t_ref)

    @pl.when((i < n_real) & (j == n_j - 1))
    def _():
        z = ALPHA * x_ref[...] + (acc_ref[...] if n_j > 1 else mix())
        x1 = _layer_norm(z, g_ref[...], b_ref[...])
        _store_token_major(x1t_ref, x1, TOKEN_PITCH)
        _zero_token_pad(x1t_ref, x1.shape[0], TOKEN_PITCH)
        hi, lo = _split(x1)
        rwh, rwl = _split(rw_ref[...])
        both = _dot_nt(jnp.concatenate([rwh, rwl], axis=0), hi)
        logits = both[:ROUTE_ROWS] + (both[ROUTE_ROWS:] + _dot_nt(rwh, lo))
        eid, ewt = _route(logits, rb_ref[...])
        t = eid.shape[1]
        eid_ref[...] = jnp.concatenate([eid, jnp.zeros((SUBLANES - TOP_K, t), I32)], axis=0)
        ewt_ref[...] = jnp.concatenate([ewt, jnp.zeros((SUBLANES - TOP_K, t), F32)], axis=0)


def _outproj_ln_route(ys, x, w_out, layer, ln_g, ln_b, rw_t, rb, *, precise,
                      x1_rows, x1_row0=0, x1_buf=None, tm=512, tn=1024):
    m, d = x.shape
    tm = min(tm, m)
    assert x1_row0 % tm == 0
    n_y = len(ys)
    n_real = m // tm
    n_j = d // tn
    n_i = n_real if x1_buf is not None else pl.cdiv(x1_rows, tm)
    real = lambda i: jnp.minimum(i, n_real - 1)
    y_specs = [pl.BlockSpec((tm, y.shape[1]), lambda i, j: (real(i), 0)) for y in ys]
    vec = lambda: pl.BlockSpec((1, d), lambda i, j: (0, 0))
    operands = list(ys) + [x, w_out, ln_g, ln_b, rw_t, rb]
    in_specs = y_specs + [pl.BlockSpec((tm, d), lambda i, j: (real(i), 0)),
                          pl.BlockSpec((None, d, tn),
                                       lambda i, j: (layer, 0, jnp.where(i < n_real, j, n_j - 1))),
                          vec(), vec(),
                          pl.BlockSpec((ROUTE_ROWS, d), lambda i, j: (0, 0)),
                          pl.BlockSpec((ROUTE_ROWS, 1), lambda i, j: (0, 0))]
    aliases = {}
    if x1_buf is not None:
        aliases = {len(operands): 0}
        operands.append(x1_buf)
        in_specs.append(pl.BlockSpec(memory_space=pl.ANY))
    return pl.pallas_call(
        functools.partial(_outproj_kernel, n_y=n_y, precise=precise, n_real=n_real, n_j=n_j),
        out_shape=(jax.ShapeDtypeStruct((x1_rows * TOKEN_PITCH, LANES), F32),
                   jax.ShapeDtypeStruct((SUBLANES, m), I32),
                   jax.ShapeDtypeStruct((SUBLANES, m), F32)),
        grid=(n_i, n_j),
        in_specs=in_specs,
        out_specs=(pl.BlockSpec((tm * TOKEN_PITCH, LANES), lambda i, j: (x1_row0 // tm + i, 0)),
                   pl.BlockSpec((SUBLANES, tm), lambda i, j: (0, real(i))),
                   pl.BlockSpec((SUBLANES, tm), lambda i, j: (0, real(i)))),
        scratch_shapes=[pltpu.VMEM((tm, d) if n_j > 1 else (SUBLANES, LANES), F32)],
        input_output_aliases=aliases,
        compiler_params=_cparams("arbitrary", "arbitrary"),
        name="outproj_ln_route",
    )(*operands)


def _slot_base(k, is_sample, n_prompt, n_sample):
    return k * n_prompt + is_sample * (TOP_K * n_prompt + k * (n_sample - n_prompt))


def _token_copy(src_ref, src_row, dst_ref, dst_row, sem):
    return pltpu.make_async_copy(src_ref.at[pl.ds(src_row, TOKEN_ROWS)],
                                 dst_ref.at[pl.ds(dst_row, TOKEN_ROWS)], sem)


def _ffn_kernel(pos_ref, vt_ref, ve_ref, lo_ref, hi_ref, nact_ref,
                x_hbm, wg_ref, wu_ref, wd_ref, y_hbm,
                src_ref, dst_ref, xbuf, ybuf, wgb, wub, wdb, semx, semy,
                *, n_prompt, n_sample, tile):
    v = pl.program_id(0)
    nact = nact_ref[0]
    n_tok = n_prompt + n_sample
    rows = TOP_K * n_tok
    n_tab = src_ref.shape[0]
    dump_row0 = rows * TOKEN_ROWS
    col = FFN_PIECE_COLS
    d, de = wgb.shape

    def gather_one(t0, b, i, prio):
        _token_copy(x_hbm, src_ref[t0 + i], xbuf.at[b], i * TOKEN_PITCH,
                    semx.at[b]).start(priority=prio)

    def scatter_one(t0, b, i, prio):
        _token_copy(ybuf.at[b], i * TOKEN_PITCH, y_hbm, dst_ref[t0 + i],
                    semy).start(priority=prio)

    def looped(one):
        def body(grp, c):
            for u in range(DMA_UNROLL):
                one(grp * DMA_UNROLL + u, u % 2)
            return c

        lax.fori_loop(0, tile // DMA_UNROLL, body, 0)

    tile_rows = pl.ds(0, tile * TOKEN_ROWS)

    def wait_gather(b):
        pltpu.make_async_copy(x_hbm.at[tile_rows], xbuf.at[b].at[tile_rows], semx.at[b]).wait()

    def wait_scatter():
        pltpu.make_async_copy(ybuf.at[0].at[tile_rows], y_hbm.at[tile_rows], semy).wait()

    @pl.when(v == 0)
    def _():
        dump = pltpu.make_async_copy(
            x_hbm.at[tile_rows], y_hbm.at[pl.ds(dump_row0, tile * TOKEN_ROWS)], semy)
        dump.start()

        def invert_segment(k, smp, trips):
            s0 = k * n_tok + smp * n_prompt
            tok0 = smp * n_prompt
            slot0 = _slot_base(k, smp, n_prompt, n_sample)

            def body(grp, c):
                j0 = grp * INVERT_UNROLL
                src0 = (tok0 + j0) * TOKEN_PITCH
                dst0 = (slot0 + j0) * TOKEN_ROWS
                for u in range(INVERT_UNROLL):
                    p = pos_ref[s0 + j0 + u]
                    src_ref[p] = src0 + u * TOKEN_PITCH
                    dst_ref[p] = dst0 + u * TOKEN_ROWS
                return c

            lax.fori_loop(0, trips, body, 0)

        for k in range(TOP_K):
            invert_segment(k, 0, nact_ref[1])
            invert_segment(k, 1, nact_ref[2])
        for p in range(rows, n_tab):
            src_ref[p] = 0
            dst_ref[p] = dump_row0 + (p - rows) * TOKEN_ROWS
        ybuf[...] = jnp.zeros_like(ybuf)
        dump.wait()
        looped(lambda i, prio: gather_one(0, 0, i, prio))

    def visit(do_gather, do_scatter):
        t = vt_ref[v]
        b = lax.rem(t, 2)
        first = (v == 0) | (vt_ref[jnp.maximum(v - 1, 0)] != t)
        pending = iter(range(tile))

        def issue(count):
            for _ in range(count):
                i = next(pending, None)
                if i is not None and do_gather:
                    gather_one((t + 1) * tile, 1 - b, i, i % 2)
                if i is not None and do_scatter:
                    scatter_one((t - 1) * tile, 1 - b, i, (i + 1) % 2)

        row = t * tile + lax.broadcasted_iota(I32, (tile, 1), 0)
        keep = first | ((row >= lo_ref[v]) & (row < hi_ref[v]))
        n_pieces = 2 * (de // col) + d // col
        per_piece = pl.cdiv(tile, n_pieces)
        xb = _load_token_major(xbuf.at[b], tile, TOKEN_PITCH).astype(BF16)
        hs = []
        for c in range(de // col):
            g = _dot(xb, wgb[:, c * col:(c + 1) * col])
            issue(per_piece)
            u = _dot(xb, wub[:, c * col:(c + 1) * col])
            issue(per_piece)
            hs.append((_silu(g) * u).astype(BF16))
        hb = jnp.concatenate(hs, axis=1)
        yb = ybuf.at[b]
        for c in range(d // col):
            y = _dot(hb, wdb[:, c * col:(c + 1) * col])
            for r in range(col // LANES):
                rows_r = pl.ds(c * (col // LANES) + r, tile, stride=TOKEN_PITCH)
                yb[rows_r, :] = jnp.where(keep, y[:, r * LANES:(r + 1) * LANES], yb[rows_r, :])
            issue(per_piece)
        issue(tile)

    @pl.when(v < nact)
    def _():
        t = vt_ref[v]
        b = lax.rem(t, 2)
        final = v == nact - 1
        first = (v == 0) | (vt_ref[jnp.maximum(v - 1, 0)] != t)
        last = final | (vt_ref[jnp.minimum(v + 1, nact - 1)] != t)

        @pl.when((v == 0) | (ve_ref[v] != ve_ref[jnp.maximum(v - 1, 0)]))
        def _():
            wgb[...] = wg_ref[...].astype(BF16)
            wub[...] = wu_ref[...].astype(BF16)
            wdb[...] = wd_ref[...].astype(BF16)

        @pl.when(first)
        def _():
            wait_gather(b)

        @pl.when(first & (t >= 2))
        def _():
            wait_scatter()

        want_gather = last & jnp.logical_not(final)
        want_scatter = first & (t >= 1)
        for do_gather in (True, False):
            for do_scatter in (True, False):
                pl.when((want_gather == do_gather) & (want_scatter == do_scatter))(
                    functools.partial(visit, do_gather, do_scatter))

        @pl.when(final)
        def _():
            looped(lambda i, prio: scatter_one(t * tile, b, i, prio))
            wait_scatter()

            @pl.when(t >= 1)
            def _():
                wait_scatter()


def _expert_ffn(pos, plan, x1t, w_gate, w_up, w_down, layer, *, n_prompt, n_sample,
                tile=EXPERT_TILE):
    d, de = w_gate.shape[2], w_gate.shape[3]
    n_visits = plan[0].shape[0]
    rows = TOP_K * (n_prompt + n_sample)
    assert n_prompt + n_sample >= tile
    n_tab = pl.cdiv(rows, tile) * tile
    wspec = lambda r, c: pl.BlockSpec(
        (None, None, r, c), lambda v, pos, vt, ve, lo, hi, na: (layer, ve[v], 0, 0))
    any_spec = pl.BlockSpec(memory_space=pl.ANY)
    return pl.pallas_call(
        functools.partial(_ffn_kernel, n_prompt=n_prompt, n_sample=n_sample, tile=tile),
        out_shape=jax.ShapeDtypeStruct(((rows + tile) * TOKEN_ROWS, LANES), F32),
        grid_spec=pltpu.PrefetchScalarGridSpec(
            num_scalar_prefetch=6,
            grid=(n_visits,),
            in_specs=[any_spec, wspec(d, de), wspec(d, de), wspec(de, d)],
            out_specs=any_spec,
            scratch_shapes=[pltpu.SMEM((n_tab,), I32),
                            pltpu.SMEM((n_tab,), I32),
                            pltpu.VMEM((2, tile * TOKEN_PITCH, LANES), F32),
                            pltpu.VMEM((2, tile * TOKEN_PITCH, LANES), F32),
                            pltpu.VMEM((d, de), BF16),
                            pltpu.VMEM((d, de), BF16),
                            pltpu.VMEM((de, d), BF16),
                            pltpu.SemaphoreType.DMA((2,)),
                            pltpu.SemaphoreType.DMA(())]),
        compiler_params=_cparams("arbitrary"),
        name="moe_ffn",
    )(pos, *plan, x1t, w_gate, w_up, w_down)


def _combine_kernel(x_ref, y0_ref, y1_ref, wt_ref, g_ref, b_ref, o_ref, *maybe_ob_ref):
    tile = o_ref.shape[0]
    wt = wt_ref[...]
    ffn = (wt[:, 0:1] * _load_token_major(y0_ref, tile, TOKEN_ROWS)
           + wt[:, 1:2] * _load_token_major(y1_ref, tile, TOKEN_ROWS))
    x1 = _load_token_major(x_ref, tile, TOKEN_PITCH)
    out = _layer_norm(ALPHA * x1 + ffn, g_ref[...], b_ref[...])
    o_ref[...] = out
    for ob_ref in maybe_ob_ref:
        ob_ref[...] = out.astype(BF16)


def _combine_ln(x1t, y2, wt, ln_g, ln_b, *, row0, m, slots, with_bf16, tile=COMBINE_TILE):
    d = ln_g.shape[1]
    tile = min(tile, m)
    assert row0 % tile == 0 and slots[0] % tile == 0 and slots[1] % tile == 0
    src = lambda base, pitch: pl.BlockSpec((tile * pitch, LANES),
                                           lambda i: (base // tile + i, 0))
    vec = pl.BlockSpec((1, d), lambda i: (0, 0))
    out = pl.BlockSpec((tile, d), lambda i: (i, 0))
    dtypes = (F32, BF16) if with_bf16 else (F32,)
    return pl.pallas_call(
        _combine_kernel,
        out_shape=tuple(jax.ShapeDtypeStruct((m, d), dt) for dt in dtypes),
        grid=(m // tile,),
        in_specs=[src(row0, TOKEN_PITCH), src(slots[0], TOKEN_ROWS), src(slots[1], TOKEN_ROWS),
                  pl.BlockSpec((tile, TOP_K), lambda i: (i, 0)), vec, vec],
        out_specs=tuple(out for _ in dtypes),
        compiler_params=_cparams("arbitrary"),
        name="moe_combine_ln",
    )(x1t, y2, y2, wt, ln_g, ln_b)


def _sortpos_kernel(e_ref, pos_ref, cnt_ref, cum_ref, *, chunk):
    n_chunks = e_ref.shape[1] // chunk
    tri = (lax.broadcasted_iota(I32, (chunk, chunk), 0)
           <= lax.broadcasted_iota(I32, (chunk, chunk), 1)).astype(BF16)
    row = lax.broadcasted_iota(I32, (N_EXPERTS, chunk), 0)
    hot = lambda c: row == e_ref[:, c * chunk:(c + 1) * chunk]
    starts = []
    running = jnp.zeros((N_EXPERTS, LANES), F32)
    for c in range(n_chunks):
        cum = _dot(hot(c).astype(BF16), tri)
        cum_ref[c] = cum
        starts.append(running)
        running = running + cum[:, chunk - 1:chunk]
    cnt_ref[...] = running.astype(I32)
    first_row = _seg_cumsum(running, N_EXPERTS) - running
    for c in range(n_chunks):
        before = (first_row + starts[c])[:, 0:1] + cum_ref[c]
        pos_ref[:, c * chunk:(c + 1) * chunk] = (
            jnp.sum(jnp.where(hot(c), before, 0.0), axis=0, keepdims=True) - 1.0).astype(I32)


def _sort_positions(flat, *, chunk=512):
    n = flat.shape[0]
    n_chunks = pl.cdiv(n, chunk)
    e = jnp.pad(flat, (0, n_chunks * chunk - n), constant_values=N_EXPERTS)
    pos, cnt = pl.pallas_call(
        functools.partial(_sortpos_kernel, chunk=chunk),
        out_shape=(jax.ShapeDtypeStruct((1, n_chunks * chunk), I32),
                   jax.ShapeDtypeStruct((N_EXPERTS, LANES), I32)),
        scratch_shapes=[pltpu.VMEM((n_chunks, N_EXPERTS, chunk), F32)],
        compiler_params=pltpu.CompilerParams(vmem_limit_bytes=VMEM_LIMIT_BYTES),
        name="moe_sort_positions",
    )(e.reshape(1, n_chunks * chunk))
    return pos[0, :n], cnt[:, 0]


def _route_plan(eid, tile, n_prompt):
    n_tok = eid.shape[1]
    pos, counts = _sort_positions(eid.reshape(-1))
    ends = jnp.cumsum(counts)
    offs = ends - counts
    first_tile = offs // tile
    last_tile = (ends - 1) // tile
    nvis = jnp.where(counts > 0, last_tile - first_tile + 1, 0)
    vend = jnp.cumsum(nvis)
    vbase = vend - nvis
    nact = vend[-1]
    n_visits = pl.cdiv(TOP_K * n_tok, tile) + N_EXPERTS - 1
    v = jnp.minimum(jnp.arange(n_visits, dtype=I32), nact - 1)
    e = jnp.sum((vend[None, :] <= v[:, None]).astype(I32), axis=1)
    hot = (e[:, None] == jnp.arange(N_EXPERTS, dtype=I32)[None, :]).astype(I32)
    look = lambda table: jnp.sum(hot * table[None, :].astype(I32), axis=1)
    t = look(first_tile) + (v - look(vbase))
    assert n_prompt % INVERT_UNROLL == 0 and (n_tok - n_prompt) % INVERT_UNROLL == 0
    counts_smem = jnp.stack([nact.astype(I32),
                             jnp.asarray(n_prompt // INVERT_UNROLL, I32),
                             jnp.asarray((n_tok - n_prompt) // INVERT_UNROLL, I32)])
    plan = (t.astype(I32), e.astype(I32), look(offs), look(ends), counts_smem)
    return pos, plan


def kernel(x_prompt, x_sample, state_conv, state_hgrn, cache_k_win, cache_v_win, w_in, w_out, conv_w, hg_lb_param, hg_gain, attn_sinks, ln1_g, ln1_b, w_group, b_group, w_router, b_router, w_gate, w_up, w_down, ln2_g, ln2_b):
    bsz, seq, d = x_prompt.shape
    nb = x_sample.shape[0]
    n_prompt = bsz * seq
    n_tok = n_prompt + nb
    w_buf = cache_k_win.shape[2]

    xp = x_prompt.reshape(n_prompt, d)
    xp_mm = xp
    xs = x_sample.reshape(nb, d)
    pad_rows = ROUTE_ROWS - N_GROUPS - N_EXPERTS
    w_out_b = w_out.astype(BF16)
    outs = {k: [] for k in ("cp", "cs", "hp", "hs", "kp", "ks", "vp", "vs")}
    for l in range(DEPTH):
        gain = hg_gain[l].reshape(1, HG_DK)
        sinks = attn_sinks[l].reshape(1, N_HEADS)
        rw_t = jnp.concatenate([w_group[l].T, w_router[l].T, jnp.zeros((pad_rows, d), F32)], axis=0)
        rb = jnp.concatenate([b_group[l], b_router[l], jnp.zeros((pad_rows,), F32)]).reshape(ROUTE_ROWS, 1)
        g1, b1 = ln1_g[l].reshape(1, d), ln1_b[l].reshape(1, d)
        g2, b2 = ln2_g[l].reshape(1, d), ln2_b[l].reshape(1, d)

        proj = _inproj(xp_mm, w_in, l, tm=2048, tn=512)
        ya, ctail = _conv_prompt(proj, conv_w[l], bsz, seq)
        yb, hstate = _hgrn_prompt(proj, hg_lb_param, gain, l, bsz, seq)
        yc = _attn_prompt(proj, sinks, bsz, seq)
        x1t, eid_p, ewt_p = _outproj_ln_route(
            [ya, yb, yc], xp, w_out_b, l, g1, b1, rw_t, rb, precise=False,
            x1_rows=n_tok, tn=d)
        outs["cp"].append(ctail[:, SUBLANES - (CONV_W - 1):])
        outs["hp"].append(hstate)
        kv = proj.reshape(bsz, seq, IN_DIM)[:, seq - w_buf:]
        outs["kp"].append(kv[:, :, OFF_AK:OFF_AK + KV_DIM].reshape(bsz, w_buf, N_KV_HEADS, HEAD_DIM))
        outs["vp"].append(kv[:, :, OFF_AV:OFF_AV + KV_DIM].reshape(bsz, w_buf, N_KV_HEADS, HEAD_DIM))

        proj_s = _inproj3(xs, w_in, l)
        ysm, cst, hst, kst, vst = _sample_mixers(
            proj_s, state_conv, state_hgrn, cache_k_win, cache_v_win,
            conv_w[l], hg_lb_param, gain, sinks, l)
        x1t, eid_s, ewt_s = _outproj_ln_route(
            [ysm.reshape(nb, d)], xs, w_out, l, g1, b1, rw_t, rb, precise=True,
            x1_rows=n_tok, x1_row0=n_prompt, x1_buf=x1t)
        outs["cs"].append(cst)
        outs["hs"].append(hst)
        outs["ks"].append(kst.reshape(nb, w_buf, N_KV_HEADS, HEAD_DIM))
        outs["vs"].append(vst.reshape(nb, w_buf, N_KV_HEADS, HEAD_DIM))

        eid = jnp.concatenate([eid_p[:TOP_K], eid_s[:TOP_K]], axis=1)
        pos, plan = _route_plan(eid, EXPERT_TILE, n_prompt)
        y2 = _expert_ffn(pos, plan, x1t, w_gate, w_up, w_down, l,
                         n_prompt=n_prompt, n_sample=nb)
        slots = lambda smp: [_slot_base(k, smp, n_prompt, nb) for k in range(TOP_K)]
        xp, *xp_bf16 = _combine_ln(x1t, y2, ewt_p[:TOP_K].T, g2, b2, row0=0, m=n_prompt,
                                   slots=slots(0), with_bf16=l + 1 < DEPTH)
        xp_mm = xp_bf16[0] if xp_bf16 else xp
        xs, = _combine_ln(x1t, y2, ewt_s[:TOP_K].T, g2, b2, row0=n_prompt, m=nb,
                          slots=slots(1), with_bf16=False)

    st = lambda k: jnp.stack(outs[k])
    return (xp.reshape(bsz, seq, d), xs.reshape(nb, 1, d), st("cp"), st("cs"),
            st("hp"), st("hs"), st("kp"), st("ks"), st("vp"), st("vs"))
```

```python
import functools

import jax
import jax.numpy as jnp
import numpy as np
from jax import lax
from jax.experimental import pallas as pl
from jax.experimental.pallas import tpu as pltpu

F32 = jnp.float32
BF16 = jnp.bfloat16
I32 = jnp.int32

D_MODEL = 2048
DEPTH = 2
CONV_DIM = 512
CONV_W = 3
HG_DIM = 512
HG_HEADS = 4
HG_DK = 128
HG_CHUNK = 16
HEAD_DIM = 64
ATTN_DIM = 1024
N_HEADS = 16
N_KV_HEADS = 4
Q_PER_KV = 4
KV_DIM = 256
WINDOW = 128
N_GROUPS = 4
EXPERTS_PER_GROUP = 4
N_EXPERTS = 16
TOP_K = 2
D_EXPERT = 512
ALPHA = (2 * DEPTH) ** 0.25
LOG2E = 1.4426950408889634
LN_EPS = 1e-5
RMS_EPS = 1e-6
IN_DIM = 5120
OFF_CB, OFF_CC, OFF_CH = 0, 512, 1024
OFF_HQ, OFF_HF, OFF_HI, OFF_HG = 1536, 2048, 2560, 3072
OFF_AQ, OFF_AK, OFF_AV = 3584, 4608, 4864

VMEM_LIMIT_BYTES = 56 * 1024 * 1024
LANES = 128
SUBLANES = 8

EXPERT_TILE = 256
FFN_PIECE_COLS = 256
COMBINE_TILE = 512
INVERT_UNROLL = 8
DMA_UNROLL = 8
X_AHEAD = 2
SAMPLE_ROWS = 4
ATTN_BLOCKS = 2
ROUTE_ROWS = 32


TOKEN_ROWS = D_MODEL // LANES
TOKEN_PITCH = 20


def _store_token_major(ref, x, pitch):
    n = x.shape[0]
    for c in range(TOKEN_ROWS):
        ref[pl.ds(c, n, stride=pitch), :] = x[:, c * LANES:(c + 1) * LANES]


def _zero_token_pad(ref, n, pitch):
    for c in range(TOKEN_ROWS, pitch):
        ref[pl.ds(c, n, stride=pitch), :] = jnp.zeros((n, LANES), ref.dtype)


def _load_token_major(ref, n, pitch):
    return jnp.concatenate(
        [ref[pl.ds(c, n, stride=pitch), :] for c in range(TOKEN_ROWS)], axis=1)


def _cparams(*sem):
    return pltpu.CompilerParams(dimension_semantics=sem,
                                vmem_limit_bytes=VMEM_LIMIT_BYTES)


def _dot(a, b):
    return jnp.dot(a, b, preferred_element_type=F32)


def _dot_nt(a, b):
    return lax.dot_general(a, b, (((1,), (1,)), ((), ())),
                           preferred_element_type=F32)


def _split(x):
    hi = x.astype(BF16)
    lo = (x - hi.astype(F32)).astype(BF16)
    return hi, lo


def _dot3(a, b):
    ah, al = _split(a)
    bh, bl = _split(b)
    return _dot(ah, bh) + (_dot(ah, bl) + _dot(al, bh))


def _dot3_nt(a, b):
    ah, al = _split(a)
    bh, bl = _split(b)
    return _dot_nt(ah, bh) + (_dot_nt(ah, bl) + _dot_nt(al, bh))


def _sigmoid(x):
    return 1.0 / (1.0 + jnp.exp(-x))


def _silu(x):
    return x * (0.5 * jnp.tanh(0.5 * x) + 0.5)


def _col_from_row(row):
    n = row.shape[1]
    eye = (lax.broadcasted_iota(I32, (n, n), 0)
           == lax.broadcasted_iota(I32, (n, n), 1))
    return jnp.sum(jnp.where(eye, row, 0.0), axis=1, keepdims=True)


def _hg_lower_bound(lbp, layer):
    m = jnp.max(lbp, axis=0, keepdims=True)
    e = jnp.exp(lbp - m)
    soft = e / jnp.sum(e, axis=0, keepdims=True)
    acc = soft[0:1]
    for i in range(1, layer + 1):
        acc = acc + soft[i:i + 1]
    return acc - soft[0:1]


def _alibi_slope(head):
    return float(2.0 ** (-8.0 * (head + 1) / N_HEADS))


def _inproj_kernel(x_ref, w_ref, o_ref, xb_ref):
    @pl.when(pl.program_id(1) == 0)
    def _():
        xb_ref[...] = x_ref[...].astype(BF16)

    o_ref[...] = _dot(xb_ref[...], w_ref[...].astype(BF16))


def _inproj(x, w, layer, *, tm=1024, tn=1024):
    m, k = x.shape
    n = w.shape[2]
    tm = min(tm, m)
    x_mode = dict(pipeline_mode=pl.Buffered(1)) if x.dtype == F32 else {}
    return pl.pallas_call(
        _inproj_kernel,
        out_shape=jax.ShapeDtypeStruct((m, n), F32),
        grid=(m // tm, n // tn),
        in_specs=[pl.BlockSpec((tm, k), lambda i, j: (i, 0), **x_mode),
                  pl.BlockSpec((None, k, tn), lambda i, j: (layer, 0, j))],
        out_specs=pl.BlockSpec((tm, tn), lambda i, j: (i, j)),
        scratch_shapes=[pltpu.VMEM((tm, k), BF16)],
        compiler_params=_cparams("arbitrary", "arbitrary"),
        name="inproj",
    )(x, w)


def _inproj3_kernel(x_ref, w_ref, o_ref):
    o_ref[...] = _dot3(x_ref[...], w_ref[...])


def _inproj3(x, w, layer, *, tn=512):
    m, k = x.shape
    n = w.shape[2]
    return pl.pallas_call(
        _inproj3_kernel,
        out_shape=jax.ShapeDtypeStruct((m, n), F32),
        grid=(n // tn,),
        in_specs=[pl.BlockSpec((m, k), lambda j: (0, 0)),
                  pl.BlockSpec((None, k, tn), lambda j: (layer, 0, j))],
        out_specs=pl.BlockSpec((m, tn), lambda j: (0, j)),
        compiler_params=_cparams("arbitrary"),
        name="inproj_sample",
    )(x, w)


def _conv_kernel(cb_ref, cc_ref, ch_ref, w_ref, y_ref, tail_ref, carry_ref):
    i = pl.program_id(1)

    @pl.when(i == 0)
    def _():
        carry_ref[...] = jnp.zeros_like(carry_ref)

    u = cc_ref[...] * ch_ref[...]
    tl = u.shape[0]
    row = lax.broadcasted_iota(I32, u.shape, 0)
    prev1 = carry_ref[SUBLANES - 1:SUBLANES, :]
    prev2 = carry_ref[SUBLANES - 2:SUBLANES - 1, :]
    u1 = jnp.where(row == 0, prev1, pltpu.roll(u, 1, 0))
    u2 = jnp.where(row == 0, prev2, jnp.where(row == 1, prev1, pltpu.roll(u, 2, 0)))
    w = w_ref[...]
    y = w[0:1] * u2 + w[1:2] * u1 + w[2:3] * u
    y_ref[...] = (cb_ref[...] * y).astype(y_ref.dtype)
    tail = u[tl - SUBLANES:tl, :]
    carry_ref[...] = tail
    tail_ref[...] = tail


def _conv_prompt(proj, conv_w, bsz, seq, *, tl=512):
    tl = min(tl, seq)
    nt = seq // tl
    cblk = lambda c: pl.BlockSpec((tl, CONV_DIM), lambda b, i, c=c: (b * nt + i, c))
    return pl.pallas_call(
        _conv_kernel,
        out_shape=(jax.ShapeDtypeStruct((bsz * seq, CONV_DIM), BF16),
                   jax.ShapeDtypeStruct((bsz, SUBLANES, CONV_DIM), F32)),
        grid=(bsz, nt),
        in_specs=[cblk(OFF_CB // CONV_DIM), cblk(OFF_CC // CONV_DIM),
                  cblk(OFF_CH // CONV_DIM),
                  pl.BlockSpec((CONV_W, CONV_DIM), lambda b, i: (0, 0))],
        out_specs=(pl.BlockSpec((tl, CONV_DIM), lambda b, i: (b * nt + i, 0)),
                   pl.BlockSpec((None, SUBLANES, CONV_DIM), lambda b, i: (b, 0, 0))),
        scratch_shapes=[pltpu.VMEM((SUBLANES, CONV_DIM), F32)],
        compiler_params=_cparams("arbitrary", "arbitrary"),
        name="conv_prompt",
    )(proj, proj, proj, conv_w)


def _seg_cumsum(g, seg):
    row = lax.broadcasted_iota(I32, g.shape, 0) % seg
    s = 1
    while s < seg:
        g = g + jnp.where(row >= s, pltpu.roll(g, s, 0), 0.0)
        s *= 2
    return g


def _hgrn_kernel(hq_ref, hf_ref, hi_ref, hg_ref, lbp_ref, gain_ref,
                 y_ref, st_ref, cum_ref, k_ref, q_ref, o_ref, s_ref, *, layer):
    i = pl.program_id(1)
    nt = pl.num_programs(1)
    tb = hq_ref.shape[0]
    c = HG_CHUNK

    @pl.when(i == 0)
    def _():
        s_ref[...] = jnp.zeros_like(s_ref)

    lb = _hg_lower_bound(lbp_ref[...], layer)
    f = lb + (1.0 - lb) * _sigmoid(hf_ref[...])
    cum_ref[...] = _seg_cumsum(jnp.log(f) * LOG2E, c)
    k_ref[...] = 1.0 - f
    q_ref[...] = hq_ref[...] * (HG_DK ** -0.5)

    ones = jnp.ones((HG_DK, HG_DK), BF16)
    hc = c // 2
    trow = lax.broadcasted_iota(I32, (hc, HG_DK), 0)

    def chunk(ci, carry):
        r0 = pl.multiple_of(ci * c, c)
        for h in range(HG_HEADS):
            cols = slice(h * HG_DK, (h + 1) * HG_DK)
            cum = cum_ref[pl.ds(r0, c), cols]
            kc = k_ref[pl.ds(r0, c), cols]
            qc = q_ref[pl.ds(r0, c), cols]
            vc = hi_ref[pl.ds(r0, c), cols]
            parts = []
            for s in range(c):
                cs, qk_hi = cum[s:s + 1], qc[hc:] * kc[s:s + 1]
                if s < hc:
                    dlo = jnp.exp2(jnp.where(trow >= s, cum[:hc] - cs, -jnp.inf))
                    parts.append((qc[:hc] * kc[s:s + 1] * dlo).astype(BF16))
                    dhi = jnp.exp2(cum[hc:] - cs)
                else:
                    dhi = jnp.exp2(jnp.where(trow >= s - hc, cum[hc:] - cs, -jnp.inf))
                parts.append((qk_hi * dhi).astype(BF16))
            sc = _dot(jnp.concatenate(parts, axis=0), ones)
            o_lo = jnp.zeros((hc, HG_DK), F32)
            o_hi = jnp.zeros((hc, HG_DK), F32)
            r = 0
            for s in range(c):
                if s < hc:
                    o_lo = o_lo + sc[r:r + hc] * vc[s:s + 1]
                    r += hc
                o_hi = o_hi + sc[r:r + hc] * vc[s:s + 1]
                r += hc
            o = jnp.concatenate([o_lo, o_hi], axis=0)
            st = s_ref[h]
            last = cum[c - 1:c]
            o = o + _dot_nt((qc * jnp.exp2(cum)).astype(BF16), st.astype(BF16))
            kte = (kc * jnp.exp2(last - cum)).astype(BF16)
            du = lax.dot_general(vc.astype(BF16), kte, (((0,), (0,)), ((), ())),
                                 preferred_element_type=F32)
            s_ref[h] = jnp.exp2(last) * st + du
            o_ref[pl.ds(r0, c), cols] = o
        return carry

    lax.fori_loop(0, tb // c, chunk, 0, unroll=4)

    gain = gain_ref[...]
    outs = []
    for h in range(HG_HEADS):
        cols = slice(h * HG_DK, (h + 1) * HG_DK)
        o = o_ref[:, cols]
        o = o * lax.rsqrt(jnp.mean(o * o, axis=-1, keepdims=True) + RMS_EPS) * gain
        outs.append(o * _silu(hg_ref[:, cols]))
    y_ref[...] = jnp.concatenate(outs, axis=-1).astype(y_ref.dtype)

    @pl.when(i == nt - 1)
    def _():
        for h in range(HG_HEADS):
            st_ref[h] = s_ref[h].T


def _hgrn_prompt(proj, lb_param, gain, layer, bsz, seq, *, tb=512):
    tb = min(tb, seq)
    nt = seq // tb
    cblk = lambda c: pl.BlockSpec((tb, HG_DIM), lambda b, i, c=c: (b * nt + i, c))
    return pl.pallas_call(
        functools.partial(_hgrn_kernel, layer=layer),
        out_shape=(jax.ShapeDtypeStruct((bsz * seq, HG_DIM), BF16),
                   jax.ShapeDtypeStruct((bsz, HG_HEADS, HG_DK, HG_DK), F32)),
        grid=(bsz, nt),
        in_specs=[cblk(OFF_HQ // HG_DIM), cblk(OFF_HF // HG_DIM),
                  cblk(OFF_HI // HG_DIM), cblk(OFF_HG // HG_DIM),
                  pl.BlockSpec((DEPTH, HG_DIM), lambda b, i: (0, 0)),
                  pl.BlockSpec((1, HG_DK), lambda b, i: (0, 0))],
        out_specs=(pl.BlockSpec((tb, HG_DIM), lambda b, i: (b * nt + i, 0)),
                   pl.BlockSpec((None, HG_HEADS, HG_DK, HG_DK), lambda b, i: (b, 0, 0, 0))),
        scratch_shapes=[pltpu.VMEM((tb, HG_DIM), F32),
                        pltpu.VMEM((tb, HG_DIM), F32),
                        pltpu.VMEM((tb, HG_DIM), F32),
                        pltpu.VMEM((tb, HG_DIM), F32),
                        pltpu.VMEM((HG_HEADS, HG_DK, HG_DK), F32)],
        compiler_params=_cparams("arbitrary", "arbitrary"),
        name="hgrn_prompt",
    )(proj, proj, proj, proj, lb_param, gain)


def _attn_kernel(qa_ref, qb_ref, kc_ref, vc_ref, kp_ref, vp_ref, sink_ref, y_ref,
                 bias_ref, sinkcol_ref):
    n = pl.program_id(1)
    w = WINDOW
    qi = lax.broadcasted_iota(I32, (Q_PER_KV * w, 2 * w), 0) % w
    kj = lax.broadcasted_iota(I32, (Q_PER_KV * w, 2 * w), 1)

    @pl.when((pl.program_id(0) == 0) & (n == 0))
    def _():
        dist = qi + w - kj
        valid = (dist >= 0) & (dist < w)
        distf = dist.astype(F32)
        for kv in range(N_KV_HEADS):
            grp = lax.broadcasted_iota(I32, (Q_PER_KV * w, 1), 0) // w
            slope = jnp.zeros((Q_PER_KV * w, 1), F32)
            for g in range(Q_PER_KV):
                slope = jnp.where(grp == g, _alibi_slope(kv * Q_PER_KV + g), slope)
            bias = jnp.where(valid, -slope * distf, -jnp.inf)
            bias_ref[1, pl.ds(kv * Q_PER_KV * w, Q_PER_KV * w), :] = bias
            bias_ref[0, pl.ds(kv * Q_PER_KV * w, Q_PER_KV * w), :] = jnp.where(
                kj >= w, bias, -jnp.inf)
        for h in range(N_HEADS):
            sinkcol_ref[pl.ds(h * w, w), :] = jnp.broadcast_to(sink_ref[:, h:h + 1], (w, LANES))

    low = lax.broadcasted_iota(I32, (1, LANES), 1) < HEAD_DIM
    ones = jnp.ones((2 * w, LANES), BF16)
    n_blk = kc_ref.shape[0] // w
    scores, values = [], []
    for j in range(n_blk):
        blk = slice(j * w, (j + 1) * w)
        for kv in range(N_KV_HEADS):
            q_ref = qa_ref if kv < 2 else qb_ref
            qoff = (kv % 2) * Q_PER_KV * HEAD_DIM
            kt = slice((kv // 2) * LANES, (kv // 2 + 1) * LANES)

            def both_halves(prev_ref, cur_ref):
                prev = prev_ref[:, kt] if j == 0 else cur_ref[(j - 1) * w:j * w, kt]
                t = jnp.concatenate([prev, cur_ref[blk, kt]], axis=0)
                r = pltpu.roll(t, HEAD_DIM, 1)
                return (jnp.where(low, t, r) if kv % 2 == 0
                        else jnp.where(low, r, t)).astype(BF16)

            q_parts = []
            for g in range(Q_PER_KV):
                qt = q_ref[blk, qoff + (g // 2) * LANES: qoff + (g // 2 + 1) * LANES]
                q_parts.append(jnp.where(low if g % 2 == 0 else ~low, qt, 0.0))
            q = jnp.concatenate(q_parts, axis=0).astype(BF16)
            scores.append(_dot_nt(q, both_halves(kp_ref, kc_ref)))
            values.append(both_halves(vp_ref, vc_ref))
    bias = jnp.concatenate([bias_ref[jnp.minimum(n, 1)]] + [bias_ref[1]] * (n_blk - 1), axis=0)
    s = jnp.concatenate(scores, axis=0) * (HEAD_DIM ** -0.5) + bias
    sink = jnp.concatenate([sinkcol_ref[...]] * n_blk, axis=0)
    m = jnp.maximum(jnp.max(s, axis=-1, keepdims=True), sink)
    p = jnp.exp(s - jnp.concatenate([m, m], axis=1)).astype(BF16)
    rows = Q_PER_KV * w
    for j in range(n_blk):
        outs = []
        for kv in range(N_KV_HEADS):
            sl = slice((j * N_KV_HEADS + kv) * rows, (j * N_KV_HEADS + kv + 1) * rows)
            pk = p[sl]
            denom = _dot(pk, ones) + jnp.exp(sink[sl] - m[sl])
            o = _dot(pk, values[j * N_KV_HEADS + kv]) / denom
            for g in range(0, Q_PER_KV, 2):
                outs.append(jnp.where(low, o[g * w:(g + 1) * w], o[(g + 1) * w:(g + 2) * w]))
        y_ref[j * w:(j + 1) * w, :] = jnp.concatenate(outs, axis=-1).astype(y_ref.dtype)


def _attn_prompt(proj, sinks, bsz, seq):
    w = WINDOW
    per = ATTN_BLOCKS if seq % (ATTN_BLOCKS * w) == 0 else 1
    nb = seq // (per * w)
    half = ATTN_DIM // 2
    cur = lambda width, off: pl.BlockSpec(
        (per * w, width), lambda b, n: (b * nb + n, off // width))
    prev = lambda width, off: pl.BlockSpec(
        (w, width), lambda b, n: ((b * nb + n) * per - jnp.minimum(n, 1), off // width))
    return pl.pallas_call(
        _attn_kernel,
        out_shape=jax.ShapeDtypeStruct((bsz * seq, ATTN_DIM), BF16),
        grid=(bsz, nb),
        in_specs=[cur(half, OFF_AQ), cur(half, OFF_AQ + half),
                  cur(KV_DIM, OFF_AK), cur(KV_DIM, OFF_AV),
                  prev(KV_DIM, OFF_AK), prev(KV_DIM, OFF_AV),
                  pl.BlockSpec((1, N_HEADS), lambda b, n: (0, 0))],
        out_specs=pl.BlockSpec((per * w, ATTN_DIM), lambda b, n: (b * nb + n, 0)),
        scratch_shapes=[pltpu.VMEM((2, N_HEADS * w, 2 * w), F32),
                        pltpu.VMEM((N_HEADS * w, LANES), F32)],
        compiler_params=_cparams("arbitrary", "arbitrary"),
        name="attn_prompt",
    )(proj, proj, proj, proj, proj, proj, sinks)


def _sample_mix_kernel(p_ref, sc_ref, s0_ref, kc_ref, vc_ref, cw_ref, lbp_ref,
                       gain_ref, sink_ref,
                       y_ref, sco_ref, so_ref, ko_ref, vo_ref, *, layer):
    for r in range(p_ref.shape[0]):
        _sample_mix_one(p_ref.at[r], sc_ref.at[r], s0_ref.at[r], kc_ref.at[r], vc_ref.at[r],
                        cw_ref, lbp_ref, gain_ref, sink_ref,
                        y_ref.at[r], sco_ref.at[r], so_ref.at[r], ko_ref.at[r], vo_ref.at[r],
                        layer)


def _sample_mix_one(p_ref, sc_ref, s0_ref, kc_ref, vc_ref, cw_ref, lbp_ref, gain_ref, sink_ref,
                    y_ref, sco_ref, so_ref, ko_ref, vo_ref, layer):
    p = p_ref[...]
    seg = lambda off, n: p[:, off:off + n]
    u = seg(OFF_CC, CONV_DIM) * seg(OFF_CH, CONV_DIM)
    hist = sc_ref[...]
    cw = cw_ref[...]
    conv = cw[0:1] * hist[0:1] + cw[1:2] * hist[1:2] + cw[2:3] * u
    ya = seg(OFF_CB, CONV_DIM) * conv
    sco_ref[...] = jnp.concatenate([hist[1:2], u], axis=0)
    lb = _hg_lower_bound(lbp_ref[...], layer)
    f = lb + (1.0 - lb) * _sigmoid(seg(OFF_HF, HG_DIM))
    g = jnp.log(f)
    kk = 1.0 - f
    q = seg(OFF_HQ, HG_DIM) * (HG_DK ** -0.5)
    v = seg(OFF_HI, HG_DIM)
    gate = seg(OFF_HG, HG_DIM)
    gain = gain_ref[...]
    yb = []
    for h in range(HG_HEADS):
        cols = slice(h * HG_DK, (h + 1) * HG_DK)
        s0 = s0_ref[h]
        eg = jnp.exp(g[:, cols])
        qe_col = _col_from_row(q[:, cols] * eg)
        o = (jnp.sum(q[:, cols] * kk[:, cols], axis=-1, keepdims=True) * v[:, cols]
             + jnp.sum(qe_col * s0, axis=0, keepdims=True))
        so_ref[h] = _col_from_row(eg) * s0 + _col_from_row(kk[:, cols]) * v[:, cols]
        o = o * lax.rsqrt(jnp.mean(o * o, axis=-1, keepdims=True) + RMS_EPS) * gain
        yb.append(o * _silu(gate[:, cols]))
    w = kc_ref.shape[0]
    kcache = kc_ref[...]
    vcache = vc_ref[...]
    knew = seg(OFF_AK, KV_DIM)
    vnew = seg(OFF_AV, KV_DIM)
    aq = seg(OFF_AQ, ATTN_DIM)
    sinks = sink_ref[...]
    kj = lax.broadcasted_iota(I32, (N_HEADS, w), 1)
    dist = w - kj
    valid = dist < WINDOW
    q_rows = []
    for hd in range(N_HEADS):
        kv = hd // Q_PER_KV
        piece = aq[:, hd * HEAD_DIM:(hd + 1) * HEAD_DIM]
        parts = ([jnp.zeros((1, kv * HEAD_DIM), F32)] if kv else []) + [piece]
        if kv + 1 < N_KV_HEADS:
            parts.append(jnp.zeros((1, (N_KV_HEADS - kv - 1) * HEAD_DIM), F32))
        q_rows.append(jnp.concatenate(parts, axis=1))
    q_all = jnp.concatenate(q_rows, axis=0)
    head = lax.broadcasted_iota(I32, (N_HEADS, 1), 0).astype(F32)
    slope = jnp.exp2(-8.0 * (head + 1.0) / N_HEADS)
    sink = _col_from_row(sinks)
    scale = HEAD_DIM ** -0.5
    sc = _dot3_nt(q_all, kcache) * scale - slope * dist.astype(F32)
    sc = jnp.where(valid, sc, -jnp.inf)
    sn = jnp.sum(q_all * knew, axis=-1, keepdims=True) * scale
    m = jnp.maximum(jnp.maximum(jnp.max(sc, axis=-1, keepdims=True), sn), sink)
    pc = jnp.exp(sc - m)
    pn = jnp.exp(sn - m)
    denom = jnp.sum(pc, axis=-1, keepdims=True) + pn + jnp.exp(sink - m)
    o_all = (_dot3(pc, vcache) + pn * vnew) / denom
    yc = [o_all[hd:hd + 1, (hd // Q_PER_KV) * HEAD_DIM:(hd // Q_PER_KV + 1) * HEAD_DIM]
          for hd in range(N_HEADS)]
    y_ref[...] = jnp.concatenate([ya] + yb + yc, axis=-1)
    row = lax.broadcasted_iota(I32, (w, KV_DIM), 0)
    ko_ref[...] = jnp.where(row == w - 1, knew, pltpu.roll(kcache, w - 1, 0))
    vo_ref[...] = jnp.where(row == w - 1, vnew, pltpu.roll(vcache, w - 1, 0))


def _sample_mixers(proj, state_conv, state_hgrn, cache_k, cache_v, conv_w,
                   lb_param, gain, sinks, layer):
    nb = proj.shape[0]
    w = cache_k.shape[2]
    rows = SAMPLE_ROWS
    assert nb % rows == 0
    per_b = lambda *shape: pl.BlockSpec((rows,) + shape,
                                        lambda b: (b,) + (0,) * len(shape))
    per_lb = lambda *shape: pl.BlockSpec((None, rows) + shape,
                                         lambda b: (layer, b) + (0,) * len(shape))
    whole = lambda *shape: pl.BlockSpec(shape, lambda b: (0,) * len(shape))
    return pl.pallas_call(
        functools.partial(_sample_mix_kernel, layer=layer),
        out_shape=(jax.ShapeDtypeStruct((nb, 1, D_MODEL), F32),
                   jax.ShapeDtypeStruct((nb, CONV_W - 1, CONV_DIM), F32),
                   jax.ShapeDtypeStruct((nb, HG_HEADS, HG_DK, HG_DK), F32),
                   jax.ShapeDtypeStruct((nb, w, KV_DIM), F32),
                   jax.ShapeDtypeStruct((nb, w, KV_DIM), F32)),
        grid=(nb // rows,),
        in_specs=[per_b(1, IN_DIM), per_lb(CONV_W - 1, CONV_DIM),
                  per_lb(HG_HEADS, HG_DK, HG_DK), per_lb(w, KV_DIM), per_lb(w, KV_DIM),
                  whole(CONV_W, CONV_DIM), whole(DEPTH, HG_DIM), whole(1, HG_DK),
                  whole(1, N_HEADS)],
        out_specs=(per_b(1, D_MODEL), per_b(CONV_W - 1, CONV_DIM),
                   per_b(HG_HEADS, HG_DK, HG_DK), per_b(w, KV_DIM), per_b(w, KV_DIM)),
        compiler_params=_cparams("arbitrary"),
        name="sample_mixers",
    )(proj.reshape(nb, 1, IN_DIM), state_conv, state_hgrn,
      cache_k.reshape(DEPTH, nb, w, KV_DIM), cache_v.reshape(DEPTH, nb, w, KV_DIM),
      conv_w, lb_param, gain, sinks)


def _layer_norm(z, g, b):
    mu = jnp.mean(z, axis=-1, keepdims=True)
    zc = z - mu
    var = jnp.mean(zc * zc, axis=-1, keepdims=True)
    return zc * lax.rsqrt(var + LN_EPS) * g + b


def _route(logits, bias):
    lg = logits + bias
    rowv = lambda r: lg[r:r + 1]
    best, gidx = rowv(0), jnp.zeros_like(rowv(0), dtype=I32)
    for r in range(1, N_GROUPS):
        upd = rowv(r) > best
        best = jnp.where(upd, rowv(r), best)
        gidx = jnp.where(upd, r, gidx)
    gden = sum(jnp.exp(rowv(r) - best) for r in range(N_GROUPS))
    gprob = 1.0 / gden
    ev = []
    for j in range(EXPERTS_PER_GROUP):
        val = rowv(N_GROUPS + j)
        for grp in range(1, N_GROUPS):
            val = jnp.where(gidx == grp, rowv(N_GROUPS + grp * EXPERTS_PER_GROUP + j), val)
        ev.append(val)
    v1, j1 = ev[0], jnp.zeros_like(gidx)
    for j in range(1, EXPERTS_PER_GROUP):
        upd = ev[j] > v1
        v1 = jnp.where(upd, ev[j], v1)
        j1 = jnp.where(upd, j, j1)
    v2, j2 = jnp.full_like(v1, -jnp.inf), jnp.zeros_like(gidx)
    for j in range(EXPERTS_PER_GROUP):
        upd = (j1 != j) & (ev[j] > v2)
        v2 = jnp.where(upd, ev[j], v2)
        j2 = jnp.where(upd, j, j2)
    e2 = jnp.exp(v2 - v1)
    w1 = gprob / (1.0 + e2)
    w2 = gprob * e2 / (1.0 + e2)
    base = gidx * EXPERTS_PER_GROUP
    return (jnp.concatenate([base + j1, base + j2], axis=0),
            jnp.concatenate([w1, w2], axis=0))


def _outproj_kernel(*refs, n_y, precise, n_real, n_j):
    y_refs = refs[:n_y]
    x_ref, w_ref, g_ref, b_ref, rw_ref, rb_ref = refs[n_y:n_y + 6]
    x1t_ref, eid_ref, ewt_ref, acc_ref = refs[-4:]
    i = pl.program_id(0)
    j = pl.program_id(1)
    tn = w_ref.shape[1]
    mm = _dot3 if precise else _dot

    def mix():
        y = jnp.concatenate([y_ref[...] for y_ref in y_refs], axis=1)
        return mm(y, w_ref[...])

    if n_j > 1:
        @pl.when(i < n_real)
        def _():
            acc_ref[:, pl.ds(pl.multiple_of(j * tn, tn), tn)] = mix()

    @pl.when((i >= n_real) & (j == n_j - 1))
    def _():
        x1t_ref[...] = jnp.zeros_like(x1t_ref)

    @pl.when((i < n_real) & (j == n_j - 1))
    def _():
        z = ALPHA * x_ref[...] + (acc_ref[...] if n_j > 1 else mix())
        x1 = _layer_norm(z, g_ref[...], b_ref[...])
        _store_token_major(x1t_ref, x1, TOKEN_PITCH)
        _zero_token_pad(x1t_ref, x1.shape[0], TOKEN_PITCH)
        hi, lo = _split(x1)
        rwh, rwl = _split(rw_ref[...])
        both = _dot_nt(jnp.concatenate([rwh, rwl], axis=0), hi)
        logits = both[:ROUTE_ROWS] + (both[ROUTE_ROWS:] + _dot_nt(rwh, lo))
        eid, ewt = _route(logits, rb_ref[...])
        t = eid.shape[1]
        eid_ref[...] = jnp.concatenate([eid, jnp.zeros((SUBLANES - TOP_K, t), I32)], axis=0)
        ewt_ref[...] = jnp.concatenate([ewt, jnp.zeros((SUBLANES - TOP_K, t), F32)], axis=0)


def _outproj_ln_route(ys, x, w_out, layer, ln_g, ln_b, rw_t, rb, *, precise,
                      x1_rows, x1_row0=0, x1_buf=None, tm=512, tn=1024):
    m, d = x.shape
    tm = min(tm, m)
    assert x1_row0 % tm == 0
    n_y = len(ys)
    n_real = m // tm
    n_j = d // tn
    n_i = n_real if x1_buf is not None else pl.cdiv(x1_rows, tm)
    real = lambda i: jnp.minimum(i, n_real - 1)
    y_specs = [pl.BlockSpec((tm, y.shape[1]), lambda i, j: (real(i), 0)) for y in ys]
    vec = lambda: pl.BlockSpec((1, d), lambda i, j: (0, 0))
    operands = list(ys) + [x, w_out, ln_g, ln_b, rw_t, rb]
    in_specs = y_specs + [pl.BlockSpec((tm, d), lambda i, j: (real(i), 0)),
                          pl.BlockSpec((None, d, tn),
                                       lambda i, j: (layer, 0, jnp.where(i < n_real, j, n_j - 1))),
                          vec(), vec(),
                          pl.BlockSpec((ROUTE_ROWS, d), lambda i, j: (0, 0)),
                          pl.BlockSpec((ROUTE_ROWS, 1), lambda i, j: (0, 0))]
    aliases = {}
    if x1_buf is not None:
        aliases = {len(operands): 0}
        operands.append(x1_buf)
        in_specs.append(pl.BlockSpec(memory_space=pl.ANY))
    return pl.pallas_call(
        functools.partial(_outproj_kernel, n_y=n_y, precise=precise, n_real=n_real, n_j=n_j),
        out_shape=(jax.ShapeDtypeStruct((x1_rows * TOKEN_PITCH, LANES), F32),
                   jax.ShapeDtypeStruct((SUBLANES, m), I32),
                   jax.ShapeDtypeStruct((SUBLANES, m), F32)),
        grid=(n_i, n_j),
        in_specs=in_specs,
        out_specs=(pl.BlockSpec((tm * TOKEN_PITCH, LANES), lambda i, j: (x1_row0 // tm + i, 0)),
                   pl.BlockSpec((SUBLANES, tm), lambda i, j: (0, real(i))),
                   pl.BlockSpec((SUBLANES, tm), lambda i, j: (0, real(i)))),
        scratch_shapes=[pltpu.VMEM((tm, d) if n_j > 1 else (SUBLANES, LANES), F32)],
        input_output_aliases=aliases,
        compiler_params=_cparams("arbitrary", "arbitrary"),
        name="outproj_ln_route",
    )(*operands)


def _slot_base(k, is_sample, n_prompt, n_sample):
    return k * n_prompt + is_sample * (TOP_K * n_prompt + k * (n_sample - n_prompt))


def _token_copy(src_ref, src_row, dst_ref, dst_row, sem):
    return pltpu.make_async_copy(src_ref.at[pl.ds(src_row, TOKEN_ROWS)],
                                 dst_ref.at[pl.ds(dst_row, TOKEN_ROWS)], sem)


def _ffn_kernel(pos_ref, vt_ref, ve_ref, lo_ref, hi_ref, nact_ref,
                x_hbm, wg_ref, wu_ref, wd_ref, y_hbm,
                src_ref, xbuf, ybuf, wgb, wub, wdb, semx, semy,
                *, n_prompt, n_sample, tile):
    v = pl.program_id(0)
    nact = nact_ref[0]
    n_tok = n_prompt + n_sample
    rows = TOP_K * n_tok
    n_tab = src_ref.shape[0]
    n_tiles = n_tab // tile
    col = FFN_PIECE_COLS
    d, de = wgb.shape

    def gather_one(t0, b, i, prio):
        _token_copy(x_hbm, src_ref[t0 + i], xbuf.at[b], i * TOKEN_PITCH,
                    semx.at[b]).start(priority=prio)

    def looped(one):
        def body(grp, c):
            for u in range(DMA_UNROLL):
                one(grp * DMA_UNROLL + u, u % 2)
            return c

        lax.fori_loop(0, tile // DMA_UNROLL, body, 0)

    tile_rows = pl.ds(0, tile * TOKEN_ROWS)

    def wait_gather(b):
        pltpu.make_async_copy(x_hbm.at[tile_rows], xbuf.at[b].at[tile_rows], semx.at[b]).wait()

    def tile_out(t, b):
        return pltpu.make_async_copy(
            ybuf.at[b], y_hbm.at[pl.ds(t * (tile * TOKEN_PITCH), tile * TOKEN_PITCH)],
            semy.at[b])

    @pl.when(v == 0)
    def _():
        def invert_segment(k, smp, trips):
            s0 = k * n_tok + smp * n_prompt
            tok0 = smp * n_prompt

            def body(grp, c):
                j0 = grp * INVERT_UNROLL
                src0 = (tok0 + j0) * TOKEN_PITCH
                for u in range(INVERT_UNROLL):
                    src_ref[pos_ref[s0 + j0 + u]] = src0 + u * TOKEN_PITCH
                return c

            lax.fori_loop(0, trips, body, 0)

        for k in range(TOP_K):
            invert_segment(k, 0, nact_ref[1])
            invert_segment(k, 1, nact_ref[2])
        for p in range(rows, n_tab):
            src_ref[p] = 0
        ybuf[...] = jnp.zeros_like(ybuf)
        for t0 in range(min(X_AHEAD, n_tiles)):
            looped(lambda i, prio, t0=t0: gather_one(t0 * tile, t0, i, prio))

    def visit(do_gather):
        t = vt_ref[v]
        b = lax.rem(t, 2)
        first = (v == 0) | (vt_ref[jnp.maximum(v - 1, 0)] != t)
        pending = iter(range(tile))

        def issue(count):
            for _ in range(count):
                i = next(pending, None)
                if i is not None and do_gather:
                    gather_one((t + X_AHEAD) * tile, lax.rem(t + X_AHEAD, X_AHEAD + 1), i, i % 2)

        row = t * tile + lax.broadcasted_iota(I32, (tile, 1), 0)
        keep = first | ((row >= lo_ref[v]) & (row < hi_ref[v]))
        n_pieces = 2 * (de // col) + d // col
        per_piece = pl.cdiv(tile, n_pieces)
        xb = _load_token_major(xbuf.at[lax.rem(t, X_AHEAD + 1)], tile, TOKEN_PITCH).astype(BF16)
        hs = []
        for c in range(de // col):
            g = _dot(xb, wgb[:, c * col:(c + 1) * col])
            issue(per_piece)
            u = _dot(xb, wub[:, c * col:(c + 1) * col])
            issue(per_piece)
            hs.append((_silu(g) * u).astype(BF16))
        hb = jnp.concatenate(hs, axis=1)
        yb = ybuf.at[b]
        for c in range(d // col):
            y = _dot(hb, wdb[:, c * col:(c + 1) * col])
            for r in range(col // LANES):
                rows_r = pl.ds(c * (col // LANES) + r, tile, stride=TOKEN_PITCH)
                yb[rows_r, :] = jnp.where(keep, y[:, r * LANES:(r + 1) * LANES], yb[rows_r, :])
            issue(per_piece)
        issue(tile)

    @pl.when(v < nact)
    def _():
        t = vt_ref[v]
        b = lax.rem(t, 2)
        final = v == nact - 1
        first = (v == 0) | (vt_ref[jnp.maximum(v - 1, 0)] != t)
        last = final | (vt_ref[jnp.minimum(v + 1, nact - 1)] != t)

        @pl.when((v == 0) | (ve_ref[v] != ve_ref[jnp.maximum(v - 1, 0)]))
        def _():
            wgb[...] = wg_ref[...].astype(BF16)
            wub[...] = wu_ref[...].astype(BF16)
            wdb[...] = wd_ref[...].astype(BF16)

        @pl.when(first)
        def _():
            wait_gather(lax.rem(t, X_AHEAD + 1))

        @pl.when(first & (t >= 2))
        def _():
            tile_out(t - 2, b).wait()

        want_gather = last & (t + X_AHEAD < n_tiles)
        pl.when(want_gather)(functools.partial(visit, True))
        pl.when(jnp.logical_not(want_gather))(functools.partial(visit, False))

        @pl.when(last)
        def _():
            tile_out(t, b).start()

        @pl.when(final)
        def _():
            tile_out(t, b).wait()

            @pl.when(t >= 1)
            def _():
                tile_out(t - 1, 1 - b).wait()


def _expert_ffn(pos, plan, x1t, w_gate, w_up, w_down, layer, *, n_prompt, n_sample,
                tile=EXPERT_TILE):
    d, de = w_gate.shape[2], w_gate.shape[3]
    n_visits = plan[0].shape[0]
    rows = TOP_K * (n_prompt + n_sample)
    assert n_prompt + n_sample >= tile
    n_tab = pl.cdiv(rows, tile) * tile
    wspec = lambda r, c: pl.BlockSpec(
        (None, None, r, c), lambda v, pos, vt, ve, lo, hi, na: (layer, ve[v], 0, 0))
    any_spec = pl.BlockSpec(memory_space=pl.ANY)
    return pl.pallas_call(
        functools.partial(_ffn_kernel, n_prompt=n_prompt, n_sample=n_sample, tile=tile),
        out_shape=jax.ShapeDtypeStruct((n_tab * TOKEN_PITCH, LANES), F32),
        grid_spec=pltpu.PrefetchScalarGridSpec(
            num_scalar_prefetch=6,
            grid=(n_visits,),
            in_specs=[any_spec, wspec(d, de), wspec(d, de), wspec(de, d)],
            out_specs=any_spec,
            scratch_shapes=[pltpu.SMEM((n_tab,), I32),
                            pltpu.VMEM((X_AHEAD + 1, tile * TOKEN_PITCH, LANES), F32),
                            pltpu.VMEM((2, tile * TOKEN_PITCH, LANES), F32),
                            pltpu.VMEM((d, de), BF16),
                            pltpu.VMEM((d, de), BF16),
                            pltpu.VMEM((de, d), BF16),
                            pltpu.SemaphoreType.DMA((X_AHEAD + 1,)),
                            pltpu.SemaphoreType.DMA((2,))]),
        compiler_params=_cparams("arbitrary"),
        name="moe_ffn",
    )(pos, *plan, x1t, w_gate, w_up, w_down)


def _combine_kernel(pos_ref, x_ref, y_hbm, wt_ref, g_ref, b_ref, o_ref, *rest, row0, n_tok):
    *maybe_ob_ref, ybuf, sem = rest
    i = pl.program_id(0)
    tile = o_ref.shape[0]
    b = lax.rem(i, 2)
    tile_rows = pl.ds(0, tile * TOKEN_ROWS)

    def gather_one(step, slot, r):
        for k in range(TOP_K):
            p = pos_ref[k * n_tok + row0 + step * tile + r]
            _token_copy(y_hbm, p * TOKEN_PITCH, ybuf.at[slot, k], r * TOKEN_PITCH,
                        sem.at[slot]).start(priority=k)

    def wait(slot):
        for k in range(TOP_K):
            pltpu.make_async_copy(y_hbm.at[tile_rows], ybuf.at[slot, k].at[tile_rows],
                                  sem.at[slot]).wait()

    @pl.when(i == 0)
    def _():
        def body(grp, c):
            for u in range(DMA_UNROLL):
                gather_one(0, 0, grp * DMA_UNROLL + u)
            return c

        lax.fori_loop(0, tile // DMA_UNROLL, body, 0)

    wait(b)
    nxt = jnp.minimum(i + 1, pl.num_programs(0) - 1)
    pending = iter(range(tile))

    def issue(count):
        for _ in range(count):
            r = next(pending, None)
            if r is not None:
                gather_one(nxt, 1 - b, r)

    per_piece = pl.cdiv(tile, 3 * TOKEN_ROWS)
    w0 = _col_from_row(wt_ref[0:1, :])
    w1 = _col_from_row(wt_ref[1:2, :])
    chunk = lambda ref, c, pitch: ref[pl.ds(c, tile, stride=pitch), :]
    z = []
    for c in range(TOKEN_ROWS):
        z.append(ALPHA * chunk(x_ref, c, TOKEN_PITCH)
                 + w0 * chunk(ybuf.at[b, 0], c, TOKEN_PITCH)
                 + w1 * chunk(ybuf.at[b, 1], c, TOKEN_PITCH))
        issue(per_piece)
    d = TOKEN_ROWS * LANES
    total = z[0]
    for c in range(1, TOKEN_ROWS):
        total = total + z[c]
    mu = jnp.sum(total, axis=-1, keepdims=True) * (1.0 / d)
    sq = None
    for c in range(TOKEN_ROWS):
        z[c] = z[c] - mu
        sq = z[c] * z[c] if sq is None else sq + z[c] * z[c]
        issue(per_piece)
    rstd = lax.rsqrt(jnp.sum(sq, axis=-1, keepdims=True) * (1.0 / d) + LN_EPS)
    for c in range(TOKEN_ROWS):
        cols = slice(c * LANES, (c + 1) * LANES)
        out = z[c] * rstd * g_ref[:, cols] + b_ref[:, cols]
        o_ref[:, cols] = out
        for ob_ref in maybe_ob_ref:
            ob_ref[:, cols] = out.astype(BF16)
        issue(per_piece)
    issue(tile)

    @pl.when(i == pl.num_programs(0) - 1)
    def _():
        wait(1 - b)


def _combine_ln(pos, x1t, y_sorted, wt, ln_g, ln_b, *, row0, m, n_tok, with_bf16,
                tile=COMBINE_TILE):
    d = ln_g.shape[1]
    tile = min(tile, m)
    assert row0 % tile == 0
    vec = pl.BlockSpec((1, d), lambda i, pos: (0, 0))
    out = pl.BlockSpec((tile, d), lambda i, pos: (i, 0))
    dtypes = (F32, BF16) if with_bf16 else (F32,)
    return pl.pallas_call(
        functools.partial(_combine_kernel, row0=row0, n_tok=n_tok),
        out_shape=tuple(jax.ShapeDtypeStruct((m, d), dt) for dt in dtypes),
        grid_spec=pltpu.PrefetchScalarGridSpec(
            num_scalar_prefetch=1,
            grid=(m // tile,),
            in_specs=[pl.BlockSpec((tile * TOKEN_PITCH, LANES),
                                   lambda i, pos: (row0 // tile + i, 0)),
                      pl.BlockSpec(memory_space=pl.ANY),
                      pl.BlockSpec((SUBLANES, tile), lambda i, pos: (0, i)), vec, vec],
            out_specs=tuple(out for _ in dtypes),
            scratch_shapes=[pltpu.VMEM((2, TOP_K, tile * TOKEN_PITCH, LANES), F32),
                            pltpu.SemaphoreType.DMA((2,))]),
        compiler_params=_cparams("arbitrary"),
        name="moe_combine_ln",
    )(pos, x1t, y_sorted, wt, ln_g, ln_b)


def _sortpos_kernel(e_ref, pos_ref, cnt_ref, cum_ref, *, chunk):
    n_chunks = e_ref.shape[1] // chunk
    tri = (lax.broadcasted_iota(I32, (chunk, chunk), 0)
           <= lax.broadcasted_iota(I32, (chunk, chunk), 1)).astype(BF16)
    row = lax.broadcasted_iota(I32, (N_EXPERTS, chunk), 0)
    hot = lambda c: row == e_ref[:, c * chunk:(c + 1) * chunk]
    starts = []
    running = jnp.zeros((N_EXPERTS, LANES), F32)
    for c in range(n_chunks):
        cum = _dot(hot(c).astype(BF16), tri)
        cum_ref[c] = cum
        starts.append(running)
        running = running + cum[:, chunk - 1:chunk]
    cnt_ref[...] = running.astype(I32)
    first_row = _seg_cumsum(running, N_EXPERTS) - running
    for c in range(n_chunks):
        before = (first_row + starts[c])[:, 0:1] + cum_ref[c]
        pos_ref[:, c * chunk:(c + 1) * chunk] = (
            jnp.sum(jnp.where(hot(c), before, 0.0), axis=0, keepdims=True) - 1.0).astype(I32)


def _sort_positions(flat, *, chunk=512):
    n = flat.shape[0]
    n_chunks = pl.cdiv(n, chunk)
    e = jnp.pad(flat, (0, n_chunks * chunk - n), constant_values=N_EXPERTS)
    pos, cnt = pl.pallas_call(
        functools.partial(_sortpos_kernel, chunk=chunk),
        out_shape=(jax.ShapeDtypeStruct((1, n_chunks * chunk), I32),
                   jax.ShapeDtypeStruct((N_EXPERTS, LANES), I32)),
        scratch_shapes=[pltpu.VMEM((n_chunks, N_EXPERTS, chunk), F32)],
        compiler_params=pltpu.CompilerParams(vmem_limit_bytes=VMEM_LIMIT_BYTES),
        name="moe_sort_positions",
    )(e.reshape(1, n_chunks * chunk))
    return pos[0, :n], cnt[:, 0]


def _route_plan(eid, tile, n_prompt):
    n_tok = eid.shape[1]
    pos, counts = _sort_positions(eid.reshape(-1))
    ends = jnp.cumsum(counts)
    offs = ends - counts
    first_tile = offs // tile
    last_tile = (ends - 1) // tile
    nvis = jnp.where(counts > 0, last_tile - first_tile + 1, 0)
    vend = jnp.cumsum(nvis)
    vbase = vend - nvis
    nact = vend[-1]
    n_visits = pl.cdiv(TOP_K * n_tok, tile) + N_EXPERTS - 1
    v = jnp.minimum(jnp.arange(n_visits, dtype=I32), nact - 1)
    e = jnp.sum((vend[None, :] <= v[:, None]).astype(I32), axis=1)
    hot = (e[:, None] == jnp.arange(N_EXPERTS, dtype=I32)[None, :]).astype(I32)
    look = lambda table: jnp.sum(hot * table[None, :].astype(I32), axis=1)
    t = look(first_tile) + (v - look(vbase))
    assert n_prompt % INVERT_UNROLL == 0 and (n_tok - n_prompt) % INVERT_UNROLL == 0
    counts_smem = jnp.stack([nact.astype(I32),
                             jnp.asarray(n_prompt // INVERT_UNROLL, I32),
                             jnp.asarray((n_tok - n_prompt) // INVERT_UNROLL, I32)])
    plan = (t.astype(I32), e.astype(I32), look(offs), look(ends), counts_smem)
    return pos, plan


def kernel(x_prompt, x_sample, state_conv, state_hgrn, cache_k_win, cache_v_win, w_in, w_out, conv_w, hg_lb_param, hg_gain, attn_sinks, ln1_g, ln1_b, w_group, b_group, w_router, b_router, w_gate, w_up, w_down, ln2_g, ln2_b):
    bsz, seq, d = x_prompt.shape
    nb = x_sample.shape[0]
    n_prompt = bsz * seq
    n_tok = n_prompt + nb
    w_buf = cache_k_win.shape[2]

    xp = x_prompt.reshape(n_prompt, d)
    xp_mm = xp
    xs = x_sample.reshape(nb, d)
    pad_rows = ROUTE_ROWS - N_GROUPS - N_EXPERTS
    w_out_b = w_out.astype(BF16)
    outs = {k: [] for k in ("cp", "cs", "hp", "hs", "kp", "ks", "vp", "vs")}
    for l in range(DEPTH):
        gain = hg_gain[l].reshape(1, HG_DK)
        sinks = attn_sinks[l].reshape(1, N_HEADS)
        rw_t = jnp.concatenate([w_group[l].T, w_router[l].T, jnp.zeros((pad_rows, d), F32)], axis=0)
        rb = jnp.concatenate([b_group[l], b_router[l], jnp.zeros((pad_rows,), F32)]).reshape(ROUTE_ROWS, 1)
        g1, b1 = ln1_g[l].reshape(1, d), ln1_b[l].reshape(1, d)
        g2, b2 = ln2_g[l].reshape(1, d), ln2_b[l].reshape(1, d)

        proj = _inproj(xp_mm, w_in, l, tm=2048, tn=512)
        ya, ctail = _conv_prompt(proj, conv_w[l], bsz, seq)
        yb, hstate = _hgrn_prompt(proj, hg_lb_param, gain, l, bsz, seq)
        yc = _attn_prompt(proj, sinks, bsz, seq)
        x1t, eid_p, ewt_p = _outproj_ln_route(
            [ya, yb, yc], xp, w_out_b, l, g1, b1, rw_t, rb, precise=False,
            x1_rows=n_tok, tn=d)
        outs["cp"].append(ctail[:, SUBLANES - (CONV_W - 1):])
        outs["hp"].append(hstate)
        kv = proj.reshape(bsz, seq, IN_DIM)[:, seq - w_buf:]
        outs["kp"].append(kv[:, :, OFF_AK:OFF_AK + KV_DIM].reshape(bsz, w_buf, N_KV_HEADS, HEAD_DIM))
        outs["vp"].append(kv[:, :, OFF_AV:OFF_AV + KV_DIM].reshape(bsz, w_buf, N_KV_HEADS, HEAD_DIM))

        proj_s = _inproj3(xs, w_in, l)
        ysm, cst, hst, kst, vst = _sample_mixers(
            proj_s, state_conv, state_hgrn, cache_k_win, cache_v_win,
            conv_w[l], hg_lb_param, gain, sinks, l)
        x1t, eid_s, ewt_s = _outproj_ln_route(
            [ysm.reshape(nb, d)], xs, w_out, l, g1, b1, rw_t, rb, precise=True,
            x1_rows=n_tok, x1_row0=n_prompt, x1_buf=x1t)
        outs["cs"].append(cst)
        outs["hs"].append(hst)
        outs["ks"].append(kst.reshape(nb, w_buf, N_KV_HEADS, HEAD_DIM))
        outs["vs"].append(vst.reshape(nb, w_buf, N_KV_HEADS, HEAD_DIM))

        eid = jnp.concatenate([eid_p[:TOP_K], eid_s[:TOP_K]], axis=1)
        pos, plan = _route_plan(eid, EXPERT_TILE, n_prompt)
        y_sorted = _expert_ffn(pos, plan, x1t, w_gate, w_up, w_down, l,
                               n_prompt=n_prompt, n_sample=nb)
        xp, *xp_bf16 = _combine_ln(pos, x1t, y_sorted, ewt_p, g2, b2, row0=0, m=n_prompt,
                                   n_tok=n_tok, with_bf16=l + 1 < DEPTH)
        xp_mm = xp_bf16[0] if xp_bf16 else xp
        xs, = _combine_ln(pos, x1t, y_sorted, ewt_s, g2, b2, row0=n_prompt, m=nb,
                          n_tok=n_tok, with_bf16=False)

    st = lambda k: jnp.stack(outs[k])
    return (xp.reshape(bsz, seq, d), xs.reshape(nb, 1, d), st("cp"), st("cs"),
            st("hp"), st("hs"), st("kp"), st("ks"), st("vp"), st("vs"))
```

```python
import functools

import jax
import jax.numpy as jnp
import numpy as np
from jax import lax
from jax.experimental import pallas as pl
from jax.experimental.pallas import tpu as pltpu

F32 = jnp.float32
BF16 = jnp.bfloat16
I32 = jnp.int32

D_MODEL = 2048
DEPTH = 2
CONV_DIM = 512
CONV_W = 3
HG_DIM = 512
HG_HEADS = 4
HG_DK = 128
HG_CHUNK = 16
HEAD_DIM = 64
ATTN_DIM = 1024
N_HEADS = 16
N_KV_HEADS = 4
Q_PER_KV = 4
KV_DIM = 256
WINDOW = 128
N_GROUPS = 4
EXPERTS_PER_GROUP = 4
N_EXPERTS = 16
TOP_K = 2
D_EXPERT = 512
ALPHA = (2 * DEPTH) ** 0.25
LOG2E = 1.4426950408889634
LN_EPS = 1e-5
RMS_EPS = 1e-6
IN_DIM = 5120
OFF_CB, OFF_CC, OFF_CH = 0, 512, 1024
OFF_HQ, OFF_HF, OFF_HI, OFF_HG = 1536, 2048, 2560, 3072
OFF_AQ, OFF_AK, OFF_AV = 3584, 4608, 4864

VMEM_LIMIT_BYTES = 56 * 1024 * 1024
LANES = 128
SUBLANES = 8

EXPERT_TILE = 256
FFN_PIECE_COLS = 256
COMBINE_TILE = 512
INVERT_UNROLL = 8
DMA_UNROLL = 8
X_AHEAD = 2
SAMPLE_ROWS = 4
ATTN_BLOCKS = 4
ROUTE_ROWS = 32


TOKEN_ROWS = D_MODEL // LANES
TOKEN_PITCH = 20


def _store_token_major(ref, x, pitch):
    n = x.shape[0]
    for c in range(TOKEN_ROWS):
        ref[pl.ds(c, n, stride=pitch), :] = x[:, c * LANES:(c + 1) * LANES]


def _zero_token_pad(ref, n, pitch):
    for c in range(TOKEN_ROWS, pitch):
        ref[pl.ds(c, n, stride=pitch), :] = jnp.zeros((n, LANES), ref.dtype)


def _load_token_major(ref, n, pitch):
    return jnp.concatenate(
        [ref[pl.ds(c, n, stride=pitch), :] for c in range(TOKEN_ROWS)], axis=1)


def _cparams(*sem):
    return pltpu.CompilerParams(dimension_semantics=sem,
                                vmem_limit_bytes=VMEM_LIMIT_BYTES)


def _dot(a, b):
    return jnp.dot(a, b, preferred_element_type=F32)


def _dot_nt(a, b):
    return lax.dot_general(a, b, (((1,), (1,)), ((), ())),
                           preferred_element_type=F32)


def _split(x):
    hi = x.astype(BF16)
    lo = (x - hi.astype(F32)).astype(BF16)
    return hi, lo


def _dot3(a, b):
    ah, al = _split(a)
    bh, bl = _split(b)
    return _dot(ah, bh) + (_dot(ah, bl) + _dot(al, bh))


def _dot3_nt(a, b):
    ah, al = _split(a)
    bh, bl = _split(b)
    return _dot_nt(ah, bh) + (_dot_nt(ah, bl) + _dot_nt(al, bh))


def _sigmoid(x):
    return 1.0 / (1.0 + jnp.exp(-x))


def _silu(x):
    return x * (0.5 * jnp.tanh(0.5 * x) + 0.5)


def _col_from_row(row):
    n = row.shape[1]
    eye = (lax.broadcasted_iota(I32, (n, n), 0)
           == lax.broadcasted_iota(I32, (n, n), 1))
    return jnp.sum(jnp.where(eye, row, 0.0), axis=1, keepdims=True)


def _hg_lower_bound(lbp, layer):
    m = jnp.max(lbp, axis=0, keepdims=True)
    e = jnp.exp(lbp - m)
    soft = e / jnp.sum(e, axis=0, keepdims=True)
    acc = soft[0:1]
    for i in range(1, layer + 1):
        acc = acc + soft[i:i + 1]
    return acc - soft[0:1]


def _alibi_slope(head):
    return float(2.0 ** (-8.0 * (head + 1) / N_HEADS))


def _inproj_kernel(x_ref, w_ref, o_ref, xb_ref):
    @pl.when(pl.program_id(1) == 0)
    def _():
        xb_ref[...] = x_ref[...].astype(BF16)

    o_ref[...] = _dot(xb_ref[...], w_ref[...].astype(BF16))


def _inproj(x, w, layer, *, tm=1024, tn=1024):
    m, k = x.shape
    n = w.shape[2]
    tm = min(tm, m)
    x_mode = dict(pipeline_mode=pl.Buffered(1)) if x.dtype == F32 else {}
    return pl.pallas_call(
        _inproj_kernel,
        out_shape=jax.ShapeDtypeStruct((m, n), F32),
        grid=(m // tm, n // tn),
        in_specs=[pl.BlockSpec((tm, k), lambda i, j: (i, 0), **x_mode),
                  pl.BlockSpec((None, k, tn), lambda i, j: (layer, 0, j))],
        out_specs=pl.BlockSpec((tm, tn), lambda i, j: (i, j)),
        scratch_shapes=[pltpu.VMEM((tm, k), BF16)],
        compiler_params=_cparams("arbitrary", "arbitrary"),
        name="inproj",
    )(x, w)


def _inproj3_kernel(x_ref, w_ref, o_ref):
    o_ref[...] = _dot3(x_ref[...], w_ref[...])


def _inproj3(x, w, layer, *, tn=512):
    m, k = x.shape
    n = w.shape[2]
    return pl.pallas_call(
        _inproj3_kernel,
        out_shape=jax.ShapeDtypeStruct((m, n), F32),
        grid=(n // tn,),
        in_specs=[pl.BlockSpec((m, k), lambda j: (0, 0)),
                  pl.BlockSpec((None, k, tn), lambda j: (layer, 0, j))],
        out_specs=pl.BlockSpec((m, tn), lambda j: (0, j)),
        compiler_params=_cparams("arbitrary"),
        name="inproj_sample",
    )(x, w)


def _conv_kernel(cb_ref, cc_ref, ch_ref, w_ref, y_ref, tail_ref, carry_ref):
    i = pl.program_id(1)

    @pl.when(i == 0)
    def _():
        carry_ref[...] = jnp.zeros_like(carry_ref)

    u = cc_ref[...] * ch_ref[...]
    tl = u.shape[0]
    row = lax.broadcasted_iota(I32, u.shape, 0)
    prev1 = carry_ref[SUBLANES - 1:SUBLANES, :]
    prev2 = carry_ref[SUBLANES - 2:SUBLANES - 1, :]
    u1 = jnp.where(row == 0, prev1, pltpu.roll(u, 1, 0))
    u2 = jnp.where(row == 0, prev2, jnp.where(row == 1, prev1, pltpu.roll(u, 2, 0)))
    w = w_ref[...]
    y = w[0:1] * u2 + w[1:2] * u1 + w[2:3] * u
    y_ref[...] = (cb_ref[...] * y).astype(y_ref.dtype)
    tail = u[tl - SUBLANES:tl, :]
    carry_ref[...] = tail
    tail_ref[...] = tail


def _conv_prompt(proj, conv_w, bsz, seq, *, tl=512):
    tl = min(tl, seq)
    nt = seq // tl
    cblk = lambda c: pl.BlockSpec((tl, CONV_DIM), lambda b, i, c=c: (b * nt + i, c))
    return pl.pallas_call(
        _conv_kernel,
        out_shape=(jax.ShapeDtypeStruct((bsz * seq, CONV_DIM), BF16),
                   jax.ShapeDtypeStruct((bsz, SUBLANES, CONV_DIM), F32)),
        grid=(bsz, nt),
        in_specs=[cblk(OFF_CB // CONV_DIM), cblk(OFF_CC // CONV_DIM),
                  cblk(OFF_CH // CONV_DIM),
                  pl.BlockSpec((CONV_W, CONV_DIM), lambda b, i: (0, 0))],
        out_specs=(pl.BlockSpec((tl, CONV_DIM), lambda b, i: (b * nt + i, 0)),
                   pl.BlockSpec((None, SUBLANES, CONV_DIM), lambda b, i: (b, 0, 0))),
        scratch_shapes=[pltpu.VMEM((SUBLANES, CONV_DIM), F32)],
        compiler_params=_cparams("arbitrary", "arbitrary"),
        name="conv_prompt",
    )(proj, proj, proj, conv_w)


def _seg_cumsum(g, seg):
    row = lax.broadcasted_iota(I32, g.shape, 0) % seg
    s = 1
    while s < seg:
        g = g + jnp.where(row >= s, pltpu.roll(g, s, 0), 0.0)
        s *= 2
    return g


def _hgrn_kernel(hq_ref, hf_ref, hi_ref, hg_ref, lbp_ref, gain_ref,
                 y_ref, st_ref, cum_ref, k_ref, q_ref, o_ref, s_ref, *, layer):
    i = pl.program_id(1)
    nt = pl.num_programs(1)
    tb = hq_ref.shape[0]
    c = HG_CHUNK

    @pl.when(i == 0)
    def _():
        s_ref[...] = jnp.zeros_like(s_ref)

    lb = _hg_lower_bound(lbp_ref[...], layer)
    f = lb + (1.0 - lb) * _sigmoid(hf_ref[...])
    cum_ref[...] = _seg_cumsum(jnp.log(f) * LOG2E, c)
    k_ref[...] = 1.0 - f
    q_ref[...] = hq_ref[...] * (HG_DK ** -0.5)

    ones = jnp.ones((HG_DK, HG_DK), BF16)
    hc = c // 2
    trow = lax.broadcasted_iota(I32, (hc, HG_DK), 0)

    def chunk(ci, carry):
        r0 = pl.multiple_of(ci * c, c)
        for h in range(HG_HEADS):
            cols = slice(h * HG_DK, (h + 1) * HG_DK)
            cum = cum_ref[pl.ds(r0, c), cols]
            kc = k_ref[pl.ds(r0, c), cols]
            qc = q_ref[pl.ds(r0, c), cols]
            vc = hi_ref[pl.ds(r0, c), cols]
            parts = []
            for s in range(c):
                cs, qk_hi = cum[s:s + 1], qc[hc:] * kc[s:s + 1]
                if s < hc:
                    dlo = jnp.exp2(jnp.where(trow >= s, cum[:hc] - cs, -jnp.inf))
                    parts.append((qc[:hc] * kc[s:s + 1] * dlo).astype(BF16))
                    dhi = jnp.exp2(cum[hc:] - cs)
                else:
                    dhi = jnp.exp2(jnp.where(trow >= s - hc, cum[hc:] - cs, -jnp.inf))
                parts.append((qk_hi * dhi).astype(BF16))
            sc = _dot(jnp.concatenate(parts, axis=0), ones)
            o_lo = jnp.zeros((hc, HG_DK), F32)
            o_hi = jnp.zeros((hc, HG_DK), F32)
            r = 0
            for s in range(c):
                if s < hc:
                    o_lo = o_lo + sc[r:r + hc] * vc[s:s + 1]
                    r += hc
                o_hi = o_hi + sc[r:r + hc] * vc[s:s + 1]
                r += hc
            o = jnp.concatenate([o_lo, o_hi], axis=0)
            st = s_ref[h]
            last = cum[c - 1:c]
            o = o + _dot_nt((qc * jnp.exp2(cum)).astype(BF16), st.astype(BF16))
            kte = (kc * jnp.exp2(last - cum)).astype(BF16)
            du = lax.dot_general(vc.astype(BF16), kte, (((0,), (0,)), ((), ())),
                                 preferred_element_type=F32)
            s_ref[h] = jnp.exp2(last) * st + du
            o_ref[pl.ds(r0, c), cols] = o
        return carry

    lax.fori_loop(0, tb // c, chunk, 0, unroll=16)

    gain = gain_ref[...]
    outs = []
    for h in range(HG_HEADS):
        cols = slice(h * HG_DK, (h + 1) * HG_DK)
        o = o_ref[:, cols]
        o = o * lax.rsqrt(jnp.mean(o * o, axis=-1, keepdims=True) + RMS_EPS) * gain
        outs.append(o * _silu(hg_ref[:, cols]))
    y_ref[...] = jnp.concatenate(outs, axis=-1).astype(y_ref.dtype)

    @pl.when(i == nt - 1)
    def _():
        for h in range(HG_HEADS):
            st_ref[h] = s_ref[h].T


def _hgrn_prompt(proj, lb_param, gain, layer, bsz, seq, *, tb=512):
    tb = min(tb, seq)
    nt = seq // tb
    cblk = lambda c: pl.BlockSpec((tb, HG_DIM), lambda b, i, c=c: (b * nt + i, c))
    return pl.pallas_call(
        functools.partial(_hgrn_kernel, layer=layer),
        out_shape=(jax.ShapeDtypeStruct((bsz * seq, HG_DIM), BF16),
                   jax.ShapeDtypeStruct((bsz, HG_HEADS, HG_DK, HG_DK), F32)),
        grid=(bsz, nt),
        in_specs=[cblk(OFF_HQ // HG_DIM), cblk(OFF_HF // HG_DIM),
                  cblk(OFF_HI // HG_DIM), cblk(OFF_HG // HG_DIM),
                  pl.BlockSpec((DEPTH, HG_DIM), lambda b, i: (0, 0)),
                  pl.BlockSpec((1, HG_DK), lambda b, i: (0, 0))],
        out_specs=(pl.BlockSpec((tb, HG_DIM), lambda b, i: (b * nt + i, 0)),
                   pl.BlockSpec((None, HG_HEADS, HG_DK, HG_DK), lambda b, i: (b, 0, 0, 0))),
        scratch_shapes=[pltpu.VMEM((tb, HG_DIM), F32),
                        pltpu.VMEM((tb, HG_DIM), F32),
                        pltpu.VMEM((tb, HG_DIM), F32),
                        pltpu.VMEM((tb, HG_DIM), F32),
                        pltpu.VMEM((HG_HEADS, HG_DK, HG_DK), F32)],
        compiler_params=_cparams("arbitrary", "arbitrary"),
        name="hgrn_prompt",
    )(proj, proj, proj, proj, lb_param, gain)


def _attn_kernel(qa_ref, qb_ref, kc_ref, vc_ref, kp_ref, vp_ref, sink_ref, y_ref,
                 bias_ref, sinkcol_ref):
    n = pl.program_id(1)
    w = WINDOW
    qi = lax.broadcasted_iota(I32, (Q_PER_KV * w, 2 * w), 0) % w
    kj = lax.broadcasted_iota(I32, (Q_PER_KV * w, 2 * w), 1)

    @pl.when((pl.program_id(0) == 0) & (n == 0))
    def _():
        dist = qi + w - kj
        valid = (dist >= 0) & (dist < w)
        distf = dist.astype(F32)
        for kv in range(N_KV_HEADS):
            grp = lax.broadcasted_iota(I32, (Q_PER_KV * w, 1), 0) // w
            slope = jnp.zeros((Q_PER_KV * w, 1), F32)
            for g in range(Q_PER_KV):
                slope = jnp.where(grp == g, _alibi_slope(kv * Q_PER_KV + g), slope)
            bias = jnp.where(valid, -slope * distf, -jnp.inf)
            bias_ref[1, pl.ds(kv * Q_PER_KV * w, Q_PER_KV * w), :] = bias
            bias_ref[0, pl.ds(kv * Q_PER_KV * w, Q_PER_KV * w), :] = jnp.where(
                kj >= w, bias, -jnp.inf)
        for h in range(N_HEADS):
            sinkcol_ref[pl.ds(h * w, w), :] = jnp.broadcast_to(sink_ref[:, h:h + 1], (w, LANES))

    low = lax.broadcasted_iota(I32, (1, LANES), 1) < HEAD_DIM
    ones = jnp.ones((2 * w, LANES), BF16)
    n_blk = kc_ref.shape[0] // w
    scores, values = [], []
    for j in range(n_blk):
        blk = slice(j * w, (j + 1) * w)
        for kv in range(N_KV_HEADS):
            q_ref = qa_ref if kv < 2 else qb_ref
            qoff = (kv % 2) * Q_PER_KV * HEAD_DIM
            kt = slice((kv // 2) * LANES, (kv // 2 + 1) * LANES)

            def both_halves(prev_ref, cur_ref):
                prev = prev_ref[:, kt] if j == 0 else cur_ref[(j - 1) * w:j * w, kt]
                t = jnp.concatenate([prev, cur_ref[blk, kt]], axis=0)
                r = pltpu.roll(t, HEAD_DIM, 1)
                return (jnp.where(low, t, r) if kv % 2 == 0
                        else jnp.where(low, r, t)).astype(BF16)

            q_parts = []
            for g in range(Q_PER_KV):
                qt = q_ref[blk, qoff + (g // 2) * LANES: qoff + (g // 2 + 1) * LANES]
                q_parts.append(jnp.where(low if g % 2 == 0 else ~low, qt, 0.0))
            q = jnp.concatenate(q_parts, axis=0).astype(BF16)
            scores.append(_dot_nt(q, both_halves(kp_ref, kc_ref)))
            values.append(both_halves(vp_ref, vc_ref))
    bias = jnp.concatenate([bias_ref[jnp.minimum(n, 1)]] + [bias_ref[1]] * (n_blk - 1), axis=0)
    s = jnp.concatenate(scores, axis=0) * (HEAD_DIM ** -0.5) + bias
    sink = jnp.concatenate([sinkcol_ref[...]] * n_blk, axis=0)
    m = jnp.maximum(jnp.max(s, axis=-1, keepdims=True), sink)
    p = jnp.exp(s - jnp.concatenate([m, m], axis=1)).astype(BF16)
    rows = Q_PER_KV * w
    for j in range(n_blk):
        outs = []
        for kv in range(N_KV_HEADS):
            sl = slice((j * N_KV_HEADS + kv) * rows, (j * N_KV_HEADS + kv + 1) * rows)
            pk = p[sl]
            denom = _dot(pk, ones) + jnp.exp(sink[sl] - m[sl])
            o = _dot(pk, values[j * N_KV_HEADS + kv]) / denom
            for g in range(0, Q_PER_KV, 2):
                outs.append(jnp.where(low, o[g * w:(g + 1) * w], o[(g + 1) * w:(g + 2) * w]))
        y_ref[j * w:(j + 1) * w, :] = jnp.concatenate(outs, axis=-1).astype(y_ref.dtype)


def _attn_prompt(proj, sinks, bsz, seq):
    w = WINDOW
    per = ATTN_BLOCKS if seq % (ATTN_BLOCKS * w) == 0 else 1
    nb = seq // (per * w)
    half = ATTN_DIM // 2
    cur = lambda width, off: pl.BlockSpec(
        (per * w, width), lambda b, n: (b * nb + n, off // width))
    prev = lambda width, off: pl.BlockSpec(
        (w, width), lambda b, n: ((b * nb + n) * per - jnp.minimum(n, 1), off // width))
    return pl.pallas_call(
        _attn_kernel,
        out_shape=jax.ShapeDtypeStruct((bsz * seq, ATTN_DIM), BF16),
        grid=(bsz, nb),
        in_specs=[cur(half, OFF_AQ), cur(half, OFF_AQ + half),
                  cur(KV_DIM, OFF_AK), cur(KV_DIM, OFF_AV),
                  prev(KV_DIM, OFF_AK), prev(KV_DIM, OFF_AV),
                  pl.BlockSpec((1, N_HEADS), lambda b, n: (0, 0))],
        out_specs=pl.BlockSpec((per * w, ATTN_DIM), lambda b, n: (b * nb + n, 0)),
        scratch_shapes=[pltpu.VMEM((2, N_HEADS * w, 2 * w), F32),
                        pltpu.VMEM((N_HEADS * w, LANES), F32)],
        compiler_params=_cparams("arbitrary", "arbitrary"),
        name="attn_prompt",
    )(proj, proj, proj, proj, proj, proj, sinks)


def _sample_mix_kernel(p_ref, sc_ref, s0_ref, kc_ref, vc_ref, cw_ref, lbp_ref,
                       gain_ref, sink_ref,
                       y_ref, sco_ref, so_ref, ko_ref, vo_ref, *, layer):
    for r in range(p_ref.shape[0]):
        _sample_mix_one(p_ref.at[r], sc_ref.at[r], s0_ref.at[r], kc_ref.at[r], vc_ref.at[r],
                        cw_ref, lbp_ref, gain_ref, sink_ref,
                        y_ref.at[r], sco_ref.at[r], so_ref.at[r], ko_ref.at[r], vo_ref.at[r],
                        layer)


def _sample_mix_one(p_ref, sc_ref, s0_ref, kc_ref, vc_ref, cw_ref, lbp_ref, gain_ref, sink_ref,
                    y_ref, sco_ref, so_ref, ko_ref, vo_ref, layer):
    p = p_ref[...]
    seg = lambda off, n: p[:, off:off + n]
    u = seg(OFF_CC, CONV_DIM) * seg(OFF_CH, CONV_DIM)
    hist = sc_ref[...]
    cw = cw_ref[...]
    conv = cw[0:1] * hist[0:1] + cw[1:2] * hist[1:2] + cw[2:3] * u
    ya = seg(OFF_CB, CONV_DIM) * conv
    sco_ref[...] = jnp.concatenate([hist[1:2], u], axis=0)
    lb = _hg_lower_bound(lbp_ref[...], layer)
    f = lb + (1.0 - lb) * _sigmoid(seg(OFF_HF, HG_DIM))
    g = jnp.log(f)
    kk = 1.0 - f
    q = seg(OFF_HQ, HG_DIM) * (HG_DK ** -0.5)
    v = seg(OFF_HI, HG_DIM)
    gate = seg(OFF_HG, HG_DIM)
    gain = gain_ref[...]
    yb = []
    for h in range(HG_HEADS):
        cols = slice(h * HG_DK, (h + 1) * HG_DK)
        s0 = s0_ref[h]
        eg = jnp.exp(g[:, cols])
        qe_col = _col_from_row(q[:, cols] * eg)
        o = (jnp.sum(q[:, cols] * kk[:, cols], axis=-1, keepdims=True) * v[:, cols]
             + jnp.sum(qe_col * s0, axis=0, keepdims=True))
        so_ref[h] = _col_from_row(eg) * s0 + _col_from_row(kk[:, cols]) * v[:, cols]
        o = o * lax.rsqrt(jnp.mean(o * o, axis=-1, keepdims=True) + RMS_EPS) * gain
        yb.append(o * _silu(gate[:, cols]))
    w = kc_ref.shape[0]
    kcache = kc_ref[...]
    vcache = vc_ref[...]
    knew = seg(OFF_AK, KV_DIM)
    vnew = seg(OFF_AV, KV_DIM)
    aq = seg(OFF_AQ, ATTN_DIM)
    sinks = sink_ref[...]
    kj = lax.broadcasted_iota(I32, (N_HEADS, w), 1)
    dist = w - kj
    valid = dist < WINDOW
    q_rows = []
    for hd in range(N_HEADS):
        kv = hd // Q_PER_KV
        piece = aq[:, hd * HEAD_DIM:(hd + 1) * HEAD_DIM]
        parts = ([jnp.zeros((1, kv * HEAD_DIM), F32)] if kv else []) + [piece]
        if kv + 1 < N_KV_HEADS:
            parts.append(jnp.zeros((1, (N_KV_HEADS - kv - 1) * HEAD_DIM), F32))
        q_rows.append(jnp.concatenate(parts, axis=1))
    q_all = jnp.concatenate(q_rows, axis=0)
    head = lax.broadcasted_iota(I32, (N_HEADS, 1), 0).astype(F32)
    slope = jnp.exp2(-8.0 * (head + 1.0) / N_HEADS)
    sink = _col_from_row(sinks)
    scale = HEAD_DIM ** -0.5
    sc = _dot3_nt(q_all, kcache) * scale - slope * dist.astype(F32)
    sc = jnp.where(valid, sc, -jnp.inf)
    sn = jnp.sum(q_all * knew, axis=-1, keepdims=True) * scale
    m = jnp.maximum(jnp.maximum(jnp.max(sc, axis=-1, keepdims=True), sn), sink)
    pc = jnp.exp(sc - m)
    pn = jnp.exp(sn - m)
    denom = jnp.sum(pc, axis=-1, keepdims=True) + pn + jnp.exp(sink - m)
    o_all = (_dot3(pc, vcache) + pn * vnew) / denom
    yc = [o_all[hd:hd + 1, (hd // Q_PER_KV) * HEAD_DIM:(hd // Q_PER_KV + 1) * HEAD_DIM]
          for hd in range(N_HEADS)]
    y_ref[...] = jnp.concatenate([ya] + yb + yc, axis=-1)
    row = lax.broadcasted_iota(I32, (w, KV_DIM), 0)
    ko_ref[...] = jnp.where(row == w - 1, knew, pltpu.roll(kcache, w - 1, 0))
    vo_ref[...] = jnp.where(row == w - 1, vnew, pltpu.roll(vcache, w - 1, 0))


def _sample_mixers(proj, state_conv, state_hgrn, cache_k, cache_v, conv_w,
                   lb_param, gain, sinks, layer):
    nb = proj.shape[0]
    w = cache_k.shape[2]
    rows = SAMPLE_ROWS
    assert nb % rows == 0
    per_b = lambda *shape: pl.BlockSpec((rows,) + shape,
                                        lambda b: (b,) + (0,) * len(shape))
    per_lb = lambda *shape: pl.BlockSpec((None, rows) + shape,
                                         lambda b: (layer, b) + (0,) * len(shape))
    whole = lambda *shape: pl.BlockSpec(shape, lambda b: (0,) * len(shape))
    return pl.pallas_call(
        functools.partial(_sample_mix_kernel, layer=layer),
        out_shape=(jax.ShapeDtypeStruct((nb, 1, D_MODEL), F32),
                   jax.ShapeDtypeStruct((nb, CONV_W - 1, CONV_DIM), F32),
                   jax.ShapeDtypeStruct((nb, HG_HEADS, HG_DK, HG_DK), F32),
                   jax.ShapeDtypeStruct((nb, w, KV_DIM), F32),
                   jax.ShapeDtypeStruct((nb, w, KV_DIM), F32)),
        grid=(nb // rows,),
        in_specs=[per_b(1, IN_DIM), per_lb(CONV_W - 1, CONV_DIM),
                  per_lb(HG_HEADS, HG_DK, HG_DK), per_lb(w, KV_DIM), per_lb(w, KV_DIM),
                  whole(CONV_W, CONV_DIM), whole(DEPTH, HG_DIM), whole(1, HG_DK),
                  whole(1, N_HEADS)],
        out_specs=(per_b(1, D_MODEL), per_b(CONV_W - 1, CONV_DIM),
                   per_b(HG_HEADS, HG_DK, HG_DK), per_b(w, KV_DIM), per_b(w, KV_DIM)),
        compiler_params=_cparams("arbitrary"),
        name="sample_mixers",
    )(proj.reshape(nb, 1, IN_DIM), state_conv, state_hgrn,
      cache_k.reshape(DEPTH, nb, w, KV_DIM), cache_v.reshape(DEPTH, nb, w, KV_DIM),
      conv_w, lb_param, gain, sinks)


def _layer_norm(z, g, b):
    mu = jnp.mean(z, axis=-1, keepdims=True)
    zc = z - mu
    var = jnp.mean(zc * zc, axis=-1, keepdims=True)
    return zc * lax.rsqrt(var + LN_EPS) * g + b


def _route(logits, bias):
    lg = logits + bias
    rowv = lambda r: lg[r:r + 1]
    best, gidx = rowv(0), jnp.zeros_like(rowv(0), dtype=I32)
    for r in range(1, N_GROUPS):
        upd = rowv(r) > best
        best = jnp.where(upd, rowv(r), best)
        gidx = jnp.where(upd, r, gidx)
    gden = sum(jnp.exp(rowv(r) - best) for r in range(N_GROUPS))
    gprob = 1.0 / gden
    ev = []
    for j in range(EXPERTS_PER_GROUP):
        val = rowv(N_GROUPS + j)
        for grp in range(1, N_GROUPS):
            val = jnp.where(gidx == grp, rowv(N_GROUPS + grp * EXPERTS_PER_GROUP + j), val)
        ev.append(val)
    v1, j1 = ev[0], jnp.zeros_like(gidx)
    for j in range(1, EXPERTS_PER_GROUP):
        upd = ev[j] > v1
        v1 = jnp.where(upd, ev[j], v1)
        j1 = jnp.where(upd, j, j1)
    v2, j2 = jnp.full_like(v1, -jnp.inf), jnp.zeros_like(gidx)
    for j in range(EXPERTS_PER_GROUP):
        upd = (j1 != j) & (ev[j] > v2)
        v2 = jnp.where(upd, ev[j], v2)
        j2 = jnp.where(upd, j, j2)
    e2 = jnp.exp(v2 - v1)
    w1 = gprob / (1.0 + e2)
    w2 = gprob * e2 / (1.0 + e2)
    base = gidx * EXPERTS_PER_GROUP
    return (jnp.concatenate([base + j1, base + j2], axis=0),
            jnp.concatenate([w1, w2], axis=0))


def _outproj_kernel(*refs, n_y, precise, n_real, n_j):
    y_refs = refs[:n_y]
    x_ref, w_ref, g_ref, b_ref, rw_ref, rb_ref = refs[n_y:n_y + 6]
    x1t_ref, eid_ref, ewt_ref, acc_ref = refs[-4:]
    i = pl.program_id(0)
    j = pl.program_id(1)
    tn = w_ref.shape[1]
    mm = _dot3 if precise else _dot

    def mix():
        y = jnp.concatenate([y_ref[...] for y_ref in y_refs], axis=1)
        return mm(y, w_ref[...])

    if n_j > 1:
        @pl.when(i < n_real)
        def _():
            acc_ref[:, pl.ds(pl.multiple_of(j * tn, tn), tn)] = mix()

    @pl.when((i >= n_real) & (j == n_j - 1))
    def _():
        x1t_ref[...] = jnp.zeros_like(x1t_ref)

    @pl.when((i < n_real) & (j == n_j - 1))
    def _():
        z = ALPHA * x_ref[...] + (acc_ref[...] if n_j > 1 else mix())
        x1 = _layer_norm(z, g_ref[...], b_ref[...])
        _store_token_major(x1t_ref, x1, TOKEN_PITCH)
        _zero_token_pad(x1t_ref, x1.shape[0], TOKEN_PITCH)
        hi, lo = _split(x1)
        rwh, rwl = _split(rw_ref[...])
        both = _dot_nt(jnp.concatenate([rwh, rwl], axis=0), hi)
        logits = both[:ROUTE_ROWS] + (both[ROUTE_ROWS:] + _dot_nt(rwh, lo))
        eid, ewt = _route(logits, rb_ref[...])
        t = eid.shape[1]
        eid_ref[...] = jnp.concatenate([eid, jnp.zeros((SUBLANES - TOP_K, t), I32)], axis=0)
        ewt_ref[...] = jnp.concatenate([ewt, jnp.zeros((SUBLANES - TOP_K, t), F32)], axis=0)


def _outproj_ln_route(ys, x, w_out, layer, ln_g, ln_b, rw_t, rb, *, precise,
                      x1_rows, x1_row0=0, x1_buf=None, tm=512, tn=1024):
    m, d = x.shape
    tm = min(tm, m)
    assert x1_row0 % tm == 0
    n_y = len(ys)
    n_real = m // tm
    n_j = d // tn
    n_i = n_real if x1_buf is not None else pl.cdiv(x1_rows, tm)
    real = lambda i: jnp.minimum(i, n_real - 1)
    y_specs = [pl.BlockSpec((tm, y.shape[1]), lambda i, j: (real(i), 0)) for y in ys]
    vec = lambda: pl.BlockSpec((1, d), lambda i, j: (0, 0))
    operands = list(ys) + [x, w_out, ln_g, ln_b, rw_t, rb]
    in_specs = y_specs + [pl.BlockSpec((tm, d), lambda i, j: (real(i), 0)),
                          pl.BlockSpec((None, d, tn),
                                       lambda i, j: (layer, 0, jnp.where(i < n_real, j, n_j - 1))),
                          vec(), vec(),
                          pl.BlockSpec((ROUTE_ROWS, d), lambda i, j: (0, 0)),
                          pl.BlockSpec((ROUTE_ROWS, 1), lambda i, j: (0, 0))]
    aliases = {}
    if x1_buf is not None:
        aliases = {len(operands): 0}
        operands.append(x1_buf)
        in_specs.append(pl.BlockSpec(memory_space=pl.ANY))
    return pl.pallas_call(
        functools.partial(_outproj_kernel, n_y=n_y, precise=precise, n_real=n_real, n_j=n_j),
        out_shape=(jax.ShapeDtypeStruct((x1_rows * TOKEN_PITCH, LANES), F32),
                   jax.ShapeDtypeStruct((SUBLANES, m), I32),
                   jax.ShapeDtypeStruct((SUBLANES, m), F32)),
        grid=(n_i, n_j),
        in_specs=in_specs,
        out_specs=(pl.BlockSpec((tm * TOKEN_PITCH, LANES), lambda i, j: (x1_row0 // tm + i, 0)),
                   pl.BlockSpec((SUBLANES, tm), lambda i, j: (0, real(i))),
                   pl.BlockSpec((SUBLANES, tm), lambda i, j: (0, real(i)))),
        scratch_shapes=[pltpu.VMEM((tm, d) if n_j > 1 else (SUBLANES, LANES), F32)],
        input_output_aliases=aliases,
        compiler_params=_cparams("arbitrary", "arbitrary"),
        name="outproj_ln_route",
    )(*operands)


def _slot_base(k, is_sample, n_prompt, n_sample):
    return k * n_prompt + is_sample * (TOP_K * n_prompt + k * (n_sample - n_prompt))


def _token_copy(src_ref, src_row, dst_ref, dst_row, sem):
    return pltpu.make_async_copy(src_ref.at[pl.ds(src_row, TOKEN_ROWS)],
                                 dst_ref.at[pl.ds(dst_row, TOKEN_ROWS)], sem)


def _ffn_kernel(pos_ref, vt_ref, ve_ref, lo_ref, hi_ref, nact_ref,
                x_hbm, wg_ref, wu_ref, wd_ref, y_hbm,
                src_ref, xbuf, ybuf, wgb, wub, wdb, semx, semy,
                *, n_prompt, n_sample, tile):
    v = pl.program_id(0)
    nact = nact_ref[0]
    n_tok = n_prompt + n_sample
    rows = TOP_K * n_tok
    n_tab = src_ref.shape[0]
    n_tiles = n_tab // tile
    col = FFN_PIECE_COLS
    d, de = wgb.shape

    def gather_one(t0, b, i, prio):
        _token_copy(x_hbm, src_ref[t0 + i], xbuf.at[b], i * TOKEN_PITCH,
                    semx.at[b]).start(priority=prio)

    def looped(one):
        def body(grp, c):
            for u in range(DMA_UNROLL):
                one(grp * DMA_UNROLL + u, u % 2)
            return c

        lax.fori_loop(0, tile // DMA_UNROLL, body, 0)

    tile_rows = pl.ds(0, tile * TOKEN_ROWS)

    def wait_gather(b):
        pltpu.make_async_copy(x_hbm.at[tile_rows], xbuf.at[b].at[tile_rows], semx.at[b]).wait()

    def tile_out(t, b):
        return pltpu.make_async_copy(
            ybuf.at[b], y_hbm.at[pl.ds(t * (tile * TOKEN_PITCH), tile * TOKEN_PITCH)],
            semy.at[b])

    @pl.when(v == 0)
    def _():
        def invert_segment(k, smp, trips):
            s0 = k * n_tok + smp * n_prompt
            tok0 = smp * n_prompt

            def body(grp, c):
                j0 = grp * INVERT_UNROLL
                src0 = (tok0 + j0) * TOKEN_PITCH
                for u in range(INVERT_UNROLL):
                    src_ref[pos_ref[s0 + j0 + u]] = src0 + u * TOKEN_PITCH
                return c

            lax.fori_loop(0, trips, body, 0)

        for k in range(TOP_K):
            invert_segment(k, 0, nact_ref[1])
            invert_segment(k, 1, nact_ref[2])
        for p in range(rows, n_tab):
            src_ref[p] = 0
        ybuf[...] = jnp.zeros_like(ybuf)
        for t0 in range(min(X_AHEAD, n_tiles)):
            looped(lambda i, prio, t0=t0: gather_one(t0 * tile, t0, i, prio))

    def visit(do_gather):
        t = vt_ref[v]
        b = lax.rem(t, 2)
        first = (v == 0) | (vt_ref[jnp.maximum(v - 1, 0)] != t)
        pending = iter(range(tile))

        def issue(count):
            for _ in range(count):
                i = next(pending, None)
                if i is not None and do_gather:
                    gather_one((t + X_AHEAD) * tile, lax.rem(t + X_AHEAD, X_AHEAD + 1), i, i % 2)

        row = t * tile + lax.broadcasted_iota(I32, (tile, 1), 0)
        keep = first | ((row >= lo_ref[v]) & (row < hi_ref[v]))
        n_pieces = 2 * (de // col) + d // col
        per_piece = pl.cdiv(tile, n_pieces)
        xb = _load_token_major(xbuf.at[lax.rem(t, X_AHEAD + 1)], tile, TOKEN_PITCH).astype(BF16)
        hs = []
        for c in range(de // col):
            g = _dot(xb, wgb[:, c * col:(c + 1) * col])
            issue(per_piece)
            u = _dot(xb, wub[:, c * col:(c + 1) * col])
            issue(per_piece)
            hs.append((_silu(g) * u).astype(BF16))
        hb = jnp.concatenate(hs, axis=1)
        yb = ybuf.at[b]
        for c in range(d // col):
            y = _dot(hb, wdb[:, c * col:(c + 1) * col])
            for r in range(col // LANES):
                rows_r = pl.ds(c * (col // LANES) + r, tile, stride=TOKEN_PITCH)
                yb[rows_r, :] = jnp.where(keep, y[:, r * LANES:(r + 1) * LANES], yb[rows_r, :])
            issue(per_piece)
        issue(tile)

    @pl.when(v < nact)
    def _():
        t = vt_ref[v]
        b = lax.rem(t, 2)
        final = v == nact - 1
        first = (v == 0) | (vt_ref[jnp.maximum(v - 1, 0)] != t)
        last = final | (vt_ref[jnp.minimum(v + 1, nact - 1)] != t)

        @pl.when((v == 0) | (ve_ref[v] != ve_ref[jnp.maximum(v - 1, 0)]))
        def _():
            wgb[...] = wg_ref[...].astype(BF16)
            wub[...] = wu_ref[...].astype(BF16)
            wdb[...] = wd_ref[...].astype(BF16)

        @pl.when(first)
        def _():
            wait_gather(lax.rem(t, X_AHEAD + 1))

        @pl.when(first & (t >= 2))
        def _():
            tile_out(t - 2, b).wait()

        want_gather = last & (t + X_AHEAD < n_tiles)
        pl.when(want_gather)(functools.partial(visit, True))
        pl.when(jnp.logical_not(want_gather))(functools.partial(visit, False))

        @pl.when(last)
        def _():
            tile_out(t, b).start()

        @pl.when(final)
        def _():
            tile_out(t, b).wait()

            @pl.when(t >= 1)
            def _():
                tile_out(t - 1, 1 - b).wait()


def _expert_ffn(pos, plan, x1t, w_gate, w_up, w_down, layer, *, n_prompt, n_sample,
                tile=EXPERT_TILE):
    d, de = w_gate.shape[2], w_gate.shape[3]
    n_visits = plan[0].shape[0]
    rows = TOP_K * (n_prompt + n_sample)
    assert n_prompt + n_sample >= tile
    n_tab = pl.cdiv(rows, tile) * tile
    wspec = lambda r, c: pl.BlockSpec(
        (None, None, r, c), lambda v, pos, vt, ve, lo, hi, na: (layer, ve[v], 0, 0))
    any_spec = pl.BlockSpec(memory_space=pl.ANY)
    return pl.pallas_call(
        functools.partial(_ffn_kernel, n_prompt=n_prompt, n_sample=n_sample, tile=tile),
        out_shape=jax.ShapeDtypeStruct((n_tab * TOKEN_PITCH, LANES), F32),
        grid_spec=pltpu.PrefetchScalarGridSpec(
            num_scalar_prefetch=6,
            grid=(n_visits,),
            in_specs=[any_spec, wspec(d, de), wspec(d, de), wspec(de, d)],
            out_specs=any_spec,
            scratch_shapes=[pltpu.SMEM((n_tab,), I32),
                            pltpu.VMEM((X_AHEAD + 1, tile * TOKEN_PITCH, LANES), F32),
                            pltpu.VMEM((2, tile * TOKEN_PITCH, LANES), F32),
                            pltpu.VMEM((d, de), BF16),
                            pltpu.VMEM((d, de), BF16),
                            pltpu.VMEM((de, d), BF16),
                            pltpu.SemaphoreType.DMA((X_AHEAD + 1,)),
                            pltpu.SemaphoreType.DMA((2,))]),
        compiler_params=_cparams("arbitrary"),
        name="moe_ffn",
    )(pos, *plan, x1t, w_gate, w_up, w_down)


def _combine_kernel(pos_ref, x_ref, y_hbm, wt_ref, g_ref, b_ref, o_ref, *rest, row0, n_tok):
    *maybe_ob_ref, ybuf, sem = rest
    i = pl.program_id(0)
    tile = o_ref.shape[0]
    b = lax.rem(i, 2)
    tile_rows = pl.ds(0, tile * TOKEN_ROWS)

    def gather_one(step, slot, r):
        for k in range(TOP_K):
            p = pos_ref[k * n_tok + row0 + step * tile + r]
            _token_copy(y_hbm, p * TOKEN_PITCH, ybuf.at[slot, k], r * TOKEN_PITCH,
                        sem.at[slot]).start(priority=k)

    def wait(slot):
        for k in range(TOP_K):
            pltpu.make_async_copy(y_hbm.at[tile_rows], ybuf.at[slot, k].at[tile_rows],
                                  sem.at[slot]).wait()

    @pl.when(i == 0)
    def _():
        def body(grp, c):
            for u in range(DMA_UNROLL):
                gather_one(0, 0, grp * DMA_UNROLL + u)
            return c

        lax.fori_loop(0, tile // DMA_UNROLL, body, 0)

    wait(b)
    nxt = jnp.minimum(i + 1, pl.num_programs(0) - 1)
    pending = iter(range(tile))

    def issue(count):
        for _ in range(count):
            r = next(pending, None)
            if r is not None:
                gather_one(nxt, 1 - b, r)

    per_piece = pl.cdiv(tile, 3 * TOKEN_ROWS)
    w0 = _col_from_row(wt_ref[0:1, :])
    w1 = _col_from_row(wt_ref[1:2, :])
    chunk = lambda ref, c, pitch: ref[pl.ds(c, tile, stride=pitch), :]
    z = []
    for c in range(TOKEN_ROWS):
        z.append(ALPHA * chunk(x_ref, c, TOKEN_PITCH)
                 + w0 * chunk(ybuf.at[b, 0], c, TOKEN_PITCH)
                 + w1 * chunk(ybuf.at[b, 1], c, TOKEN_PITCH))
        issue(per_piece)
    d = TOKEN_ROWS * LANES
    total = z[0]
    for c in range(1, TOKEN_ROWS):
        total = total + z[c]
    mu = jnp.sum(total, axis=-1, keepdims=True) * (1.0 / d)
    sq = None
    for c in range(TOKEN_ROWS):
        z[c] = z[c] - mu
        sq = z[c] * z[c] if sq is None else sq + z[c] * z[c]
        issue(per_piece)
    rstd = lax.rsqrt(jnp.sum(sq, axis=-1, keepdims=True) * (1.0 / d) + LN_EPS)
    for c in range(TOKEN_ROWS):
        cols = slice(c * LANES, (c + 1) * LANES)
        out = z[c] * rstd * g_ref[:, cols] + b_ref[:, cols]
        o_ref[:, cols] = out
        for ob_ref in maybe_ob_ref:
            ob_ref[:, cols] = out.astype(BF16)
        issue(per_piece)
    issue(tile)

    @pl.when(i == pl.num_programs(0) - 1)
    def _():
        wait(1 - b)


def _combine_ln(pos, x1t, y_sorted, wt, ln_g, ln_b, *, row0, m, n_tok, with_bf16,
                tile=COMBINE_TILE):
    d = ln_g.shape[1]
    tile = min(tile, m)
    assert row0 % tile == 0
    vec = pl.BlockSpec((1, d), lambda i, pos: (0, 0))
    out = pl.BlockSpec((tile, d), lambda i, pos: (i, 0))
    dtypes = (F32, BF16) if with_bf16 else (F32,)
    return pl.pallas_call(
        functools.partial(_combine_kernel, row0=row0, n_tok=n_tok),
        out_shape=tuple(jax.ShapeDtypeStruct((m, d), dt) for dt in dtypes),
        grid_spec=pltpu.PrefetchScalarGridSpec(
            num_scalar_prefetch=1,
            grid=(m // tile,),
            in_specs=[pl.BlockSpec((tile * TOKEN_PITCH, LANES),
                                   lambda i, pos: (row0 // tile + i, 0)),
                      pl.BlockSpec(memory_space=pl.ANY),
                      pl.BlockSpec((SUBLANES, tile), lambda i, pos: (0, i)), vec, vec],
            out_specs=tuple(out for _ in dtypes),
            scratch_shapes=[pltpu.VMEM((2, TOP_K, tile * TOKEN_PITCH, LANES), F32),
                            pltpu.SemaphoreType.DMA((2,))]),
        compiler_params=_cparams("arbitrary"),
        name="moe_combine_ln",
    )(pos, x1t, y_sorted, wt, ln_g, ln_b)


def _sortpos_kernel(e_ref, pos_ref, cnt_ref, cum_ref, *, chunk):
    n_chunks = e_ref.shape[1] // chunk
    tri = (lax.broadcasted_iota(I32, (chunk, chunk), 0)
           <= lax.broadcasted_iota(I32, (chunk, chunk), 1)).astype(BF16)
    row = lax.broadcasted_iota(I32, (N_EXPERTS, chunk), 0)
    hot = lambda c: row == e_ref[:, c * chunk:(c + 1) * chunk]
    starts = []
    running = jnp.zeros((N_EXPERTS, LANES), F32)
    for c in range(n_chunks):
        cum = _dot(hot(c).astype(BF16), tri)
        cum_ref[c] = cum
        starts.append(running)
        running = running + cum[:, chunk - 1:chunk]
    cnt_ref[...] = running.astype(I32)
    first_row = _seg_cumsum(running, N_EXPERTS) - running
    for c in range(n_chunks):
        before = (first_row + starts[c])[:, 0:1] + cum_ref[c]
        pos_ref[:, c * chunk:(c + 1) * chunk] = (
            jnp.sum(jnp.where(hot(c), before, 0.0), axis=0, keepdims=True) - 1.0).astype(I32)


def _sort_positions(flat, *, chunk=512):
    n = flat.shape[0]
    n_chunks = pl.cdiv(n, chunk)
    e = jnp.pad(flat, (0, n_chunks * chunk - n), constant_values=N_EXPERTS)
    pos, cnt = pl.pallas_call(
        functools.partial(_sortpos_kernel, chunk=chunk),
        out_shape=(jax.ShapeDtypeStruct((1, n_chunks * chunk), I32),
                   jax.ShapeDtypeStruct((N_EXPERTS, LANES), I32)),
        scratch_shapes=[pltpu.VMEM((n_chunks, N_EXPERTS, chunk), F32)],
        compiler_params=pltpu.CompilerParams(vmem_limit_bytes=VMEM_LIMIT_BYTES),
        name="moe_sort_positions",
    )(e.reshape(1, n_chunks * chunk))
    return pos[0, :n], cnt[:, 0]


def _route_plan(eid, tile, n_prompt):
    n_tok = eid.shape[1]
    pos, counts = _sort_positions(eid.reshape(-1))
    ends = jnp.cumsum(counts)
    offs = ends - counts
    first_tile = offs // tile
    last_tile = (ends - 1) // tile
    nvis = jnp.where(counts > 0, last_tile - first_tile + 1, 0)
    vend = jnp.cumsum(nvis)
    vbase = vend - nvis
    nact = vend[-1]
    n_visits = pl.cdiv(TOP_K * n_tok, tile) + N_EXPERTS - 1
    v = jnp.minimum(jnp.arange(n_visits, dtype=I32), nact - 1)
    e = jnp.sum((vend[None, :] <= v[:, None]).astype(I32), axis=1)
    hot = (e[:, None] == jnp.arange(N_EXPERTS, dtype=I32)[None, :]).astype(I32)
    look = lambda table: jnp.sum(hot * table[None, :].astype(I32), axis=1)
    t = look(first_tile) + (v - look(vbase))
    assert n_prompt % INVERT_UNROLL == 0 and (n_tok - n_prompt) % INVERT_UNROLL == 0
    counts_smem = jnp.stack([nact.astype(I32),
                             jnp.asarray(n_prompt // INVERT_UNROLL, I32),
                             jnp.asarray((n_tok - n_prompt) // INVERT_UNROLL, I32)])
    plan = (t.astype(I32), e.astype(I32), look(offs), look(ends), counts_smem)
    return pos, plan


def kernel(x_prompt, x_sample, state_conv, state_hgrn, cache_k_win, cache_v_win, w_in, w_out, conv_w, hg_lb_param, hg_gain, attn_sinks, ln1_g, ln1_b, w_group, b_group, w_router, b_router, w_gate, w_up, w_down, ln2_g, ln2_b):
    bsz, seq, d = x_prompt.shape
    nb = x_sample.shape[0]
    n_prompt = bsz * seq
    n_tok = n_prompt + nb
    w_buf = cache_k_win.shape[2]

    xp = x_prompt.reshape(n_prompt, d)
    xp_mm = xp
    xs = x_sample.reshape(nb, d)
    pad_rows = ROUTE_ROWS - N_GROUPS - N_EXPERTS
    w_out_b = w_out.astype(BF16)
    outs = {k: [] for k in ("cp", "cs", "hp", "hs", "kp", "ks", "vp", "vs")}
    for l in range(DEPTH):
        gain = hg_gain[l].reshape(1, HG_DK)
        sinks = attn_sinks[l].reshape(1, N_HEADS)
        rw_t = jnp.concatenate([w_group[l].T, w_router[l].T, jnp.zeros((pad_rows, d), F32)], axis=0)
        rb = jnp.concatenate([b_group[l], b_router[l], jnp.zeros((pad_rows,), F32)]).reshape(ROUTE_ROWS, 1)
        g1, b1 = ln1_g[l].reshape(1, d), ln1_b[l].reshape(1, d)
        g2, b2 = ln2_g[l].reshape(1, d), ln2_b[l].reshape(1, d)

        proj = _inproj(xp_mm, w_in, l, tm=2048, tn=512)
        ya, ctail = _conv_prompt(proj, conv_w[l], bsz, seq)
        yb, hstate = _hgrn_prompt(proj, hg_lb_param, gain, l, bsz, seq)
        yc = _attn_prompt(proj, sinks, bsz, seq)
        x1t, eid_p, ewt_p = _outproj_ln_route(
            [ya, yb, yc], xp, w_out_b, l, g1, b1, rw_t, rb, precise=False,
            x1_rows=n_tok, tn=d)
        outs["cp"].append(ctail[:, SUBLANES - (CONV_W - 1):])
        outs["hp"].append(hstate)
        kv = proj.reshape(bsz, seq, IN_DIM)[:, seq - w_buf:]
        outs["kp"].append(kv[:, :, OFF_AK:OFF_AK + KV_DIM].reshape(bsz, w_buf, N_KV_HEADS, HEAD_DIM))
        outs["vp"].append(kv[:, :, OFF_AV:OFF_AV + KV_DIM].reshape(bsz, w_buf, N_KV_HEADS, HEAD_DIM))

        proj_s = _inproj3(xs, w_in, l)
        ysm, cst, hst, kst, vst = _sample_mixers(
            proj_s, state_conv, state_hgrn, cache_k_win, cache_v_win,
            conv_w[l], hg_lb_param, gain, sinks, l)
        x1t, eid_s, ewt_s = _outproj_ln_route(
            [ysm.reshape(nb, d)], xs, w_out, l, g1, b1, rw_t, rb, precise=True,
            x1_rows=n_tok, x1_row0=n_prompt, x1_buf=x1t)
        outs["cs"].append(cst)
        outs["hs"].append(hst)
        outs["ks"].append(kst.reshape(nb, w_buf, N_KV_HEADS, HEAD_DIM))
        outs["vs"].append(vst.reshape(nb, w_buf, N_KV_HEADS, HEAD_DIM))

        eid = jnp.concatenate([eid_p[:TOP_K], eid_s[:TOP_K]], axis=1)
        pos, plan = _route_plan(eid, EXPERT_TILE, n_prompt)
        y_sorted = _expert_ffn(pos, plan, x1t, w_gate, w_up, w_down, l,
                               n_prompt=n_prompt, n_sample=nb)
        xp, *xp_bf16 = _combine_ln(pos, x1t, y_sorted, ewt_p, g2, b2, row0=0, m=n_prompt,
                                   n_tok=n_tok, with_bf16=l + 1 < DEPTH)
        xp_mm = xp_bf16[0] if xp_bf16 else xp
        xs, = _combine_ln(pos, x1t, y_sorted, ewt_s, g2, b2, row0=n_prompt, m=nb,
                          n_tok=n_tok, with_bf16=False)

    st = lambda k: jnp.stack(outs[k])
    return (xp.reshape(bsz, seq, d), xs.reshape(nb, 1, d), st("cp"), st("cs"),
            st("hp"), st("hs"), st("kp"), st("ks"), st("vp"), st("vs"))
```

```python
import functools

import jax
import jax.numpy as jnp
import numpy as np
from jax import lax
from jax.experimental import pallas as pl
from jax.experimental.pallas import tpu as pltpu

F32 = jnp.float32
BF16 = jnp.bfloat16
I32 = jnp.int32

D_MODEL = 2048
DEPTH = 2
CONV_DIM = 512
CONV_W = 3
HG_DIM = 512
HG_HEADS = 4
HG_DK = 128
HG_CHUNK = 16
HEAD_DIM = 64
ATTN_DIM = 1024
N_HEADS = 16
N_KV_HEADS = 4
Q_PER_KV = 4
KV_DIM = 256
WINDOW = 128
N_GROUPS = 4
EXPERTS_PER_GROUP = 4
N_EXPERTS = 16
TOP_K = 2
D_EXPERT = 512
ALPHA = (2 * DEPTH) ** 0.25
LOG2E = 1.4426950408889634
LN_EPS = 1e-5
RMS_EPS = 1e-6
IN_DIM = 5120
OFF_CB, OFF_CC, OFF_CH = 0, 512, 1024
OFF_HQ, OFF_HF, OFF_HI, OFF_HG = 1536, 2048, 2560, 3072
OFF_AQ, OFF_AK, OFF_AV = 3584, 4608, 4864

VMEM_LIMIT_BYTES = 56 * 1024 * 1024
LANES = 128
SUBLANES = 8

EXPERT_TILE = 256
FFN_PIECE_COLS = 256
COMBINE_TILE = 512
INVERT_UNROLL = 8
DMA_UNROLL = 8
X_AHEAD = 2
SAMPLE_ROWS = 4
ATTN_BLOCKS = 4
ROUTE_ROWS = 32


TOKEN_ROWS = D_MODEL // LANES
TOKEN_PITCH = 20


def _store_token_major(ref, x, pitch):
    n = x.shape[0]
    for c in range(TOKEN_ROWS):
        ref[pl.ds(c, n, stride=pitch), :] = x[:, c * LANES:(c + 1) * LANES]


def _zero_token_pad(ref, n, pitch):
    for c in range(TOKEN_ROWS, pitch):
        ref[pl.ds(c, n, stride=pitch), :] = jnp.zeros((n, LANES), ref.dtype)


def _load_token_major(ref, n, pitch):
    return jnp.concatenate(
        [ref[pl.ds(c, n, stride=pitch), :] for c in range(TOKEN_ROWS)], axis=1)


def _cparams(*sem):
    return pltpu.CompilerParams(dimension_semantics=sem,
                                vmem_limit_bytes=VMEM_LIMIT_BYTES)


def _dot(a, b):
    return jnp.dot(a, b, preferred_element_type=F32)


def _dot_nt(a, b):
    return lax.dot_general(a, b, (((1,), (1,)), ((), ())),
                           preferred_element_type=F32)


def _split(x):
    hi = x.astype(BF16)
    lo = (x - hi.astype(F32)).astype(BF16)
    return hi, lo


def _dot3(a, b):
    ah, al = _split(a)
    bh, bl = _split(b)
    return _dot(ah, bh) + (_dot(ah, bl) + _dot(al, bh))


def _dot3_nt(a, b):
    ah, al = _split(a)
    bh, bl = _split(b)
    return _dot_nt(ah, bh) + (_dot_nt(ah, bl) + _dot_nt(al, bh))


def _sigmoid(x):
    return 1.0 / (1.0 + jnp.exp(-x))


def _silu(x):
    return x * (0.5 * jnp.tanh(0.5 * x) + 0.5)


def _col_from_row(row):
    n = row.shape[1]
    eye = (lax.broadcasted_iota(I32, (n, n), 0)
           == lax.broadcasted_iota(I32, (n, n), 1))
    return jnp.sum(jnp.where(eye, row, 0.0), axis=1, keepdims=True)


def _hg_lower_bound(lbp, layer):
    m = jnp.max(lbp, axis=0, keepdims=True)
    e = jnp.exp(lbp - m)
    soft = e / jnp.sum(e, axis=0, keepdims=True)
    acc = soft[0:1]
    for i in range(1, layer + 1):
        acc = acc + soft[i:i + 1]
    return acc - soft[0:1]


def _alibi_slope(head):
    return float(2.0 ** (-8.0 * (head + 1) / N_HEADS))


def _inproj_kernel(x_ref, w_ref, o_ref, xb_ref):
    @pl.when(pl.program_id(1) == 0)
    def _():
        xb_ref[...] = x_ref[...].astype(BF16)

    o_ref[...] = _dot(xb_ref[...], w_ref[...].astype(BF16))


def _inproj(x, w, layer, *, tm=1024, tn=1024):
    m, k = x.shape
    n = w.shape[2]
    tm = min(tm, m)
    x_mode = dict(pipeline_mode=pl.Buffered(1)) if x.dtype == F32 else {}
    return pl.pallas_call(
        _inproj_kernel,
        out_shape=jax.ShapeDtypeStruct((m, n), F32),
        grid=(m // tm, n // tn),
        in_specs=[pl.BlockSpec((tm, k), lambda i, j: (i, 0), **x_mode),
                  pl.BlockSpec((None, k, tn), lambda i, j: (layer, 0, j))],
        out_specs=pl.BlockSpec((tm, tn), lambda i, j: (i, j)),
        scratch_shapes=[pltpu.VMEM((tm, k), BF16)],
        compiler_params=_cparams("arbitrary", "arbitrary"),
        name="inproj",
    )(x, w)


def _inproj3_kernel(x_ref, w_ref, o_ref):
    o_ref[...] = _dot3(x_ref[...], w_ref[...])


def _inproj3(x, w, layer, *, tn=512):
    m, k = x.shape
    n = w.shape[2]
    return pl.pallas_call(
        _inproj3_kernel,
        out_shape=jax.ShapeDtypeStruct((m, n), F32),
        grid=(n // tn,),
        in_specs=[pl.BlockSpec((m, k), lambda j: (0, 0)),
                  pl.BlockSpec((None, k, tn), lambda j: (layer, 0, j))],
        out_specs=pl.BlockSpec((m, tn), lambda j: (0, j)),
        compiler_params=_cparams("arbitrary"),
        name="inproj_sample",
    )(x, w)


def _conv_kernel(cb_ref, cc_ref, ch_ref, w_ref, y_ref, tail_ref, carry_ref):
    i = pl.program_id(1)

    @pl.when(i == 0)
    def _():
        carry_ref[...] = jnp.zeros_like(carry_ref)

    u = cc_ref[...] * ch_ref[...]
    tl = u.shape[0]
    row = lax.broadcasted_iota(I32, u.shape, 0)
    prev1 = carry_ref[SUBLANES - 1:SUBLANES, :]
    prev2 = carry_ref[SUBLANES - 2:SUBLANES - 1, :]
    u1 = jnp.where(row == 0, prev1, pltpu.roll(u, 1, 0))
    u2 = jnp.where(row == 0, prev2, jnp.where(row == 1, prev1, pltpu.roll(u, 2, 0)))
    w = w_ref[...]
    y = w[0:1] * u2 + w[1:2] * u1 + w[2:3] * u
    y_ref[...] = (cb_ref[...] * y).astype(y_ref.dtype)
    tail = u[tl - SUBLANES:tl, :]
    carry_ref[...] = tail
    tail_ref[...] = tail


def _conv_prompt(proj, conv_w, bsz, seq, *, tl=512):
    tl = min(tl, seq)
    nt = seq // tl
    cblk = lambda c: pl.BlockSpec((tl, CONV_DIM), lambda b, i, c=c: (b * nt + i, c))
    return pl.pallas_call(
        _conv_kernel,
        out_shape=(jax.ShapeDtypeStruct((bsz * seq, CONV_DIM), BF16),
                   jax.ShapeDtypeStruct((bsz, SUBLANES, CONV_DIM), F32)),
        grid=(bsz, nt),
        in_specs=[cblk(OFF_CB // CONV_DIM), cblk(OFF_CC // CONV_DIM),
                  cblk(OFF_CH // CONV_DIM),
                  pl.BlockSpec((CONV_W, CONV_DIM), lambda b, i: (0, 0))],
        out_specs=(pl.BlockSpec((tl, CONV_DIM), lambda b, i: (b * nt + i, 0)),
                   pl.BlockSpec((None, SUBLANES, CONV_DIM), lambda b, i: (b, 0, 0))),
        scratch_shapes=[pltpu.VMEM((SUBLANES, CONV_DIM), F32)],
        compiler_params=_cparams("arbitrary", "arbitrary"),
        name="conv_prompt",
    )(proj, proj, proj, conv_w)


def _seg_cumsum(g, seg):
    row = lax.broadcasted_iota(I32, g.shape, 0) % seg
    s = 1
    while s < seg:
        g = g + jnp.where(row >= s, pltpu.roll(g, s, 0), 0.0)
        s *= 2
    return g


def _hgrn_kernel(hq_ref, hf_ref, hi_ref, hg_ref, lbp_ref, gain_ref,
                 y_ref, st_ref, cum_ref, k_ref, q_ref, s_ref, *, layer):
    i = pl.program_id(1)
    nt = pl.num_programs(1)
    tb = hq_ref.shape[0]
    c = HG_CHUNK

    @pl.when(i == 0)
    def _():
        s_ref[...] = jnp.zeros_like(s_ref)

    lb = _hg_lower_bound(lbp_ref[...], layer)
    f = lb + (1.0 - lb) * _sigmoid(hf_ref[...])
    cum_ref[...] = _seg_cumsum(jnp.log(f) * LOG2E, c)
    k_ref[...] = 1.0 - f
    q_ref[...] = hq_ref[...] * (HG_DK ** -0.5)

    ones = jnp.ones((HG_DK, HG_DK), BF16)
    gain = gain_ref[...]
    hc = c // 2
    trow = lax.broadcasted_iota(I32, (hc, HG_DK), 0)

    def chunk(ci, carry):
        r0 = pl.multiple_of(ci * c, c)
        for h in range(HG_HEADS):
            cols = slice(h * HG_DK, (h + 1) * HG_DK)
            cum = cum_ref[pl.ds(r0, c), cols]
            kc = k_ref[pl.ds(r0, c), cols]
            qc = q_ref[pl.ds(r0, c), cols]
            vc = hi_ref[pl.ds(r0, c), cols]
            parts = []
            for s in range(c):
                cs, qk_hi = cum[s:s + 1], qc[hc:] * kc[s:s + 1]
                if s < hc:
                    dlo = jnp.exp2(jnp.where(trow >= s, cum[:hc] - cs, -jnp.inf))
                    parts.append((qc[:hc] * kc[s:s + 1] * dlo).astype(BF16))
                    dhi = jnp.exp2(cum[hc:] - cs)
                else:
                    dhi = jnp.exp2(jnp.where(trow >= s - hc, cum[hc:] - cs, -jnp.inf))
                parts.append((qk_hi * dhi).astype(BF16))
            sc = _dot(jnp.concatenate(parts, axis=0), ones)
            o_lo = jnp.zeros((hc, HG_DK), F32)
            o_hi = jnp.zeros((hc, HG_DK), F32)
            r = 0
            for s in range(c):
                if s < hc:
                    o_lo = o_lo + sc[r:r + hc] * vc[s:s + 1]
                    r += hc
                o_hi = o_hi + sc[r:r + hc] * vc[s:s + 1]
                r += hc
            o = jnp.concatenate([o_lo, o_hi], axis=0)
            st = s_ref[h]
            last = cum[c - 1:c]
            o = o + _dot_nt((qc * jnp.exp2(cum)).astype(BF16), st.astype(BF16))
            kte = (kc * jnp.exp2(last - cum)).astype(BF16)
            du = lax.dot_general(vc.astype(BF16), kte, (((0,), (0,)), ((), ())),
                                 preferred_element_type=F32)
            s_ref[h] = jnp.exp2(last) * st + du
            o = o * lax.rsqrt(jnp.mean(o * o, axis=-1, keepdims=True) + RMS_EPS) * gain
            y_ref[pl.ds(r0, c), cols] = (o * _silu(hg_ref[pl.ds(r0, c), cols])).astype(y_ref.dtype)
        return carry

    lax.fori_loop(0, tb // c, chunk, 0, unroll=16)

    @pl.when(i == nt - 1)
    def _():
        for h in range(HG_HEADS):
            st_ref[h] = s_ref[h].T


def _hgrn_prompt(proj, lb_param, gain, layer, bsz, seq, *, tb=512):
    tb = min(tb, seq)
    nt = seq // tb
    cblk = lambda c: pl.BlockSpec((tb, HG_DIM), lambda b, i, c=c: (b * nt + i, c))
    return pl.pallas_call(
        functools.partial(_hgrn_kernel, layer=layer),
        out_shape=(jax.ShapeDtypeStruct((bsz * seq, HG_DIM), BF16),
                   jax.ShapeDtypeStruct((bsz, HG_HEADS, HG_DK, HG_DK), F32)),
        grid=(bsz, nt),
        in_specs=[cblk(OFF_HQ // HG_DIM), cblk(OFF_HF // HG_DIM),
                  cblk(OFF_HI // HG_DIM), cblk(OFF_HG // HG_DIM),
                  pl.BlockSpec((DEPTH, HG_DIM), lambda b, i: (0, 0)),
                  pl.BlockSpec((1, HG_DK), lambda b, i: (0, 0))],
        out_specs=(pl.BlockSpec((tb, HG_DIM), lambda b, i: (b * nt + i, 0)),
                   pl.BlockSpec((None, HG_HEADS, HG_DK, HG_DK), lambda b, i: (b, 0, 0, 0))),
        scratch_shapes=[pltpu.VMEM((tb, HG_DIM), F32),
                        pltpu.VMEM((tb, HG_DIM), F32),
                        pltpu.VMEM((tb, HG_DIM), F32),
                        pltpu.VMEM((HG_HEADS, HG_DK, HG_DK), F32)],
        compiler_params=_cparams("arbitrary", "arbitrary"),
        name="hgrn_prompt",
    )(proj, proj, proj, proj, lb_param, gain)


def _attn_kernel(qa_ref, qb_ref, kc_ref, vc_ref, kp_ref, vp_ref, sink_ref, y_ref,
                 bias_ref, sinkcol_ref):
    n = pl.program_id(1)
    w = WINDOW
    qi = lax.broadcasted_iota(I32, (Q_PER_KV * w, 2 * w), 0) % w
    kj = lax.broadcasted_iota(I32, (Q_PER_KV * w, 2 * w), 1)

    @pl.when((pl.program_id(0) == 0) & (n == 0))
    def _():
        dist = qi + w - kj
        valid = (dist >= 0) & (dist < w)
        distf = dist.astype(F32)
        for kv in range(N_KV_HEADS):
            grp = lax.broadcasted_iota(I32, (Q_PER_KV * w, 1), 0) // w
            slope = jnp.zeros((Q_PER_KV * w, 1), F32)
            for g in range(Q_PER_KV):
                slope = jnp.where(grp == g, _alibi_slope(kv * Q_PER_KV + g), slope)
            bias = jnp.where(valid, -slope * distf, -jnp.inf)
            bias_ref[1, pl.ds(kv * Q_PER_KV * w, Q_PER_KV * w), :] = bias
            bias_ref[0, pl.ds(kv * Q_PER_KV * w, Q_PER_KV * w), :] = jnp.where(
                kj >= w, bias, -jnp.inf)
        for h in range(N_HEADS):
            sinkcol_ref[pl.ds(h * w, w), :] = jnp.broadcast_to(sink_ref[:, h:h + 1], (w, LANES))

    low = lax.broadcasted_iota(I32, (1, LANES), 1) < HEAD_DIM
    ones = jnp.ones((2 * w, LANES), BF16)
    n_blk = kc_ref.shape[0] // w
    scores, values = [], []
    for j in range(n_blk):
        blk = slice(j * w, (j + 1) * w)
        for kv in range(N_KV_HEADS):
            q_ref = qa_ref if kv < 2 else qb_ref
            qoff = (kv % 2) * Q_PER_KV * HEAD_DIM
            kt = slice((kv // 2) * LANES, (kv // 2 + 1) * LANES)

            def both_halves(prev_ref, cur_ref):
                prev = prev_ref[:, kt] if j == 0 else cur_ref[(j - 1) * w:j * w, kt]
                t = jnp.concatenate([prev, cur_ref[blk, kt]], axis=0)
                r = pltpu.roll(t, HEAD_DIM, 1)
                return (jnp.where(low, t, r) if kv % 2 == 0
                        else jnp.where(low, r, t)).astype(BF16)

            q_parts = []
            for g in range(Q_PER_KV):
                qt = q_ref[blk, qoff + (g // 2) * LANES: qoff + (g // 2 + 1) * LANES]
                q_parts.append(jnp.where(low if g % 2 == 0 else ~low, qt, 0.0))
            q = jnp.concatenate(q_parts, axis=0).astype(BF16)
            scores.append(_dot_nt(q, both_halves(kp_ref, kc_ref)))
            values.append(both_halves(vp_ref, vc_ref))
    bias = jnp.concatenate([bias_ref[jnp.minimum(n, 1)]] + [bias_ref[1]] * (n_blk - 1), axis=0)
    s = jnp.concatenate(scores, axis=0) * (HEAD_DIM ** -0.5) + bias
    sink = jnp.concatenate([sinkcol_ref[...]] * n_blk, axis=0)
    m = jnp.maximum(jnp.max(s, axis=-1, keepdims=True), sink)
    p = jnp.exp(s - jnp.concatenate([m, m], axis=1)).astype(BF16)
    rows = Q_PER_KV * w
    for j in range(n_blk):
        outs = []
        for kv in range(N_KV_HEADS):
            sl = slice((j * N_KV_HEADS + kv) * rows, (j * N_KV_HEADS + kv + 1) * rows)
            pk = p[sl]
            denom = _dot(pk, ones) + jnp.exp(sink[sl] - m[sl])
            o = _dot(pk, values[j * N_KV_HEADS + kv]) / denom
            for g in range(0, Q_PER_KV, 2):
                outs.append(jnp.where(low, o[g * w:(g + 1) * w], o[(g + 1) * w:(g + 2) * w]))
        y_ref[j * w:(j + 1) * w, :] = jnp.concatenate(outs, axis=-1).astype(y_ref.dtype)


def _attn_prompt(proj, sinks, bsz, seq):
    w = WINDOW
    per = ATTN_BLOCKS if seq % (ATTN_BLOCKS * w) == 0 else 1
    nb = seq // (per * w)
    half = ATTN_DIM // 2
    cur = lambda width, off: pl.BlockSpec(
        (per * w, width), lambda b, n: (b * nb + n, off // width))
    prev = lambda width, off: pl.BlockSpec(
        (w, width), lambda b, n: ((b * nb + n) * per - jnp.minimum(n, 1), off // width))
    return pl.pallas_call(
        _attn_kernel,
        out_shape=jax.ShapeDtypeStruct((bsz * seq, ATTN_DIM), BF16),
        grid=(bsz, nb),
        in_specs=[cur(half, OFF_AQ), cur(half, OFF_AQ + half),
                  cur(KV_DIM, OFF_AK), cur(KV_DIM, OFF_AV),
                  prev(KV_DIM, OFF_AK), prev(KV_DIM, OFF_AV),
                  pl.BlockSpec((1, N_HEADS), lambda b, n: (0, 0))],
        out_specs=pl.BlockSpec((per * w, ATTN_DIM), lambda b, n: (b * nb + n, 0)),
        scratch_shapes=[pltpu.VMEM((2, N_HEADS * w, 2 * w), F32),
                        pltpu.VMEM((N_HEADS * w, LANES), F32)],
        compiler_params=_cparams("arbitrary", "arbitrary"),
        name="attn_prompt",
    )(proj, proj, proj, proj, proj, proj, sinks)


def _sample_mix_kernel(p_ref, sc_ref, s0_ref, kc_ref, vc_ref, cw_ref, lbp_ref,
                       gain_ref, sink_ref,
                       y_ref, sco_ref, so_ref, ko_ref, vo_ref, *, layer):
    for r in range(p_ref.shape[0]):
        _sample_mix_one(p_ref.at[r], sc_ref.at[r], s0_ref.at[r], kc_ref.at[r], vc_ref.at[r],
                        cw_ref, lbp_ref, gain_ref, sink_ref,
                        y_ref.at[r], sco_ref.at[r], so_ref.at[r], ko_ref.at[r], vo_ref.at[r],
                        layer)


def _sample_mix_one(p_ref, sc_ref, s0_ref, kc_ref, vc_ref, cw_ref, lbp_ref, gain_ref, sink_ref,
                    y_ref, sco_ref, so_ref, ko_ref, vo_ref, layer):
    p = p_ref[...]
    seg = lambda off, n: p[:, off:off + n]
    u = seg(OFF_CC, CONV_DIM) * seg(OFF_CH, CONV_DIM)
    hist = sc_ref[...]
    cw = cw_ref[...]
    conv = cw[0:1] * hist[0:1] + cw[1:2] * hist[1:2] + cw[2:3] * u
    ya = seg(OFF_CB, CONV_DIM) * conv
    sco_ref[...] = jnp.concatenate([hist[1:2], u], axis=0)
    lb = _hg_lower_bound(lbp_ref[...], layer)
    f = lb + (1.0 - lb) * _sigmoid(seg(OFF_HF, HG_DIM))
    g = jnp.log(f)
    kk = 1.0 - f
    q = seg(OFF_HQ, HG_DIM) * (HG_DK ** -0.5)
    v = seg(OFF_HI, HG_DIM)
    gate = seg(OFF_HG, HG_DIM)
    gain = gain_ref[...]
    yb = []
    for h in range(HG_HEADS):
        cols = slice(h * HG_DK, (h + 1) * HG_DK)
        s0 = s0_ref[h]
        eg = jnp.exp(g[:, cols])
        qe_col = _col_from_row(q[:, cols] * eg)
        o = (jnp.sum(q[:, cols] * kk[:, cols], axis=-1, keepdims=True) * v[:, cols]
             + jnp.sum(qe_col * s0, axis=0, keepdims=True))
        so_ref[h] = _col_from_row(eg) * s0 + _col_from_row(kk[:, cols]) * v[:, cols]
        o = o * lax.rsqrt(jnp.mean(o * o, axis=-1, keepdims=True) + RMS_EPS) * gain
        yb.append(o * _silu(gate[:, cols]))
    w = kc_ref.shape[0]
    kcache = kc_ref[...]
    vcache = vc_ref[...]
    knew = seg(OFF_AK, KV_DIM)
    vnew = seg(OFF_AV, KV_DIM)
    aq = seg(OFF_AQ, ATTN_DIM)
    sinks = sink_ref[...]
    kj = lax.broadcasted_iota(I32, (N_HEADS, w), 1)
    dist = w - kj
    valid = dist < WINDOW
    q_rows = []
    for hd in range(N_HEADS):
        kv = hd // Q_PER_KV
        piece = aq[:, hd * HEAD_DIM:(hd + 1) * HEAD_DIM]
        parts = ([jnp.zeros((1, kv * HEAD_DIM), F32)] if kv else []) + [piece]
        if kv + 1 < N_KV_HEADS:
            parts.append(jnp.zeros((1, (N_KV_HEADS - kv - 1) * HEAD_DIM), F32))
        q_rows.append(jnp.concatenate(parts, axis=1))
    q_all = jnp.concatenate(q_rows, axis=0)
    head = lax.broadcasted_iota(I32, (N_HEADS, 1), 0).astype(F32)
    slope = jnp.exp2(-8.0 * (head + 1.0) / N_HEADS)
    sink = _col_from_row(sinks)
    scale = HEAD_DIM ** -0.5
    sc = _dot3_nt(q_all, kcache) * scale - slope * dist.astype(F32)
    sc = jnp.where(valid, sc, -jnp.inf)
    sn = jnp.sum(q_all * knew, axis=-1, keepdims=True) * scale
    m = jnp.maximum(jnp.maximum(jnp.max(sc, axis=-1, keepdims=True), sn), sink)
    pc = jnp.exp(sc - m)
    pn = jnp.exp(sn - m)
    denom = jnp.sum(pc, axis=-1, keepdims=True) + pn + jnp.exp(sink - m)
    o_all = (_dot3(pc, vcache) + pn * vnew) / denom
    yc = [o_all[hd:hd + 1, (hd // Q_PER_KV) * HEAD_DIM:(hd // Q_PER_KV + 1) * HEAD_DIM]
          for hd in range(N_HEADS)]
    y_ref[...] = jnp.concatenate([ya] + yb + yc, axis=-1)
    row = lax.broadcasted_iota(I32, (w, KV_DIM), 0)
    ko_ref[...] = jnp.where(row == w - 1, knew, pltpu.roll(kcache, w - 1, 0))
    vo_ref[...] = jnp.where(row == w - 1, vnew, pltpu.roll(vcache, w - 1, 0))


def _sample_mixers(proj, state_conv, state_hgrn, cache_k, cache_v, conv_w,
                   lb_param, gain, sinks, layer):
    nb = proj.shape[0]
    w = cache_k.shape[2]
    rows = SAMPLE_ROWS
    assert nb % rows == 0
    per_b = lambda *shape: pl.BlockSpec((rows,) + shape,
                                        lambda b: (b,) + (0,) * len(shape))
    per_lb = lambda *shape: pl.BlockSpec((None, rows) + shape,
                                         lambda b: (layer, b) + (0,) * len(shape))
    whole = lambda *shape: pl.BlockSpec(shape, lambda b: (0,) * len(shape))
    return pl.pallas_call(
        functools.partial(_sample_mix_kernel, layer=layer),
        out_shape=(jax.ShapeDtypeStruct((nb, 1, D_MODEL), F32),
                   jax.ShapeDtypeStruct((nb, CONV_W - 1, CONV_DIM), F32),
                   jax.ShapeDtypeStruct((nb, HG_HEADS, HG_DK, HG_DK), F32),
                   jax.ShapeDtypeStruct((nb, w, KV_DIM), F32),
                   jax.ShapeDtypeStruct((nb, w, KV_DIM), F32)),
        grid=(nb // rows,),
        in_specs=[per_b(1, IN_DIM), per_lb(CONV_W - 1, CONV_DIM),
                  per_lb(HG_HEADS, HG_DK, HG_DK), per_lb(w, KV_DIM), per_lb(w, KV_DIM),
                  whole(CONV_W, CONV_DIM), whole(DEPTH, HG_DIM), whole(1, HG_DK),
                  whole(1, N_HEADS)],
        out_specs=(per_b(1, D_MODEL), per_b(CONV_W - 1, CONV_DIM),
                   per_b(HG_HEADS, HG_DK, HG_DK), per_b(w, KV_DIM), per_b(w, KV_DIM)),
        compiler_params=_cparams("arbitrary"),
        name="sample_mixers",
    )(proj.reshape(nb, 1, IN_DIM), state_conv, state_hgrn,
      cache_k.reshape(DEPTH, nb, w, KV_DIM), cache_v.reshape(DEPTH, nb, w, KV_DIM),
      conv_w, lb_param, gain, sinks)


def _layer_norm(z, g, b):
    mu = jnp.mean(z, axis=-1, keepdims=True)
    zc = z - mu
    var = jnp.mean(zc * zc, axis=-1, keepdims=True)
    return zc * lax.rsqrt(var + LN_EPS) * g + b


def _route(logits, bias):
    lg = logits + bias
    rowv = lambda r: lg[r:r + 1]
    best, gidx = rowv(0), jnp.zeros_like(rowv(0), dtype=I32)
    for r in range(1, N_GROUPS):
        upd = rowv(r) > best
        best = jnp.where(upd, rowv(r), best)
        gidx = jnp.where(upd, r, gidx)
    gden = sum(jnp.exp(rowv(r) - best) for r in range(N_GROUPS))
    gprob = 1.0 / gden
    ev = []
    for j in range(EXPERTS_PER_GROUP):
        val = rowv(N_GROUPS + j)
        for grp in range(1, N_GROUPS):
            val = jnp.where(gidx == grp, rowv(N_GROUPS + grp * EXPERTS_PER_GROUP + j), val)
        ev.append(val)
    v1, j1 = ev[0], jnp.zeros_like(gidx)
    for j in range(1, EXPERTS_PER_GROUP):
        upd = ev[j] > v1
        v1 = jnp.where(upd, ev[j], v1)
        j1 = jnp.where(upd, j, j1)
    v2, j2 = jnp.full_like(v1, -jnp.inf), jnp.zeros_like(gidx)
    for j in range(EXPERTS_PER_GROUP):
        upd = (j1 != j) & (ev[j] > v2)
        v2 = jnp.where(upd, ev[j], v2)
        j2 = jnp.where(upd, j, j2)
    e2 = jnp.exp(v2 - v1)
    w1 = gprob / (1.0 + e2)
    w2 = gprob * e2 / (1.0 + e2)
    base = gidx * EXPERTS_PER_GROUP
    return (jnp.concatenate([base + j1, base + j2], axis=0),
            jnp.concatenate([w1, w2], axis=0))


def _outproj_kernel(*refs, n_y, precise, n_real, n_j):
    y_refs = refs[:n_y]
    x_ref, w_ref, g_ref, b_ref, rw_ref, rb_ref = refs[n_y:n_y + 6]
    x1t_ref, eid_ref, ewt_ref, acc_ref = refs[-4:]
    i = pl.program_id(0)
    j = pl.program_id(1)
    tn = w_ref.shape[1]
    mm = _dot3 if precise else _dot

    def mix():
        y = jnp.concatenate([y_ref[...] for y_ref in y_refs], axis=1)
        return mm(y, w_ref[...])

    if n_j > 1:
        @pl.when(i < n_real)
        def _():
            acc_ref[:, pl.ds(pl.multiple_of(j * tn, tn), tn)] = mix()

    @pl.when((i >= n_real) & (j == n_j - 1))
    def _():
        x1t_ref[...] = jnp.zeros_like(x1t_ref)

    @pl.when((i < n_real) & (j == n_j - 1))
    def _():
        z = ALPHA * x_ref[...] + (acc_ref[...] if n_j > 1 else mix())
        x1 = _layer_norm(z, g_ref[...], b_ref[...])
        _store_token_major(x1t_ref, x1, TOKEN_PITCH)
        _zero_token_pad(x1t_ref, x1.shape[0], TOKEN_PITCH)
        hi, lo = _split(x1)
        rwh, rwl = _split(rw_ref[...])
        both = _dot_nt(jnp.concatenate([rwh, rwl], axis=0), hi)
        logits = both[:ROUTE_ROWS] + (both[ROUTE_ROWS:] + _dot_nt(rwh, lo))
        eid, ewt = _route(logits, rb_ref[...])
        t = eid.shape[1]
        eid_ref[...] = jnp.concatenate([eid, jnp.zeros((SUBLANES - TOP_K, t), I32)], axis=0)
        ewt_ref[...] = jnp.concatenate([ewt, jnp.zeros((SUBLANES - TOP_K, t), F32)], axis=0)


def _outproj_ln_route(ys, x, w_out, layer, ln_g, ln_b, rw_t, rb, *, precise,
                      x1_rows, x1_row0=0, x1_buf=None, tm=512, tn=1024):
    m, d = x.shape
    tm = min(tm, m)
    assert x1_row0 % tm == 0
    n_y = len(ys)
    n_real = m // tm
    n_j = d // tn
    n_i = n_real if x1_buf is not None else pl.cdiv(x1_rows, tm)
    real = lambda i: jnp.minimum(i, n_real - 1)
    y_specs = [pl.BlockSpec((tm, y.shape[1]), lambda i, j: (real(i), 0)) for y in ys]
    vec = lambda: pl.BlockSpec((1, d), lambda i, j: (0, 0))
    operands = list(ys) + [x, w_out, ln_g, ln_b, rw_t, rb]
    in_specs = y_specs + [pl.BlockSpec((tm, d), lambda i, j: (real(i), 0)),
                          pl.BlockSpec((None, d, tn),
                                       lambda i, j: (layer, 0, jnp.where(i < n_real, j, n_j - 1))),
                          vec(), vec(),
                          pl.BlockSpec((ROUTE_ROWS, d), lambda i, j: (0, 0)),
                          pl.BlockSpec((ROUTE_ROWS, 1), lambda i, j: (0, 0))]
    aliases = {}
    if x1_buf is not None:
        aliases = {len(operands): 0}
        operands.append(x1_buf)
        in_specs.append(pl.BlockSpec(memory_space=pl.ANY))
    return pl.pallas_call(
        functools.partial(_outproj_kernel, n_y=n_y, precise=precise, n_real=n_real, n_j=n_j),
        out_shape=(jax.ShapeDtypeStruct((x1_rows * TOKEN_PITCH, LANES), F32),
                   jax.ShapeDtypeStruct((SUBLANES, m), I32),
                   jax.ShapeDtypeStruct((SUBLANES, m), F32)),
        grid=(n_i, n_j),
        in_specs=in_specs,
        out_specs=(pl.BlockSpec((tm * TOKEN_PITCH, LANES), lambda i, j: (x1_row0 // tm + i, 0)),
                   pl.BlockSpec((SUBLANES, tm), lambda i, j: (0, real(i))),
                   pl.BlockSpec((SUBLANES, tm), lambda i, j: (0, real(i)))),
        scratch_shapes=[pltpu.VMEM((tm, d) if n_j > 1 else (SUBLANES, LANES), F32)],
        input_output_aliases=aliases,
        compiler_params=_cparams("arbitrary", "arbitrary"),
        name="outproj_ln_route",
    )(*operands)


def _slot_base(k, is_sample, n_prompt, n_sample):
    return k * n_prompt + is_sample * (TOP_K * n_prompt + k * (n_sample - n_prompt))


def _token_copy(src_ref, src_row, dst_ref, dst_row, sem):
    return pltpu.make_async_copy(src_ref.at[pl.ds(src_row, TOKEN_ROWS)],
                                 dst_ref.at[pl.ds(dst_row, TOKEN_ROWS)], sem)


def _ffn_kernel(pos_ref, vt_ref, ve_ref, lo_ref, hi_ref, nact_ref,
                x_hbm, wg_ref, wu_ref, wd_ref, y_hbm,
                src_ref, xbuf, ybuf, wgb, wub, wdb, semx, semy,
                *, n_prompt, n_sample, tile):
    v = pl.program_id(0)
    nact = nact_ref[0]
    n_tok = n_prompt + n_sample
    rows = TOP_K * n_tok
    n_tab = src_ref.shape[0]
    n_tiles = n_tab // tile
    col = FFN_PIECE_COLS
    d, de = wgb.shape

    def gather_one(t0, b, i, prio):
        _token_copy(x_hbm, src_ref[t0 + i], xbuf.at[b], i * TOKEN_PITCH,
                    semx.at[b]).start(priority=prio)

    def looped(one):
        def body(grp, c):
            for u in range(DMA_UNROLL):
                one(grp * DMA_UNROLL + u, u % 2)
            return c

        lax.fori_loop(0, tile // DMA_UNROLL, body, 0)

    tile_rows = pl.ds(0, tile * TOKEN_ROWS)

    def wait_gather(b):
        pltpu.make_async_copy(x_hbm.at[tile_rows], xbuf.at[b].at[tile_rows], semx.at[b]).wait()

    def tile_out(t, b):
        return pltpu.make_async_copy(
            ybuf.at[b], y_hbm.at[pl.ds(t * (tile * TOKEN_PITCH), tile * TOKEN_PITCH)],
            semy.at[b])

    @pl.when(v == 0)
    def _():
        def invert_segment(k, smp, trips):
            s0 = k * n_tok + smp * n_prompt
            tok0 = smp * n_prompt

            def body(grp, c):
                j0 = grp * INVERT_UNROLL
                src0 = (tok0 + j0) * TOKEN_PITCH
                for u in range(INVERT_UNROLL):
                    src_ref[pos_ref[s0 + j0 + u]] = src0 + u * TOKEN_PITCH
                return c

            lax.fori_loop(0, trips, body, 0)

        for k in range(TOP_K):
            invert_segment(k, 0, nact_ref[1])
            invert_segment(k, 1, nact_ref[2])
        for p in range(rows, n_tab):
            src_ref[p] = 0
        ybuf[...] = jnp.zeros_like(ybuf)
        for t0 in range(min(X_AHEAD, n_tiles)):
            looped(lambda i, prio, t0=t0: gather_one(t0 * tile, t0, i, prio))

    def visit(do_gather):
        t = vt_ref[v]
        b = lax.rem(t, 2)
        first = (v == 0) | (vt_ref[jnp.maximum(v - 1, 0)] != t)
        pending = iter(range(tile))

        def issue(count):
            for _ in range(count):
                i = next(pending, None)
                if i is not None and do_gather:
                    gather_one((t + X_AHEAD) * tile, lax.rem(t + X_AHEAD, X_AHEAD + 1), i, i % 2)

        row = t * tile + lax.broadcasted_iota(I32, (tile, 1), 0)
        keep = first | ((row >= lo_ref[v]) & (row < hi_ref[v]))
        n_pieces = 2 * (de // col) + d // col
        per_piece = pl.cdiv(tile, n_pieces)
        xb = _load_token_major(xbuf.at[lax.rem(t, X_AHEAD + 1)], tile, TOKEN_PITCH).astype(BF16)
        hs = []
        for c in range(de // col):
            g = _dot(xb, wgb[:, c * col:(c + 1) * col])
            issue(per_piece)
            u = _dot(xb, wub[:, c * col:(c + 1) * col])
            issue(per_piece)
            hs.append((_silu(g) * u).astype(BF16))
        hb = jnp.concatenate(hs, axis=1)
        yb = ybuf.at[b]
        for c in range(d // col):
            y = _dot(hb, wdb[:, c * col:(c + 1) * col])
            for r in range(col // LANES):
                rows_r = pl.ds(c * (col // LANES) + r, tile, stride=TOKEN_PITCH)
                yb[rows_r, :] = jnp.where(keep, y[:, r * LANES:(r + 1) * LANES], yb[rows_r, :])
            issue(per_piece)
        issue(tile)

    @pl.when(v < nact)
    def _():
        t = vt_ref[v]
        b = lax.rem(t, 2)
        final = v == nact - 1
        first = (v == 0) | (vt_ref[jnp.maximum(v - 1, 0)] != t)
        last = final | (vt_ref[jnp.minimum(v + 1, nact - 1)] != t)

        @pl.when((v == 0) | (ve_ref[v] != ve_ref[jnp.maximum(v - 1, 0)]))
        def _():
            wgb[...] = wg_ref[...].astype(BF16)
            wub[...] = wu_ref[...].astype(BF16)
            wdb[...] = wd_ref[...].astype(BF16)

        @pl.when(first)
        def _():
            wait_gather(lax.rem(t, X_AHEAD + 1))

        @pl.when(first & (t >= 2))
        def _():
            tile_out(t - 2, b).wait()

        want_gather = last & (t + X_AHEAD < n_tiles)
        pl.when(want_gather)(functools.partial(visit, True))
        pl.when(jnp.logical_not(want_gather))(functools.partial(visit, False))

        @pl.when(last)
        def _():
            tile_out(t, b).start()

        @pl.when(final)
        def _():
            tile_out(t, b).wait()

            @pl.when(t >= 1)
            def _():
                tile_out(t - 1, 1 - b).wait()


def _expert_ffn(pos, plan, x1t, w_gate, w_up, w_down, layer, *, n_prompt, n_sample,
                tile=EXPERT_TILE):
    d, de = w_gate.shape[2], w_gate.shape[3]
    n_visits = plan[0].shape[0]
    rows = TOP_K * (n_prompt + n_sample)
    assert n_prompt + n_sample >= tile
    n_tab = pl.cdiv(rows, tile) * tile
    wspec = lambda r, c: pl.BlockSpec(
        (None, None, r, c), lambda v, pos, vt, ve, lo, hi, na: (layer, ve[v], 0, 0))
    any_spec = pl.BlockSpec(memory_space=pl.ANY)
    return pl.pallas_call(
        functools.partial(_ffn_kernel, n_prompt=n_prompt, n_sample=n_sample, tile=tile),
        out_shape=jax.ShapeDtypeStruct((n_tab * TOKEN_PITCH, LANES), F32),
        grid_spec=pltpu.PrefetchScalarGridSpec(
            num_scalar_prefetch=6,
            grid=(n_visits,),
            in_specs=[any_spec, wspec(d, de), wspec(d, de), wspec(de, d)],
            out_specs=any_spec,
            scratch_shapes=[pltpu.SMEM((n_tab,), I32),
                            pltpu.VMEM((X_AHEAD + 1, tile * TOKEN_PITCH, LANES), F32),
                            pltpu.VMEM((2, tile * TOKEN_PITCH, LANES), F32),
                            pltpu.VMEM((d, de), BF16),
                            pltpu.VMEM((d, de), BF16),
                            pltpu.VMEM((de, d), BF16),
                            pltpu.SemaphoreType.DMA((X_AHEAD + 1,)),
                            pltpu.SemaphoreType.DMA((2,))]),
        compiler_params=_cparams("arbitrary"),
        name="moe_ffn",
    )(pos, *plan, x1t, w_gate, w_up, w_down)


def _combine_kernel(pos_ref, x_ref, y_hbm, wt_ref, g_ref, b_ref, o_ref, *rest, row0, n_tok):
    *maybe_ob_ref, ybuf, sem = rest
    i = pl.program_id(0)
    tile = o_ref.shape[0]
    b = lax.rem(i, 2)
    tile_rows = pl.ds(0, tile * TOKEN_ROWS)

    def gather_one(step, slot, r):
        for k in range(TOP_K):
            p = pos_ref[k * n_tok + row0 + step * tile + r]
            _token_copy(y_hbm, p * TOKEN_PITCH, ybuf.at[slot, k], r * TOKEN_PITCH,
                        sem.at[slot]).start(priority=k)

    def wait(slot):
        for k in range(TOP_K):
            pltpu.make_async_copy(y_hbm.at[tile_rows], ybuf.at[slot, k].at[tile_rows],
                                  sem.at[slot]).wait()

    @pl.when(i == 0)
    def _():
        def body(grp, c):
            for u in range(DMA_UNROLL):
                gather_one(0, 0, grp * DMA_UNROLL + u)
            return c

        lax.fori_loop(0, tile // DMA_UNROLL, body, 0)

    wait(b)
    nxt = jnp.minimum(i + 1, pl.num_programs(0) - 1)
    pending = iter(range(tile))

    def issue(count):
        for _ in range(count):
            r = next(pending, None)
            if r is not None:
                gather_one(nxt, 1 - b, r)

    per_piece = pl.cdiv(tile, 3 * TOKEN_ROWS)
    w0 = _col_from_row(wt_ref[0:1, :])
    w1 = _col_from_row(wt_ref[1:2, :])
    chunk = lambda ref, c, pitch: ref[pl.ds(c, tile, stride=pitch), :]
    z = []
    for c in range(TOKEN_ROWS):
        z.append(ALPHA * chunk(x_ref, c, TOKEN_PITCH)
                 + w0 * chunk(ybuf.at[b, 0], c, TOKEN_PITCH)
                 + w1 * chunk(ybuf.at[b, 1], c, TOKEN_PITCH))
        issue(per_piece)
    d = TOKEN_ROWS * LANES
    total = z[0]
    for c in range(1, TOKEN_ROWS):
        total = total + z[c]
    mu = jnp.sum(total, axis=-1, keepdims=True) * (1.0 / d)
    sq = None
    for c in range(TOKEN_ROWS):
        z[c] = z[c] - mu
        sq = z[c] * z[c] if sq is None else sq + z[c] * z[c]
        issue(per_piece)
    rstd = lax.rsqrt(jnp.sum(sq, axis=-1, keepdims=True) * (1.0 / d) + LN_EPS)
    for c in range(TOKEN_ROWS):
        cols = slice(c * LANES, (c + 1) * LANES)
        out = z[c] * rstd * g_ref[:, cols] + b_ref[:, cols]
        o_ref[:, cols] = out
        for ob_ref in maybe_ob_ref:
            ob_ref[:, cols] = out.astype(BF16)
        issue(per_piece)
    issue(tile)

    @pl.when(i == pl.num_programs(0) - 1)
    def _():
        wait(1 - b)


def _combine_ln(pos, x1t, y_sorted, wt, ln_g, ln_b, *, row0, m, n_tok, with_bf16,
                tile=COMBINE_TILE):
    d = ln_g.shape[1]
    tile = min(tile, m)
    assert row0 % tile == 0
    vec = pl.BlockSpec((1, d), lambda i, pos: (0, 0))
    out = pl.BlockSpec((tile, d), lambda i, pos: (i, 0))
    dtypes = (F32, BF16) if with_bf16 else (F32,)
    return pl.pallas_call(
        functools.partial(_combine_kernel, row0=row0, n_tok=n_tok),
        out_shape=tuple(jax.ShapeDtypeStruct((m, d), dt) for dt in dtypes),
        grid_spec=pltpu.PrefetchScalarGridSpec(
            num_scalar_prefetch=1,
            grid=(m // tile,),
            in_specs=[pl.BlockSpec((tile * TOKEN_PITCH, LANES),
                                   lambda i, pos: (row0 // tile + i, 0)),
                      pl.BlockSpec(memory_space=pl.ANY),
                      pl.BlockSpec((SUBLANES, tile), lambda i, pos: (0, i)), vec, vec],
            out_specs=tuple(out for _ in dtypes),
            scratch_shapes=[pltpu.VMEM((2, TOP_K, tile * TOKEN_PITCH, LANES), F32),
                            pltpu.SemaphoreType.DMA((2,))]),
        compiler_params=_cparams("arbitrary"),
        name="moe_combine_ln",
    )(pos, x1t, y_sorted, wt, ln_g, ln_b)


def _sortpos_kernel(e_ref, pos_ref, cnt_ref, cum_ref, *, chunk):
    n_chunks = e_ref.shape[1] // chunk
    tri = (lax.broadcasted_iota(I32, (chunk, chunk), 0)
           <= lax.broadcasted_iota(I32, (chunk, chunk), 1)).astype(BF16)
    row = lax.broadcasted_iota(I32, (N_EXPERTS, chunk), 0)
    hot = lambda c: row == e_ref[:, c * chunk:(c + 1) * chunk]
    starts = []
    running = jnp.zeros((N_EXPERTS, LANES), F32)
    for c in range(n_chunks):
        cum = _dot(hot(c).astype(BF16), tri)
        cum_ref[c] = cum
        starts.append(running)
        running = running + cum[:, chunk - 1:chunk]
    cnt_ref[...] = running.astype(I32)
    first_row = _seg_cumsum(running, N_EXPERTS) - running
    for c in range(n_chunks):
        before = (first_row + starts[c])[:, 0:1] + cum_ref[c]
        pos_ref[:, c * chunk:(c + 1) * chunk] = (
            jnp.sum(jnp.where(hot(c), before, 0.0), axis=0, keepdims=True) - 1.0).astype(I32)


def _sort_positions(flat, *, chunk=512):
    n = flat.shape[0]
    n_chunks = pl.cdiv(n, chunk)
    e = jnp.pad(flat, (0, n_chunks * chunk - n), constant_values=N_EXPERTS)
    pos, cnt = pl.pallas_call(
        functools.partial(_sortpos_kernel, chunk=chunk),
        out_shape=(jax.ShapeDtypeStruct((1, n_chunks * chunk), I32),
                   jax.ShapeDtypeStruct((N_EXPERTS, LANES), I32)),
        scratch_shapes=[pltpu.VMEM((n_chunks, N_EXPERTS, chunk), F32)],
        compiler_params=pltpu.CompilerParams(vmem_limit_bytes=VMEM_LIMIT_BYTES),
        name="moe_sort_positions",
    )(e.reshape(1, n_chunks * chunk))
    return pos[0, :n], cnt[:, 0]


def _route_plan(eid, tile, n_prompt):
    n_tok = eid.shape[1]
    pos, counts = _sort_positions(eid.reshape(-1))
    ends = jnp.cumsum(counts)
    offs = ends - counts
    first_tile = offs // tile
    last_tile = (ends - 1) // tile
    nvis = jnp.where(counts > 0, last_tile - first_tile + 1, 0)
    vend = jnp.cumsum(nvis)
    vbase = vend - nvis
    nact = vend[-1]
    n_visits = pl.cdiv(TOP_K * n_tok, tile) + N_EXPERTS - 1
    v = jnp.minimum(jnp.arange(n_visits, dtype=I32), nact - 1)
    e = jnp.sum((vend[None, :] <= v[:, None]).astype(I32), axis=1)
    hot = (e[:, None] == jnp.arange(N_EXPERTS, dtype=I32)[None, :]).astype(I32)
    look = lambda table: jnp.sum(hot * table[None, :].astype(I32), axis=1)
    t = look(first_tile) + (v - look(vbase))
    assert n_prompt % INVERT_UNROLL == 0 and (n_tok - n_prompt) % INVERT_UNROLL == 0
    counts_smem = jnp.stack([nact.astype(I32),
                             jnp.asarray(n_prompt // INVERT_UNROLL, I32),
                             jnp.asarray((n_tok - n_prompt) // INVERT_UNROLL, I32)])
    plan = (t.astype(I32), e.astype(I32), look(offs), look(ends), counts_smem)
    return pos, plan


def kernel(x_prompt, x_sample, state_conv, state_hgrn, cache_k_win, cache_v_win, w_in, w_out, conv_w, hg_lb_param, hg_gain, attn_sinks, ln1_g, ln1_b, w_group, b_group, w_router, b_router, w_gate, w_up, w_down, ln2_g, ln2_b):
    bsz, seq, d = x_prompt.shape
    nb = x_sample.shape[0]
    n_prompt = bsz * seq
    n_tok = n_prompt + nb
    w_buf = cache_k_win.shape[2]

    xp = x_prompt.reshape(n_prompt, d)
    xp_mm = xp
    xs = x_sample.reshape(nb, d)
    pad_rows = ROUTE_ROWS - N_GROUPS - N_EXPERTS
    w_out_b = w_out.astype(BF16)
    outs = {k: [] for k in ("cp", "cs", "hp", "hs", "kp", "ks", "vp", "vs")}
    for l in range(DEPTH):
        gain = hg_gain[l].reshape(1, HG_DK)
        sinks = attn_sinks[l].reshape(1, N_HEADS)
        rw_t = jnp.concatenate([w_group[l].T, w_router[l].T, jnp.zeros((pad_rows, d), F32)], axis=0)
        rb = jnp.concatenate([b_group[l], b_router[l], jnp.zeros((pad_rows,), F32)]).reshape(ROUTE_ROWS, 1)
        g1, b1 = ln1_g[l].reshape(1, d), ln1_b[l].reshape(1, d)
        g2, b2 = ln2_g[l].reshape(1, d), ln2_b[l].reshape(1, d)

        proj = _inproj(xp_mm, w_in, l, tm=2048, tn=512)
        ya, ctail = _conv_prompt(proj, conv_w[l], bsz, seq)
        yb, hstate = _hgrn_prompt(proj, hg_lb_param, gain, l, bsz, seq)
        yc = _attn_prompt(proj, sinks, bsz, seq)
        x1t, eid_p, ewt_p = _outproj_ln_route(
            [ya, yb, yc], xp, w_out_b, l, g1, b1, rw_t, rb, precise=False,
            x1_rows=n_tok, tn=d)
        outs["cp"].append(ctail[:, SUBLANES - (CONV_W - 1):])
        outs["hp"].append(hstate)
        kv = proj.reshape(bsz, seq, IN_DIM)[:, seq - w_buf:]
        outs["kp"].append(kv[:, :, OFF_AK:OFF_AK + KV_DIM].reshape(bsz, w_buf, N_KV_HEADS, HEAD_DIM))
        outs["vp"].append(kv[:, :, OFF_AV:OFF_AV + KV_DIM].reshape(bsz, w_buf, N_KV_HEADS, HEAD_DIM))

        proj_s = _inproj3(xs, w_in, l)
        ysm, cst, hst, kst, vst = _sample_mixers(
            proj_s, state_conv, state_hgrn, cache_k_win, cache_v_win,
            conv_w[l], hg_lb_param, gain, sinks, l)
        x1t, eid_s, ewt_s = _outproj_ln_route(
            [ysm.reshape(nb, d)], xs, w_out, l, g1, b1, rw_t, rb, precise=True,
            x1_rows=n_tok, x1_row0=n_prompt, x1_buf=x1t)
        outs["cs"].append(cst)
        outs["hs"].append(hst)
        outs["ks"].append(kst.reshape(nb, w_buf, N_KV_HEADS, HEAD_DIM))
        outs["vs"].append(vst.reshape(nb, w_buf, N_KV_HEADS, HEAD_DIM))

        eid = jnp.concatenate([eid_p[:TOP_K], eid_s[:TOP_K]], axis=1)
        pos, plan = _route_plan(eid, EXPERT_TILE, n_prompt)
        y_sorted = _expert_ffn(pos, plan, x1t, w_gate, w_up, w_down, l,
                               n_prompt=n_prompt, n_sample=nb)
        xp, *xp_bf16 = _combine_ln(pos, x1t, y_sorted, ewt_p, g2, b2, row0=0, m=n_prompt,
                                   n_tok=n_tok, with_bf16=l + 1 < DEPTH)
        xp_mm = xp_bf16[0] if xp_bf16 else xp
        xs, = _combine_ln(pos, x1t, y_sorted, ewt_s, g2, b2, row0=n_prompt, m=nb,
                          n_tok=n_tok, with_bf16=False)

    st = lambda k: jnp.stack(outs[k])
    return (xp.reshape(bsz, seq, d), xs.reshape(nb, 1, d), st("cp"), st("cs"),
            st("hp"), st("hs"), st("kp"), st("ks"), st("vp"), st("vs"))
```
